```python
import jax, jax.numpy as jnp
from jax import lax
import numpy as np

D_MODEL = 1024
BATCH = 8
SEQ = 2048
DEPTH = 2
DEC_BATCH = 128
DEC_SEQ = 1
PAST_LEN = 16384
PAGE_SIZE = 128

N_MIXERS = 2
N_POOL_LAYERS = (DEPTH + N_MIXERS - 1) // N_MIXERS
N_CONV_LAYERS = DEPTH // N_MIXERS
POOL_WINDOWS = (2, 4, 8, 16)
N_POOL_GROUPS = len(POOL_WINDOWS)
POOL_GROUP = D_MODEL // N_POOL_GROUPS
POOL_STATE = max(POOL_WINDOWS) - 1
CONV_WIDTH = 3
CONV_STATE = CONV_WIDTH - 1
D_CONV = D_MODEL
D_FF = 4 * D_MODEL
N_MEM = 256
MEM_HEADS = 4
MEM_HEAD_DIM = D_MODEL // MEM_HEADS
EPS = 1e-6

kernel_name = "hybrid_pool_shortconv_memxattn_step"


def rmsnorm(x, g):
    xf = x.astype(jnp.float32)
    y = xf * lax.rsqrt(jnp.mean(xf * xf, axis=-1, keepdims=True) + EPS)
    return (y * g.astype(jnp.float32)).astype(x.dtype)


def pool_mixer(u, prev, start_pos, w_pool, scale):
    b, t, d = u.shape
    ext = jnp.concatenate([prev, u], axis=1)
    c = jnp.cumsum(ext.astype(jnp.float32), axis=1)
    c = jnp.pad(c, ((0, 0), (1, 0), (0, 0)))
    end = c[:, POOL_STATE + 1:POOL_STATE + 1 + t]
    pos = start_pos + jnp.arange(t)
    means = []
    for g, w in enumerate(POOL_WINDOWS):
        sl = slice(g * POOL_GROUP, (g + 1) * POOL_GROUP)
        begin = c[:, POOL_STATE + 1 - w:POOL_STATE + 1 - w + t, sl]
        cnt = jnp.minimum(pos + 1, w).astype(jnp.float32)[None, :, None]
        means.append((end[..., sl] - begin) / cnt)
    pooled = jnp.concatenate(means, axis=-1) - u.astype(jnp.float32)
    pooled = pooled.reshape(b, t, N_POOL_GROUPS, POOL_GROUP).astype(u.dtype)
    y = jnp.einsum('btgc,gce->btge', pooled, w_pool).reshape(b, t, d)
    return y * scale, ext[:, -POOL_STATE:]


def conv_mixer(u, prev, w_in, w_conv, w_out):
    t = u.shape[1]
    bch = jnp.einsum('btd,de->bte', u, w_in)
    gate_b, gate_c, h = jnp.split(bch, 3, axis=-1)
    ext = jnp.concatenate([prev, gate_c * h], axis=1)
    conv = ext[:, 0:t] * w_conv[0]
    for k in range(1, CONV_WIDTH):
        conv = conv + ext[:, k:k + t] * w_conv[k]
    y = jnp.einsum('bte,ed->btd', gate_b * conv, w_out)
    return y, ext[:, -CONV_STATE:]


def mem_kv(mem, g_mem, w_kv):
    b, n, _ = mem.shape
    kv = jnp.einsum('bnd,de->bne', rmsnorm(mem, g_mem), w_kv)
    k, v = jnp.split(kv, 2, axis=-1)
    return (k.reshape(b, n, MEM_HEADS, MEM_HEAD_DIM), v.reshape(b, n, MEM_HEADS, MEM_HEAD_DIM))


def mem_attend(u, k, v, w_q, w_o):
    b, t, d = u.shape
    q = jnp.einsum('btd,de->bte', u, w_q).reshape(b, t, MEM_HEADS, MEM_HEAD_DIM)
    s = jnp.einsum('bthd,bnhd->bhtn', q, k).astype(jnp.float32) * (MEM_HEAD_DIM ** -0.5)
    p = jax.nn.softmax(s, axis=-1).astype(v.dtype)
    o = jnp.einsum('bhtn,bnhd->bthd', p, v).reshape(b, t, d)
    return jnp.einsum('btd,de->bte', o, w_o)


def sq_relu_mlp(u, w_up, w_down):
    h = jax.nn.relu(jnp.einsum('btd,df->btf', u, w_up))
    return jnp.einsum('btf,fd->btd', h * h, w_down)


def trunk(x, pool_prev, conv_prev, mem_k, mem_v, start_pos,
          g_mix, g_attn, g_ffn, g_final, w_pool, pool_scale,
          w_conv_in, w_conv, w_conv_out, w_q, w_o, w_up, w_down):
    new_pool, new_conv = [], []
    for i in range(DEPTH):
        j = i // N_MIXERS
        u = rmsnorm(x, g_mix[i])
        if i % N_MIXERS == 0:
            y, st = pool_mixer(u, pool_prev[j], start_pos, w_pool[j], pool_scale[j])
            new_pool.append(st)
        else:
            y, st = conv_mixer(u, conv_prev[j], w_conv_in[j], w_conv[j], w_conv_out[j])
            new_conv.append(st)
        x = x + y
        x = x + mem_attend(rmsnorm(x, g_attn[i]), mem_k[i], mem_v[i], w_q[i], w_o[i])
        x = x + sq_relu_mlp(rmsnorm(x, g_ffn[i]), w_up[i], w_down[i])
    return rmsnorm(x, g_final), jnp.stack(new_pool), jnp.stack(new_conv)


def setup_inputs(seed: int = 0) -> dict:
    key = jax.random.key(seed)
    ks = jax.random.split(key, 24)
    f32 = jnp.float32
    nrm = lambda k, shape, s: jax.random.normal(k, shape, f32) * s
    gain = lambda k, shape: 1.0 + 0.05 * jax.random.normal(k, shape, f32)
    mem_shape = (DEPTH, DEC_BATCH, N_MEM, MEM_HEADS, MEM_HEAD_DIM)
    return {
        "x_prompt": nrm(ks[0], (BATCH, SEQ, D_MODEL), 1.0),
        "x_sample": nrm(ks[1], (DEC_BATCH, DEC_SEQ, D_MODEL), 1.0),
        "state_pool": nrm(ks[2], (N_POOL_LAYERS, DEC_BATCH, POOL_STATE, D_MODEL), 1.0),
        "state_conv": nrm(ks[3], (N_CONV_LAYERS, DEC_BATCH, CONV_STATE, D_CONV), 1.0),
        "cache_mem_k": nrm(ks[4], mem_shape, 1.0),
        "cache_mem_v": nrm(ks[5], mem_shape, 1.0),
        "mem_prompt": nrm(ks[6], (BATCH, N_MEM, D_MODEL), 1.0),
        "g_mix": gain(ks[7], (DEPTH, D_MODEL)),
        "g_attn": gain(ks[8], (DEPTH, D_MODEL)),
        "g_mem": gain(ks[9], (DEPTH, D_MODEL)),
        "g_ffn": gain(ks[10], (DEPTH, D_MODEL)),
        "g_final": gain(ks[11], (D_MODEL,)),
        "w_pool": nrm(ks[12], (N_POOL_LAYERS, N_POOL_GROUPS, POOL_GROUP, POOL_GROUP), POOL_GROUP ** -0.5),
        "pool_scale": 0.5 + 0.1 * jax.random.normal(ks[13], (N_POOL_LAYERS, D_MODEL), f32),
        "w_conv_in": nrm(ks[14], (N_CONV_LAYERS, D_MODEL, 3 * D_CONV), D_MODEL ** -0.5),
        "w_conv": nrm(ks[15], (N_CONV_LAYERS, CONV_WIDTH, D_CONV), CONV_WIDTH ** -0.5),
        "w_conv_out": nrm(ks[16], (N_CONV_LAYERS, D_CONV, D_MODEL), D_CONV ** -0.5),
        "w_q": nrm(ks[17], (DEPTH, D_MODEL, D_MODEL), D_MODEL ** -0.5),
        "w_kv": nrm(ks[18], (DEPTH, D_MODEL, 2 * D_MODEL), D_MODEL ** -0.5),
        "w_o": nrm(ks[19], (DEPTH, D_MODEL, D_MODEL), D_MODEL ** -0.5),
        "w_up": nrm(ks[20], (DEPTH, D_MODEL, D_FF), D_MODEL ** -0.5),
        "w_down": nrm(ks[21], (DEPTH, D_FF, D_MODEL), D_FF ** -0.5),
    }


def reference(x_prompt, x_sample, state_pool, state_conv, cache_mem_k, cache_mem_v, mem_prompt,
              g_mix, g_attn, g_mem, g_ffn, g_final, w_pool, pool_scale,
              w_conv_in, w_conv, w_conv_out, w_q, w_kv, w_o, w_up, w_down):
    kvs = [mem_kv(mem_prompt, g_mem[i], w_kv[i]) for i in range(DEPTH)]
    mem_k_prompt = jnp.stack([kv[0] for kv in kvs])
    mem_v_prompt = jnp.stack([kv[1] for kv in kvs])
    b = x_prompt.shape[0]
    pool0 = jnp.zeros((N_POOL_LAYERS, b, POOL_STATE, D_MODEL), x_prompt.dtype)
    conv0 = jnp.zeros((N_CONV_LAYERS, b, CONV_STATE, D_CONV), x_prompt.dtype)
    y_prompt, new_pool_prompt, new_conv_prompt = trunk(
        x_prompt, pool0, conv0, mem_k_prompt, mem_v_prompt, 0,
        g_mix, g_attn, g_ffn, g_final, w_pool, pool_scale,
        w_conv_in, w_conv, w_conv_out, w_q, w_o, w_up, w_down)
    y_sample, new_pool_sample, new_conv_sample = trunk(
        x_sample, state_pool, state_conv, cache_mem_k, cache_mem_v, PAST_LEN,
        g_mix, g_attn, g_ffn, g_final, w_pool, pool_scale,
        w_conv_in, w_conv, w_conv_out, w_q, w_o, w_up, w_down)
    return (y_prompt, y_sample, new_pool_prompt, new_conv_prompt, mem_k_prompt, mem_v_prompt,
            new_pool_sample, new_conv_sample)
```

```python
import functools

import jax
import jax.numpy as jnp
from jax import lax
from jax.experimental import pallas as pl
from jax.experimental.pallas import tpu as pltpu

D_MODEL = 1024
DEPTH = 2
PAST_LEN = 16384
POOL_WINDOWS = (2, 4, 8, 16)
POOL_GROUP = D_MODEL // len(POOL_WINDOWS)
POOL_STATE = max(POOL_WINDOWS) - 1
CONV_WIDTH = 3
CONV_STATE = CONV_WIDTH - 1
D_FF = 4 * D_MODEL
N_MEM = 256
MEM_HEADS = 4
MEM_HEAD_DIM = D_MODEL // MEM_HEADS
EPS = 1e-6

SUBLANES = 8
POOL_HALO = 16
CONV_HALO = SUBLANES
ROW_TILE = 256
KV_ROW_TILE = 512
FF_CHUNK = 1024
SAMPLE_BLOCK = 8
VMEM_LIMIT = 56 * 1024 * 1024

F32 = jnp.float32
BF16 = jnp.bfloat16

_dot = functools.partial(jnp.dot, preferred_element_type=F32)


def _rms(x, g):
    y = x * lax.rsqrt(jnp.mean(x * x, axis=-1, keepdims=True) + EPS)
    return y * g


def _vmem():
    return pl.BlockSpec(memory_space=pltpu.VMEM)


def _mlp(x, g_ffn, wup_ref, wdown_ref):
    u = _rms(x, g_ffn).astype(BF16)
    acc = x
    for c in range(D_FF // FF_CHUNK):
        sl = slice(c * FF_CHUNK, (c + 1) * FF_CHUNK)
        h = jnp.maximum(_dot(u, wup_ref[:, sl]), 0.0)
        acc = acc + _dot((h * h).astype(BF16), wdown_ref[sl, :])
    return acc


def _prompt_attend(x, g_attn, k_ref, v_ref, wq_ref, wo_ref):
    u = _rms(x, g_attn).astype(BF16)
    q = _dot(u, wq_ref[...]) * (MEM_HEAD_DIM ** -0.5)
    heads = []
    for h in range(MEM_HEADS):
        sl = slice(h * MEM_HEAD_DIM, (h + 1) * MEM_HEAD_DIM)
        s = lax.dot_general(q[:, sl].astype(BF16), k_ref[:, sl],
                            (((1,), (1,)), ((), ())), preferred_element_type=F32)
        e = jnp.exp(s - jnp.max(s, axis=-1, keepdims=True))
        inv = 1.0 / jnp.sum(e, axis=-1, keepdims=True)
        heads.append((_dot(e.astype(BF16), v_ref[:, sl]) * inv).astype(BF16))
    o = jnp.concatenate(heads, axis=-1)
    return x + _dot(o, wo_ref[...])


def _conv_tap(wconv_ref, back):
    k = CONV_WIDTH - 1 - back
    return wconv_ref[k:k + 1, :]


def _pool_project(pooled, wpool_ref, scale):
    ys = []
    for g in range(len(POOL_WINDOWS)):
        sl = slice(g * POOL_GROUP, (g + 1) * POOL_GROUP)
        ys.append(_dot(pooled[:, sl].astype(BF16), wpool_ref[g]))
    return jnp.concatenate(ys, axis=-1) * scale


def _memkv_kernel(m_ref, g_ref, w_ref, k_ref, v_ref, kb_ref, vb_ref):
    u = _rms(m_ref[...], g_ref[...]).astype(BF16)
    kv = _dot(u, w_ref[...])
    k = kv[:, :D_MODEL]
    v = kv[:, D_MODEL:]
    k_ref[...] = k
    v_ref[...] = v
    kb_ref[...] = k.astype(BF16)
    vb_ref[...] = v.astype(BF16)


def _mem_kv(mem, g_mem, w_kv):
    rows = mem.shape[0]
    out = jax.ShapeDtypeStruct((DEPTH, rows, D_MODEL), F32)
    outb = jax.ShapeDtypeStruct((DEPTH, rows, D_MODEL), BF16)
    ospec = pl.BlockSpec((None, KV_ROW_TILE, D_MODEL), lambda l, r: (l, r, 0))
    return pl.pallas_call(
        _memkv_kernel,
        grid=(DEPTH, rows // KV_ROW_TILE),
        in_specs=[
            pl.BlockSpec((KV_ROW_TILE, D_MODEL), lambda l, r: (r, 0)),
            pl.BlockSpec((None, 1, D_MODEL), lambda l, r: (l, 0, 0)),
            pl.BlockSpec((None, D_MODEL, 2 * D_MODEL), lambda l, r: (l, 0, 0)),
        ],
        out_specs=[ospec, ospec, ospec, ospec],
        out_shape=[out, out, outb, outb],
        compiler_params=pltpu.CompilerParams(
            dimension_semantics=("arbitrary", "arbitrary"), vmem_limit_bytes=VMEM_LIMIT),
        name="mem_kv",
    )(mem, g_mem.reshape(DEPTH, 1, D_MODEL), w_kv)


def _pool_layer_kernel(x_ref, gm_ref, ga_ref, gf_ref, k_ref, v_ref, wpool_ref, scale_ref,
                       wq_ref, wo_ref, wup_ref, wdown_ref, y_ref, state_ref, prev_ref):
    t = pl.program_id(1)
    tm = x_ref.shape[0]

    @pl.when(t == 0)
    def _():
        prev_ref[...] = jnp.zeros_like(prev_ref)

    x = x_ref[...]
    u = _rms(x, gm_ref[...])
    ext = jnp.concatenate([prev_ref[...], u], axis=0)
    pos = t * tm + lax.broadcasted_iota(jnp.int32, (tm, 1), 0)
    pooled = []
    for g, w in enumerate(POOL_WINDOWS):
        sl = slice(g * POOL_GROUP, (g + 1) * POOL_GROUP)
        s = ext[:, sl]
        shift = 1
        while shift < w:
            s = s + pltpu.roll(s, shift, axis=0)
            shift *= 2
        inv_cnt = 1.0 / jnp.minimum(pos + 1, w).astype(F32)
        pooled.append(s[POOL_HALO:] * inv_cnt - u[:, sl])
    pooled = jnp.concatenate(pooled, axis=-1)
    x = x + _pool_project(pooled, wpool_ref, scale_ref[...])
    prev_ref[...] = u[tm - POOL_HALO:]

    @pl.when(t == pl.num_programs(1) - 1)
    def _():
        state_ref[...] = u[tm - POOL_STATE:]

    x = _prompt_attend(x, ga_ref[...], k_ref, v_ref, wq_ref, wo_ref)
    y_ref[...] = _mlp(x, gf_ref[...], wup_ref, wdown_ref)


def _conv_layer_kernel(x_ref, gm_ref, ga_ref, gf_ref, gfin_ref, k_ref, v_ref, win_ref, wconv_ref,
                       wout_ref, wq_ref, wo_ref, wup_ref, wdown_ref, y_ref, state_ref, prev_ref):
    t = pl.program_id(1)
    tm = x_ref.shape[0]

    @pl.when(t == 0)
    def _():
        prev_ref[...] = jnp.zeros_like(prev_ref)

    x = x_ref[...]
    u = _rms(x, gm_ref[...]).astype(BF16)
    gate_b = _dot(u, win_ref[:, :D_MODEL])
    ch = _dot(u, win_ref[:, D_MODEL:2 * D_MODEL]) * _dot(u, win_ref[:, 2 * D_MODEL:])
    ext = jnp.concatenate([prev_ref[...], ch], axis=0)
    conv = ext * _conv_tap(wconv_ref, 0)
    for back in range(1, CONV_WIDTH):
        conv = conv + pltpu.roll(ext, back, axis=0) * _conv_tap(wconv_ref, back)
    conv = conv[CONV_HALO:]
    x = x + _dot((gate_b * conv).astype(BF16), wout_ref[...])
    prev_ref[...] = ch[tm - CONV_HALO:]

    @pl.when(t == pl.num_programs(1) - 1)
    def _():
        state_ref[...] = ch[tm - CONV_STATE:]

    x = _prompt_attend(x, ga_ref[...], k_ref, v_ref, wq_ref, wo_ref)
    x = _mlp(x, gf_ref[...], wup_ref, wdown_ref)
    y_ref[...] = _rms(x, gfin_ref[...])


def _prompt_layer(kernel, name, x, kb, vb, small, weights, state_rows, halo):
    batch, seq, _ = x.shape
    xspec = pl.BlockSpec((None, ROW_TILE, D_MODEL), lambda b, t: (b, t, 0))
    kvspec = pl.BlockSpec((None, N_MEM, D_MODEL), lambda b, t: (b, 0, 0))
    return pl.pallas_call(
        kernel,
        grid=(batch, seq // ROW_TILE),
        in_specs=[xspec] + [_vmem()] * len(small) + [kvspec, kvspec] + [_vmem()] * len(weights),
        out_specs=[xspec, pl.BlockSpec((None, state_rows, D_MODEL), lambda b, t: (b, 0, 0))],
        out_shape=[jax.ShapeDtypeStruct(x.shape, F32),
                   jax.ShapeDtypeStruct((batch, state_rows, D_MODEL), F32)],
        scratch_shapes=[pltpu.VMEM((halo, D_MODEL), F32)],
        compiler_params=pltpu.CompilerParams(
            dimension_semantics=("arbitrary", "arbitrary"), vmem_limit_bytes=VMEM_LIMIT),
        name=name,
    )(x, *small, kb, vb, *weights)


def _sample_pool_pre_kernel(x_ref, st_ref, gm_ref, ga_ref, wpool_ref, scale_ref, wq_ref,
                            x1_ref, q_ref, newst_ref):
    x = x_ref[...]
    u = _rms(x, gm_ref[...])
    pooled = []
    for g, w in enumerate(POOL_WINDOWS):
        s = u[:, g * POOL_GROUP:(g + 1) * POOL_GROUP]
        for back in range(1, w):
            lo = (POOL_STATE - back) * D_MODEL + g * POOL_GROUP
            s = s + st_ref[:, lo:lo + POOL_GROUP]
        cnt = min(PAST_LEN + 1, w)
        pooled.append(s * (1.0 / cnt) - u[:, g * POOL_GROUP:(g + 1) * POOL_GROUP])
    pooled = jnp.concatenate(pooled, axis=-1)
    x = x + _pool_project(pooled, wpool_ref, scale_ref[...])
    x1_ref[...] = x
    q_ref[...] = _dot(_rms(x, ga_ref[...]).astype(BF16), wq_ref[...]) * (MEM_HEAD_DIM ** -0.5)
    keep = (POOL_STATE - 1) * D_MODEL
    newst_ref[:, :keep] = st_ref[:, D_MODEL:]
    newst_ref[:, keep:] = u


def _sample_conv_pre_kernel(x_ref, st_ref, gm_ref, ga_ref, win_ref, wconv_ref, wout_ref, wq_ref,
                            x1_ref, q_ref, newst_ref):
    x = x_ref[...]
    u = _rms(x, gm_ref[...]).astype(BF16)
    gate_b = _dot(u, win_ref[:, :D_MODEL])
    ch = _dot(u, win_ref[:, D_MODEL:2 * D_MODEL]) * _dot(u, win_ref[:, 2 * D_MODEL:])
    conv = ch * _conv_tap(wconv_ref, 0)
    for back in range(1, CONV_WIDTH):
        lo = (CONV_STATE - back) * D_MODEL
        conv = conv + st_ref[:, lo:lo + D_MODEL] * _conv_tap(wconv_ref, back)
    x = x + _dot((gate_b * conv).astype(BF16), wout_ref[...])
    x1_ref[...] = x
    q_ref[...] = _dot(_rms(x, ga_ref[...]).astype(BF16), wq_ref[...]) * (MEM_HEAD_DIM ** -0.5)
    keep = (CONV_STATE - 1) * D_MODEL
    newst_ref[:, :keep] = st_ref[:, D_MODEL:]
    newst_ref[:, keep:] = ch


def _sample_pre(kernel, name, x, state, small, weights):
    n = x.shape[0]
    act = jax.ShapeDtypeStruct((n, D_MODEL), F32)
    return pl.pallas_call(
        kernel,
        out_shape=[act, act, jax.ShapeDtypeStruct(state.shape, F32)],
        compiler_params=pltpu.CompilerParams(vmem_limit_bytes=VMEM_LIMIT),
        name=name,
    )(x, state, *small, *weights)


def _sample_attn_kernel(q_ref, k_ref, v_ref, o_ref):
    for j in range(SAMPLE_BLOCK):
        prod = k_ref[j] * q_ref[j:j + 1, :]
        for h in range(MEM_HEADS):
            sl = slice(h * MEM_HEAD_DIM, (h + 1) * MEM_HEAD_DIM)
            s = jnp.sum(prod[:, sl], axis=-1, keepdims=True)
            e = jnp.exp(s - jnp.max(s, axis=0, keepdims=True))
            inv = 1.0 / jnp.sum(e, axis=0, keepdims=True)
            o_ref[j:j + 1, sl] = jnp.sum(e * v_ref[j, :, sl], axis=0, keepdims=True) * inv


def _sample_attn(q, k, v, layer, name):
    n = q.shape[0]
    qspec = pl.BlockSpec((SAMPLE_BLOCK, D_MODEL), lambda i: (i, 0))
    kvspec = pl.BlockSpec((None, SAMPLE_BLOCK, N_MEM, D_MODEL), lambda i: (layer, i, 0, 0))
    return pl.pallas_call(
        _sample_attn_kernel,
        grid=(n // SAMPLE_BLOCK,),
        in_specs=[qspec, kvspec, kvspec],
        out_specs=qspec,
        out_shape=jax.ShapeDtypeStruct((n, D_MODEL), F32),
        compiler_params=pltpu.CompilerParams(
            dimension_semantics=("arbitrary",), vmem_limit_bytes=VMEM_LIMIT),
        name=name,
    )(q, k, v)


def _sample_post_kernel(final, x_ref, o_ref, gf_ref, gfin_ref, wo_ref, wup_ref, wdown_ref, y_ref):
    x = x_ref[...] + _dot(o_ref[...].astype(BF16), wo_ref[...])
    x = _mlp(x, gf_ref[...], wup_ref, wdown_ref)
    y_ref[...] = _rms(x, gfin_ref[...]) if final else x


def _sample_post(x1, o, g_ffn, g_final, wo, wup, wdown, final, name):
    return pl.pallas_call(
        functools.partial(_sample_post_kernel, final),
        out_shape=jax.ShapeDtypeStruct(x1.shape, F32),
        compiler_params=pltpu.CompilerParams(vmem_limit_bytes=VMEM_LIMIT),
        name=name,
    )(x1, o, g_ffn, g_final, wo, wup, wdown)


def kernel(x_prompt, x_sample, state_pool, state_conv, cache_mem_k, cache_mem_v, mem_prompt,
           g_mix, g_attn, g_mem, g_ffn, g_final, w_pool, pool_scale,
           w_conv_in, w_conv, w_conv_out, w_q, w_kv, w_o, w_up, w_down):
    batch, seq, _ = x_prompt.shape
    n_sample = x_sample.shape[0]
    assert DEPTH == 2 and seq % ROW_TILE == 0 and ROW_TILE >= POOL_HALO
    assert x_sample.shape[1] == 1 and n_sample % SAMPLE_BLOCK == 0
    assert (batch * N_MEM) % KV_ROW_TILE == 0

    row = lambda a: a.reshape(1, D_MODEL)
    wpool_b, win_b, wout_b = w_pool[0].astype(BF16), w_conv_in[0].astype(BF16), w_conv_out[0].astype(BF16)
    wq_b, wkv_b, wo_b = w_q.astype(BF16), w_kv.astype(BF16), w_o.astype(BF16)
    wup_b, wdown_b = w_up.astype(BF16), w_down.astype(BF16)
    scale0, wconv0, gfin = row(pool_scale[0]), w_conv[0], row(g_final)

    k_all, v_all, kb_all, vb_all = _mem_kv(mem_prompt.reshape(batch * N_MEM, D_MODEL), g_mem, wkv_b)
    kb_all = kb_all.reshape(DEPTH, batch, N_MEM, D_MODEL)
    vb_all = vb_all.reshape(DEPTH, batch, N_MEM, D_MODEL)
    x1, pool_p = _prompt_layer(
        _pool_layer_kernel, "prompt_pool_layer", x_prompt, kb_all[0], vb_all[0],
        [row(g_mix[0]), row(g_attn[0]), row(g_ffn[0])],
        [wpool_b, scale0, wq_b[0], wo_b[0], wup_b[0], wdown_b[0]], POOL_STATE, POOL_HALO)
    y_prompt, conv_p = _prompt_layer(
        _conv_layer_kernel, "prompt_conv_layer", x1, kb_all[1], vb_all[1],
        [row(g_mix[1]), row(g_attn[1]), row(g_ffn[1]), gfin],
        [win_b, wconv0, wout_b, wq_b[1], wo_b[1], wup_b[1], wdown_b[1]], CONV_STATE, CONV_HALO)

    xs = x_sample.reshape(n_sample, D_MODEL)
    ck = cache_mem_k.reshape(DEPTH, n_sample, N_MEM, D_MODEL)
    cv = cache_mem_v.reshape(DEPTH, n_sample, N_MEM, D_MODEL)
    xs1, q0, pool_s = _sample_pre(
        _sample_pool_pre_kernel, "sample_pool_pre", xs,
        state_pool[0].reshape(n_sample, POOL_STATE * D_MODEL),
        [row(g_mix[0]), row(g_attn[0])], [wpool_b, scale0, wq_b[0]])
    o0 = _sample_attn(q0, ck, cv, 0, "sample_attn0")
    xs2 = _sample_post(xs1, o0, row(g_ffn[0]), gfin, wo_b[0], wup_b[0], wdown_b[0], False, "sample_post0")
    xs3, q1, conv_s = _sample_pre(
        _sample_conv_pre_kernel, "sample_conv_pre", xs2,
        state_conv[0].reshape(n_sample, CONV_STATE * D_MODEL),
        [row(g_mix[1]), row(g_attn[1])], [win_b, wconv0, wout_b, wq_b[1]])
    o1 = _sample_attn(q1, ck, cv, 1, "sample_attn1")
    y_sample = _sample_post(xs3, o1, row(g_ffn[1]), gfin, wo_b[1], wup_b[1], wdown_b[1], True, "sample_post1")

    mem_shape = (DEPTH, batch, N_MEM, MEM_HEADS, MEM_HEAD_DIM)
    return (y_prompt,
            y_sample.reshape(n_sample, 1, D_MODEL),
            pool_p.reshape(1, batch, POOL_STATE, D_MODEL),
            conv_p.reshape(1, batch, CONV_STATE, D_MODEL),
            k_all.reshape(mem_shape),
            v_all.reshape(mem_shape),
            pool_s.reshape(1, n_sample, POOL_STATE, D_MODEL),
            conv_s.reshape(1, n_sample, CONV_STATE, D_MODEL))
```

```python
import functools

import jax
import jax.numpy as jnp
from jax import lax
from jax.experimental import pallas as pl
from jax.experimental.pallas import tpu as pltpu

D_MODEL = 1024
DEPTH = 2
PAST_LEN = 16384
POOL_WINDOWS = (2, 4, 8, 16)
POOL_GROUP = D_MODEL // len(POOL_WINDOWS)
POOL_STATE = max(POOL_WINDOWS) - 1
CONV_WIDTH = 3
CONV_STATE = CONV_WIDTH - 1
D_FF = 4 * D_MODEL
N_MEM = 256
MEM_HEADS = 4
MEM_HEAD_DIM = D_MODEL // MEM_HEADS
EPS = 1e-6

SUBLANES = 8
LANES = 128
HEAD_ROWS = D_MODEL // LANES
POOL_HALO = 16
CONV_HALO = SUBLANES
ROW_TILE = 256
KV_ROW_TILE = 512
FF_CHUNK = 1024
SAMPLE_BLOCK = 8
VMEM_LIMIT = 56 * 1024 * 1024

F32 = jnp.float32
BF16 = jnp.bfloat16

_dot = functools.partial(jnp.dot, preferred_element_type=F32)


def _rms(x, g):
    y = x * lax.rsqrt(jnp.mean(x * x, axis=-1, keepdims=True) + EPS)
    return y * g


def _vmem():
    return pl.BlockSpec(memory_space=pltpu.VMEM)


def _mlp(x, g_ffn, wup_ref, wdown_ref):
    u = _rms(x, g_ffn).astype(BF16)
    acc = x
    for c in range(D_FF // FF_CHUNK):
        sl = slice(c * FF_CHUNK, (c + 1) * FF_CHUNK)
        h = jnp.maximum(_dot(u, wup_ref[:, sl]), 0.0)
        acc = acc + _dot((h * h).astype(BF16), wdown_ref[sl, :])
    return acc


def _prompt_attend(x, g_attn, k_ref, v_ref, wq_ref, wo_ref):
    u = _rms(x, g_attn).astype(BF16)
    q = _dot(u, wq_ref[...]) * (MEM_HEAD_DIM ** -0.5)
    heads = []
    for h in range(MEM_HEADS):
        sl = slice(h * MEM_HEAD_DIM, (h + 1) * MEM_HEAD_DIM)
        s = lax.dot_general(q[:, sl].astype(BF16), k_ref[:, sl],
                            (((1,), (1,)), ((), ())), preferred_element_type=F32)
        e = jnp.exp(s - jnp.max(s, axis=-1, keepdims=True))
        inv = 1.0 / jnp.sum(e, axis=-1, keepdims=True)
        heads.append((_dot(e.astype(BF16), v_ref[:, sl]) * inv).astype(BF16))
    o = jnp.concatenate(heads, axis=-1)
    return x + _dot(o, wo_ref[...])


def _conv_tap(wconv_ref, back):
    k = CONV_WIDTH - 1 - back
    return wconv_ref[k:k + 1, :]


def _pool_project(pooled, wpool_ref, scale):
    ys = []
    for g in range(len(POOL_WINDOWS)):
        sl = slice(g * POOL_GROUP, (g + 1) * POOL_GROUP)
        ys.append(_dot(pooled[:, sl].astype(BF16), wpool_ref[g]))
    return jnp.concatenate(ys, axis=-1) * scale


def _memkv_kernel(m_ref, g_ref, w_ref, k_ref, v_ref, kb_ref, vb_ref):
    u = _rms(m_ref[...], g_ref[...]).astype(BF16)
    kv = _dot(u, w_ref[...])
    k = kv[:, :D_MODEL]
    v = kv[:, D_MODEL:]
    k_ref[...] = k
    v_ref[...] = v
    kb_ref[...] = k.astype(BF16)
    vb_ref[...] = v.astype(BF16)


def _mem_kv(mem, g_mem, w_kv):
    rows = mem.shape[0]
    out = jax.ShapeDtypeStruct((DEPTH, rows, D_MODEL), F32)
    outb = jax.ShapeDtypeStruct((DEPTH, rows, D_MODEL), BF16)
    ospec = pl.BlockSpec((None, KV_ROW_TILE, D_MODEL), lambda l, r: (l, r, 0))
    return pl.pallas_call(
        _memkv_kernel,
        grid=(DEPTH, rows // KV_ROW_TILE),
        in_specs=[
            pl.BlockSpec((KV_ROW_TILE, D_MODEL), lambda l, r: (r, 0)),
            pl.BlockSpec((None, 1, D_MODEL), lambda l, r: (l, 0, 0)),
            pl.BlockSpec((None, D_MODEL, 2 * D_MODEL), lambda l, r: (l, 0, 0)),
        ],
        out_specs=[ospec, ospec, ospec, ospec],
        out_shape=[out, out, outb, outb],
        compiler_params=pltpu.CompilerParams(
            dimension_semantics=("arbitrary", "arbitrary"), vmem_limit_bytes=VMEM_LIMIT),
        name="mem_kv",
    )(mem, g_mem.reshape(DEPTH, 1, D_MODEL), w_kv)


def _pool_layer_kernel(x_ref, gm_ref, ga_ref, gf_ref, k_ref, v_ref, wpool_ref, scale_ref,
                       wq_ref, wo_ref, wup_ref, wdown_ref, y_ref, state_ref, prev_ref):
    t = pl.program_id(1)
    tm = x_ref.shape[0]

    @pl.when(t == 0)
    def _():
        prev_ref[...] = jnp.zeros_like(prev_ref)

    x = x_ref[...]
    u = _rms(x, gm_ref[...])
    ext = jnp.concatenate([prev_ref[...], u], axis=0)
    pos = t * tm + lax.broadcasted_iota(jnp.int32, (tm, 1), 0)
    pooled = []
    for g, w in enumerate(POOL_WINDOWS):
        sl = slice(g * POOL_GROUP, (g + 1) * POOL_GROUP)
        s = ext[:, sl]
        shift = 1
        while shift < w:
            s = s + pltpu.roll(s, shift, axis=0)
            shift *= 2
        inv_cnt = 1.0 / jnp.minimum(pos + 1, w).astype(F32)
        pooled.append(s[POOL_HALO:] * inv_cnt - u[:, sl])
    pooled = jnp.concatenate(pooled, axis=-1)
    x = x + _pool_project(pooled, wpool_ref, scale_ref[...])
    prev_ref[...] = u[tm - POOL_HALO:]

    @pl.when(t == pl.num_programs(1) - 1)
    def _():
        state_ref[...] = u[tm - POOL_STATE:]

    x = _prompt_attend(x, ga_ref[...], k_ref, v_ref, wq_ref, wo_ref)
    y_ref[...] = _mlp(x, gf_ref[...], wup_ref, wdown_ref)


def _conv_layer_kernel(x_ref, gm_ref, ga_ref, gf_ref, gfin_ref, k_ref, v_ref, win_ref, wconv_ref,
                       wout_ref, wq_ref, wo_ref, wup_ref, wdown_ref, y_ref, state_ref, prev_ref):
    t = pl.program_id(1)
    tm = x_ref.shape[0]

    @pl.when(t == 0)
    def _():
        prev_ref[...] = jnp.zeros_like(prev_ref)

    x = x_ref[...]
    u = _rms(x, gm_ref[...]).astype(BF16)
    gate_b = _dot(u, win_ref[:, :D_MODEL])
    ch = _dot(u, win_ref[:, D_MODEL:2 * D_MODEL]) * _dot(u, win_ref[:, 2 * D_MODEL:])
    ext = jnp.concatenate([prev_ref[...], ch], axis=0)
    conv = ext * _conv_tap(wconv_ref, 0)
    for back in range(1, CONV_WIDTH):
        conv = conv + pltpu.roll(ext, back, axis=0) * _conv_tap(wconv_ref, back)
    conv = conv[CONV_HALO:]
    x = x + _dot((gate_b * conv).astype(BF16), wout_ref[...])
    prev_ref[...] = ch[tm - CONV_HALO:]

    @pl.when(t == pl.num_programs(1) - 1)
    def _():
        state_ref[...] = ch[tm - CONV_STATE:]

    x = _prompt_attend(x, ga_ref[...], k_ref, v_ref, wq_ref, wo_ref)
    x = _mlp(x, gf_ref[...], wup_ref, wdown_ref)
    y_ref[...] = _rms(x, gfin_ref[...])


def _prompt_layer(kernel, name, x, kb, vb, small, weights, state_rows, halo):
    batch, seq, _ = x.shape
    xspec = pl.BlockSpec((None, ROW_TILE, D_MODEL), lambda b, t: (b, t, 0))
    kvspec = pl.BlockSpec((None, N_MEM, D_MODEL), lambda b, t: (b, 0, 0))
    return pl.pallas_call(
        kernel,
        grid=(batch, seq // ROW_TILE),
        in_specs=[xspec] + [_vmem()] * len(small) + [kvspec, kvspec] + [_vmem()] * len(weights),
        out_specs=[xspec, pl.BlockSpec((None, state_rows, D_MODEL), lambda b, t: (b, 0, 0))],
        out_shape=[jax.ShapeDtypeStruct(x.shape, F32),
                   jax.ShapeDtypeStruct((batch, state_rows, D_MODEL), F32)],
        scratch_shapes=[pltpu.VMEM((halo, D_MODEL), F32)],
        compiler_params=pltpu.CompilerParams(
            dimension_semantics=("arbitrary", "arbitrary"), vmem_limit_bytes=VMEM_LIMIT),
        name=name,
    )(x, *small, kb, vb, *weights)


def _sample_pool_pre_kernel(x_ref, st_ref, gm_ref, ga_ref, wpool_ref, scale_ref, wq_ref,
                            x1_ref, q_ref, newst_ref):
    x = x_ref[...]
    u = _rms(x, gm_ref[...])
    pooled = []
    for g, w in enumerate(POOL_WINDOWS):
        s = u[:, g * POOL_GROUP:(g + 1) * POOL_GROUP]
        for back in range(1, w):
            s = s + st_ref[POOL_STATE - back, :, g * POOL_GROUP:(g + 1) * POOL_GROUP]
        cnt = min(PAST_LEN + 1, w)
        pooled.append(s * (1.0 / cnt) - u[:, g * POOL_GROUP:(g + 1) * POOL_GROUP])
    pooled = jnp.concatenate(pooled, axis=-1)
    x = x + _pool_project(pooled, wpool_ref, scale_ref[...])
    x1_ref[...] = x
    q_ref[...] = _dot(_rms(x, ga_ref[...]).astype(BF16), wq_ref[...]) * (MEM_HEAD_DIM ** -0.5)
    newst_ref[:POOL_STATE - 1] = st_ref[1:]
    newst_ref[POOL_STATE - 1] = u


def _sample_conv_pre_kernel(x_ref, st_ref, gm_ref, ga_ref, win_ref, wconv_ref, wout_ref, wq_ref,
                            x1_ref, q_ref, newst_ref):
    x = x_ref[...]
    u = _rms(x, gm_ref[...]).astype(BF16)
    gate_b = _dot(u, win_ref[:, :D_MODEL])
    ch = _dot(u, win_ref[:, D_MODEL:2 * D_MODEL]) * _dot(u, win_ref[:, 2 * D_MODEL:])
    conv = ch * _conv_tap(wconv_ref, 0)
    for back in range(1, CONV_WIDTH):
        lo = (CONV_STATE - back) * D_MODEL
        conv = conv + st_ref[:, lo:lo + D_MODEL] * _conv_tap(wconv_ref, back)
    x = x + _dot((gate_b * conv).astype(BF16), wout_ref[...])
    x1_ref[...] = x
    q_ref[...] = _dot(_rms(x, ga_ref[...]).astype(BF16), wq_ref[...]) * (MEM_HEAD_DIM ** -0.5)
    keep = (CONV_STATE - 1) * D_MODEL
    newst_ref[:, :keep] = st_ref[:, D_MODEL:]
    newst_ref[:, keep:] = ch


def _sample_pre(kernel, name, x, state, small, weights):
    n = x.shape[0]
    act = jax.ShapeDtypeStruct((n, D_MODEL), F32)
    return pl.pallas_call(
        kernel,
        out_shape=[act, act, jax.ShapeDtypeStruct(state.shape, F32)],
        compiler_params=pltpu.CompilerParams(vmem_limit_bytes=VMEM_LIMIT),
        name=name,
    )(x, state, *small, *weights)


def _head_rows(a):
    n = a.shape[0]
    a = a.reshape(n, MEM_HEADS, MEM_HEAD_DIM // LANES, LANES)
    return a.transpose(0, 2, 1, 3).reshape(n, HEAD_ROWS, LANES)


def _unhead_rows(a):
    n = a.shape[0]
    a = a.reshape(n, MEM_HEAD_DIM // LANES, MEM_HEADS, LANES)
    return a.transpose(0, 2, 1, 3).reshape(n, D_MODEL)


def _cache_rows(c):
    depth, n = c.shape[:2]
    c = c.reshape(depth, n, N_MEM, MEM_HEADS, MEM_HEAD_DIM // LANES, LANES)
    return c.transpose(0, 1, 2, 4, 3, 5).reshape(depth, n, N_MEM, HEAD_ROWS, LANES)


def _sample_attn_kernel(q_ref, k_ref, v_ref, o_ref):
    for j in range(SAMPLE_BLOCK):
        prod = k_ref[j] * q_ref[j][None]
        part = jnp.sum(prod, axis=-1, keepdims=True)
        s = part + pltpu.roll(part, MEM_HEADS, axis=1)
        e = jnp.exp(s - jnp.max(s, axis=0, keepdims=True))
        inv = 1.0 / jnp.sum(e, axis=0)
        o_ref[j] = jnp.sum(e * v_ref[j], axis=0) * inv


def _sample_attn(q, k, v, layer, name):
    n = q.shape[0]
    qspec = pl.BlockSpec((SAMPLE_BLOCK, HEAD_ROWS, LANES), lambda i: (i, 0, 0))
    kvspec = pl.BlockSpec((None, SAMPLE_BLOCK, N_MEM, HEAD_ROWS, LANES), lambda i: (layer, i, 0, 0, 0))
    o = pl.pallas_call(
        _sample_attn_kernel,
        grid=(n // SAMPLE_BLOCK,),
        in_specs=[qspec, kvspec, kvspec],
        out_specs=qspec,
        out_shape=jax.ShapeDtypeStruct((n, HEAD_ROWS, LANES), F32),
        compiler_params=pltpu.CompilerParams(
            dimension_semantics=("arbitrary",), vmem_limit_bytes=VMEM_LIMIT),
        name=name,
    )(_head_rows(q), k, v)
    return _unhead_rows(o)


def _sample_post_kernel(final, x_ref, o_ref, gf_ref, gfin_ref, wo_ref, wup_ref, wdown_ref, y_ref):
    x = x_ref[...] + _dot(o_ref[...].astype(BF16), wo_ref[...])
    x = _mlp(x, gf_ref[...], wup_ref, wdown_ref)
    y_ref[...] = _rms(x, gfin_ref[...]) if final else x


def _sample_post(x1, o, g_ffn, g_final, wo, wup, wdown, final, name):
    return pl.pallas_call(
        functools.partial(_sample_post_kernel, final),
        out_shape=jax.ShapeDtypeStruct(x1.shape, F32),
        compiler_params=pltpu.CompilerParams(vmem_limit_bytes=VMEM_LIMIT),
        name=name,
    )(x1, o, g_ffn, g_final, wo, wup, wdown)


def kernel(x_prompt, x_sample, state_pool, state_conv, cache_mem_k, cache_mem_v, mem_prompt,
           g_mix, g_attn, g_mem, g_ffn, g_final, w_pool, pool_scale,
           w_conv_in, w_conv, w_conv_out, w_q, w_kv, w_o, w_up, w_down):
    batch, seq, _ = x_prompt.shape
    n_sample = x_sample.shape[0]
    assert DEPTH == 2 and seq % ROW_TILE == 0 and ROW_TILE >= POOL_HALO
    assert x_sample.shape[1] == 1 and n_sample % SAMPLE_BLOCK == 0
    assert (batch * N_MEM) % KV_ROW_TILE == 0

    row = lambda a: a.reshape(1, D_MODEL)
    wpool_b, win_b, wout_b = w_pool[0].astype(BF16), w_conv_in[0].astype(BF16), w_conv_out[0].astype(BF16)
    wq_b, wkv_b, wo_b = w_q.astype(BF16), w_kv.astype(BF16), w_o.astype(BF16)
    wup_b, wdown_b = w_up.astype(BF16), w_down.astype(BF16)
    scale0, wconv0, gfin = row(pool_scale[0]), w_conv[0], row(g_final)

    k_all, v_all, kb_all, vb_all = _mem_kv(mem_prompt.reshape(batch * N_MEM, D_MODEL), g_mem, wkv_b)
    kb_all = kb_all.reshape(DEPTH, batch, N_MEM, D_MODEL)
    vb_all = vb_all.reshape(DEPTH, batch, N_MEM, D_MODEL)
    x1, pool_p = _prompt_layer(
        _pool_layer_kernel, "prompt_pool_layer", x_prompt, kb_all[0], vb_all[0],
        [row(g_mix[0]), row(g_attn[0]), row(g_ffn[0])],
        [wpool_b, scale0, wq_b[0], wo_b[0], wup_b[0], wdown_b[0]], POOL_STATE, POOL_HALO)
    y_prompt, conv_p = _prompt_layer(
        _conv_layer_kernel, "prompt_conv_layer", x1, kb_all[1], vb_all[1],
        [row(g_mix[1]), row(g_attn[1]), row(g_ffn[1]), gfin],
        [win_b, wconv0, wout_b, wq_b[1], wo_b[1], wup_b[1], wdown_b[1]], CONV_STATE, CONV_HALO)

    xs = x_sample.reshape(n_sample, D_MODEL)
    ck, cv = _cache_rows(cache_mem_k), _cache_rows(cache_mem_v)
    xs1, q0, pool_s = _sample_pre(
        _sample_pool_pre_kernel, "sample_pool_pre", xs,
        state_pool[0].transpose(1, 0, 2),
        [row(g_mix[0]), row(g_attn[0])], [wpool_b, scale0, wq_b[0]])
    o0 = _sample_attn(q0, ck, cv, 0, "sample_attn0")
    xs2 = _sample_post(xs1, o0, row(g_ffn[0]), gfin, wo_b[0], wup_b[0], wdown_b[0], False, "sample_post0")
    xs3, q1, conv_s = _sample_pre(
        _sample_conv_pre_kernel, "sample_conv_pre", xs2,
        state_conv[0].reshape(n_sample, CONV_STATE * D_MODEL),
        [row(g_mix[1]), row(g_attn[1])], [win_b, wconv0, wout_b, wq_b[1]])
    o1 = _sample_attn(q1, ck, cv, 1, "sample_attn1")
    y_sample = _sample_post(xs3, o1, row(g_ffn[1]), gfin, wo_b[1], wup_b[1], wdown_b[1], True, "sample_post1")

    mem_shape = (DEPTH, batch, N_MEM, MEM_HEADS, MEM_HEAD_DIM)
    return (y_prompt,
            y_sample.reshape(n_sample, 1, D_MODEL),
            pool_p.reshape(1, batch, POOL_STATE, D_MODEL),
            conv_p.reshape(1, batch, CONV_STATE, D_MODEL),
            k_all.reshape(mem_shape),
            v_all.reshape(mem_shape),
            pool_s.transpose(1, 0, 2)[None],
            conv_s.reshape(1, n_sample, CONV_STATE, D_MODEL))
```

```python
import functools

import jax
import jax.numpy as jnp
from jax import lax
from jax.experimental import pallas as pl
from jax.experimental.pallas import tpu as pltpu

D_MODEL = 1024
DEPTH = 2
PAST_LEN = 16384
POOL_WINDOWS = (2, 4, 8, 16)
POOL_GROUP = D_MODEL // len(POOL_WINDOWS)
POOL_STATE = max(POOL_WINDOWS) - 1
CONV_WIDTH = 3
CONV_STATE = CONV_WIDTH - 1
D_FF = 4 * D_MODEL
N_MEM = 256
MEM_HEADS = 4
MEM_HEAD_DIM = D_MODEL // MEM_HEADS
EPS = 1e-6

SUBLANES = 8
LANES = 128
HEAD_ROWS = D_MODEL // LANES
POOL_HALO = 16
CONV_HALO = SUBLANES
ROW_TILE = 512
SUB_TILES = 2
KV_ROW_TILE = 512
FF_CHUNK = 1024
SAMPLE_BLOCK = 8
VMEM_LIMIT = 56 * 1024 * 1024

F32 = jnp.float32
BF16 = jnp.bfloat16

_dot = functools.partial(jnp.dot, preferred_element_type=F32)


def _rms(x, g):
    y = x * lax.rsqrt(jnp.mean(x * x, axis=-1, keepdims=True) + EPS)
    return y * g


def _vmem():
    return pl.BlockSpec(memory_space=pltpu.VMEM)


def _mlp(xs, g_ffn, wup_ref, wdown_ref):
    us = [_rms(x, g_ffn).astype(BF16) for x in xs]
    accs = list(xs)
    for c in range(D_FF // FF_CHUNK):
        sl = slice(c * FF_CHUNK, (c + 1) * FF_CHUNK)
        hs = [jnp.maximum(_dot(u, wup_ref[:, sl]), 0.0) for u in us]
        accs = [a + _dot((h * h).astype(BF16), wdown_ref[sl, :]) for a, h in zip(accs, hs)]
    return accs


def _prompt_attend(xs, g_attn, k_ref, v_ref, wq_ref, wo_ref):
    us = [_rms(x, g_attn).astype(BF16) for x in xs]
    qs = [_dot(u, wq_ref[...]) * (MEM_HEAD_DIM ** -0.5) for u in us]
    heads = [[] for _ in xs]
    for h in range(MEM_HEADS):
        sl = slice(h * MEM_HEAD_DIM, (h + 1) * MEM_HEAD_DIM)
        ss = [lax.dot_general(q[:, sl].astype(BF16), k_ref[:, sl],
                              (((1,), (1,)), ((), ())), preferred_element_type=F32) for q in qs]
        es = [jnp.exp(s - jnp.max(s, axis=-1, keepdims=True)) for s in ss]
        for i, e in enumerate(es):
            inv = 1.0 / jnp.sum(e, axis=-1, keepdims=True)
            heads[i].append((_dot(e.astype(BF16), v_ref[:, sl]) * inv).astype(BF16))
    os_ = [jnp.concatenate(hd, axis=-1) for hd in heads]
    return [x + _dot(o, wo_ref[...]) for x, o in zip(xs, os_)]


def _conv_tap(wconv_ref, back):
    k = CONV_WIDTH - 1 - back
    return wconv_ref[k:k + 1, :]


def _pool_project(pooled, wpool_ref, scale):
    ys = []
    for g in range(len(POOL_WINDOWS)):
        sl = slice(g * POOL_GROUP, (g + 1) * POOL_GROUP)
        ys.append(_dot(pooled[:, sl].astype(BF16), wpool_ref[g]))
    return jnp.concatenate(ys, axis=-1) * scale


def _carried(prev_ref, t):
    @pl.when(t == 0)
    def _():
        prev_ref[...] = jnp.zeros_like(prev_ref)

    return prev_ref[...]


def _split_rows(x_ref):
    rows = x_ref.shape[0] // SUB_TILES
    return [x_ref[i * rows:(i + 1) * rows, :] for i in range(SUB_TILES)]


def _store_rows(y_ref, ys):
    rows = y_ref.shape[0] // SUB_TILES
    for i, y in enumerate(ys):
        y_ref[i * rows:(i + 1) * rows, :] = y


def _memkv_kernel(m_ref, g_ref, w_ref, k_ref, v_ref, kb_ref, vb_ref):
    u = _rms(m_ref[...], g_ref[...]).astype(BF16)
    kv = _dot(u, w_ref[...])
    k = kv[:, :D_MODEL]
    v = kv[:, D_MODEL:]
    k_ref[...] = k
    v_ref[...] = v
    kb_ref[...] = k.astype(BF16)
    vb_ref[...] = v.astype(BF16)


def _mem_kv(mem, g_mem, w_kv):
    rows = mem.shape[0]
    out = jax.ShapeDtypeStruct((DEPTH, rows, D_MODEL), F32)
    outb = jax.ShapeDtypeStruct((DEPTH, rows, D_MODEL), BF16)
    ospec = pl.BlockSpec((None, KV_ROW_TILE, D_MODEL), lambda l, r: (l, r, 0))
    return pl.pallas_call(
        _memkv_kernel,
        grid=(DEPTH, rows // KV_ROW_TILE),
        in_specs=[
            pl.BlockSpec((KV_ROW_TILE, D_MODEL), lambda l, r: (r, 0)),
            pl.BlockSpec((None, 1, D_MODEL), lambda l, r: (l, 0, 0)),
            pl.BlockSpec((None, D_MODEL, 2 * D_MODEL), lambda l, r: (l, 0, 0)),
        ],
        out_specs=[ospec, ospec, ospec, ospec],
        out_shape=[out, out, outb, outb],
        compiler_params=pltpu.CompilerParams(
            dimension_semantics=("arbitrary", "arbitrary"), vmem_limit_bytes=VMEM_LIMIT),
        name="mem_kv",
    )(mem, g_mem.reshape(DEPTH, 1, D_MODEL), w_kv)


def _pool_layer_kernel(x_ref, gm_ref, ga_ref, gf_ref, k_ref, v_ref, wpool_ref, scale_ref,
                       wq_ref, wo_ref, wup_ref, wdown_ref, y_ref, state_ref, prev_ref):
    t = pl.program_id(1)
    tm = x_ref.shape[0]
    rows = tm // SUB_TILES

    carried = _carried(prev_ref, t)
    xs = _split_rows(x_ref)
    us = [_rms(x, gm_ref[...]) for x in xs]
    halos = [carried] + [u[rows - POOL_HALO:] for u in us[:-1]]
    x1 = []
    for i, (x, u, halo) in enumerate(zip(xs, us, halos)):
        ext = jnp.concatenate([halo, u], axis=0)
        pos = t * tm + i * rows + lax.broadcasted_iota(jnp.int32, (rows, 1), 0)
        pooled = []
        for g, w in enumerate(POOL_WINDOWS):
            sl = slice(g * POOL_GROUP, (g + 1) * POOL_GROUP)
            s = ext[:, sl]
            shift = 1
            while shift < w:
                s = s + pltpu.roll(s, shift, axis=0)
                shift *= 2
            inv_cnt = 1.0 / jnp.minimum(pos + 1, w).astype(F32)
            pooled.append(s[POOL_HALO:] * inv_cnt - u[:, sl])
        pooled = jnp.concatenate(pooled, axis=-1)
        x1.append(x + _pool_project(pooled, wpool_ref, scale_ref[...]))
    prev_ref[...] = us[-1][rows - POOL_HALO:]
    state_ref[...] = us[-1][rows - POOL_STATE:]

    x2 = _prompt_attend(x1, ga_ref[...], k_ref, v_ref, wq_ref, wo_ref)
    _store_rows(y_ref, _mlp(x2, gf_ref[...], wup_ref, wdown_ref))


def _conv_layer_kernel(x_ref, gm_ref, ga_ref, gf_ref, gfin_ref, k_ref, v_ref, win_ref, wconv_ref,
                       wout_ref, wq_ref, wo_ref, wup_ref, wdown_ref, y_ref, state_ref, prev_ref):
    t = pl.program_id(1)
    rows = x_ref.shape[0] // SUB_TILES

    carried = _carried(prev_ref, t)
    xs = _split_rows(x_ref)
    us = [_rms(x, gm_ref[...]).astype(BF16) for x in xs]
    gate_bs = [_dot(u, win_ref[:, :D_MODEL]) for u in us]
    chs = [_dot(u, win_ref[:, D_MODEL:2 * D_MODEL]) * _dot(u, win_ref[:, 2 * D_MODEL:]) for u in us]
    halos = [carried] + [ch[rows - CONV_HALO:] for ch in chs[:-1]]
    x1 = []
    for x, gate_b, ch, halo in zip(xs, gate_bs, chs, halos):
        ext = jnp.concatenate([halo, ch], axis=0)
        conv = ext * _conv_tap(wconv_ref, 0)
        for back in range(1, CONV_WIDTH):
            conv = conv + pltpu.roll(ext, back, axis=0) * _conv_tap(wconv_ref, back)
        conv = conv[CONV_HALO:]
        x1.append(x + _dot((gate_b * conv).astype(BF16), wout_ref[...]))
    prev_ref[...] = chs[-1][rows - CONV_HALO:]
    state_ref[...] = chs[-1][rows - CONV_STATE:]

    x2 = _prompt_attend(x1, ga_ref[...], k_ref, v_ref, wq_ref, wo_ref)
    x3 = _mlp(x2, gf_ref[...], wup_ref, wdown_ref)
    _store_rows(y_ref, [_rms(x, gfin_ref[...]) for x in x3])


def _prompt_layer(kernel, name, layer, x, kb, vb, small, weights, state_rows, halo):
    batch, seq, _ = x.shape
    xspec = pl.BlockSpec((None, ROW_TILE, D_MODEL), lambda b, t: (b, t, 0))
    kvspec = pl.BlockSpec((None, None, N_MEM, D_MODEL), lambda b, t: (layer, b, 0, 0))
    return pl.pallas_call(
        kernel,
        grid=(batch, seq // ROW_TILE),
        in_specs=[xspec] + [_vmem()] * len(small) + [kvspec, kvspec] + [_vmem()] * len(weights),
        out_specs=[xspec, pl.BlockSpec((None, state_rows, D_MODEL), lambda b, t: (b, 0, 0))],
        out_shape=[jax.ShapeDtypeStruct(x.shape, F32),
                   jax.ShapeDtypeStruct((batch, state_rows, D_MODEL), F32)],
        scratch_shapes=[pltpu.VMEM((halo, D_MODEL), F32)],
        compiler_params=pltpu.CompilerParams(
            dimension_semantics=("arbitrary", "arbitrary"), vmem_limit_bytes=VMEM_LIMIT),
        name=name,
    )(x, *small, kb, vb, *weights)


def _sample_pool_pre_kernel(x_ref, st_ref, gm_ref, ga_ref, wpool_ref, scale_ref, wq_ref,
                            x1_ref, q_ref, newst_ref):
    x = x_ref[...]
    u = _rms(x, gm_ref[...])
    pooled = []
    for g, w in enumerate(POOL_WINDOWS):
        s = u[:, g * POOL_GROUP:(g + 1) * POOL_GROUP]
        for back in range(1, w):
            s = s + st_ref[POOL_STATE - back, :, g * POOL_GROUP:(g + 1) * POOL_GROUP]
        cnt = min(PAST_LEN + 1, w)
        pooled.append(s * (1.0 / cnt) - u[:, g * POOL_GROUP:(g + 1) * POOL_GROUP])
    pooled = jnp.concatenate(pooled, axis=-1)
    x = x + _pool_project(pooled, wpool_ref, scale_ref[...])
    x1_ref[...] = x
    q_ref[...] = _dot(_rms(x, ga_ref[...]).astype(BF16), wq_ref[...]) * (MEM_HEAD_DIM ** -0.5)
    newst_ref[:POOL_STATE - 1] = st_ref[1:]
    newst_ref[POOL_STATE - 1] = u


def _sample_conv_pre_kernel(x_ref, st_ref, gm_ref, ga_ref, win_ref, wconv_ref, wout_ref, wq_ref,
                            x1_ref, q_ref, newst_ref):
    x = x_ref[...]
    u = _rms(x, gm_ref[...]).astype(BF16)
    gate_b = _dot(u, win_ref[:, :D_MODEL])
    ch = _dot(u, win_ref[:, D_MODEL:2 * D_MODEL]) * _dot(u, win_ref[:, 2 * D_MODEL:])
    conv = ch * _conv_tap(wconv_ref, 0)
    for back in range(1, CONV_WIDTH):
        lo = (CONV_STATE - back) * D_MODEL
        conv = conv + st_ref[:, lo:lo + D_MODEL] * _conv_tap(wconv_ref, back)
    x = x + _dot((gate_b * conv).astype(BF16), wout_ref[...])
    x1_ref[...] = x
    q_ref[...] = _dot(_rms(x, ga_ref[...]).astype(BF16), wq_ref[...]) * (MEM_HEAD_DIM ** -0.5)
    keep = (CONV_STATE - 1) * D_MODEL
    newst_ref[:, :keep] = st_ref[:, D_MODEL:]
    newst_ref[:, keep:] = ch


def _sample_pre(kernel, name, x, state, small, weights):
    n = x.shape[0]
    act = jax.ShapeDtypeStruct((n, D_MODEL), F32)
    return pl.pallas_call(
        kernel,
        out_shape=[act, act, jax.ShapeDtypeStruct(state.shape, F32)],
        compiler_params=pltpu.CompilerParams(vmem_limit_bytes=VMEM_LIMIT),
        name=name,
    )(x, state, *small, *weights)


def _head_rows(a):
    n = a.shape[0]
    a = a.reshape(n, MEM_HEADS, MEM_HEAD_DIM // LANES, LANES)
    return a.transpose(0, 2, 1, 3).reshape(n, HEAD_ROWS, LANES)


def _unhead_rows(a):
    n = a.shape[0]
    a = a.reshape(n, MEM_HEAD_DIM // LANES, MEM_HEADS, LANES)
    return a.transpose(0, 2, 1, 3).reshape(n, D_MODEL)


def _cache_rows(c):
    depth, n = c.shape[:2]
    c = c.reshape(depth, n, N_MEM, MEM_HEADS, MEM_HEAD_DIM // LANES, LANES)
    return c.transpose(0, 1, 2, 4, 3, 5).reshape(depth, n, N_MEM, HEAD_ROWS, LANES)


def _sample_attn_kernel(q_ref, k_ref, v_ref, o_ref):
    for j in range(SAMPLE_BLOCK):
        prod = k_ref[j] * q_ref[j][None]
        part = jnp.sum(prod, axis=-1, keepdims=True)
        s = part + pltpu.roll(part, MEM_HEADS, axis=1)
        e = jnp.exp(s - jnp.max(s, axis=0, keepdims=True))
        inv = 1.0 / jnp.sum(e, axis=0)
        o_ref[j] = jnp.sum(e * v_ref[j], axis=0) * inv


def _sample_attn(q, k, v, layer, name):
    n = q.shape[0]
    qspec = pl.BlockSpec((SAMPLE_BLOCK, HEAD_ROWS, LANES), lambda i: (i, 0, 0))
    kvspec = pl.BlockSpec((None, SAMPLE_BLOCK, N_MEM, HEAD_ROWS, LANES), lambda i: (layer, i, 0, 0, 0))
    o = pl.pallas_call(
        _sample_attn_kernel,
        grid=(n // SAMPLE_BLOCK,),
        in_specs=[qspec, kvspec, kvspec],
        out_specs=qspec,
        out_shape=jax.ShapeDtypeStruct((n, HEAD_ROWS, LANES), F32),
        compiler_params=pltpu.CompilerParams(
            dimension_semantics=("arbitrary",), vmem_limit_bytes=VMEM_LIMIT),
        name=name,
    )(_head_rows(q), k, v)
    return _unhead_rows(o)


def _sample_post_kernel(final, x_ref, o_ref, gf_ref, gfin_ref, wo_ref, wup_ref, wdown_ref, y_ref):
    x = x_ref[...] + _dot(o_ref[...].astype(BF16), wo_ref[...])
    (x,) = _mlp([x], gf_ref[...], wup_ref, wdown_ref)
    y_ref[...] = _rms(x, gfin_ref[...]) if final else x


def _sample_post(x1, o, g_ffn, g_final, wo, wup, wdown, final, name):
    return pl.pallas_call(
        functools.partial(_sample_post_kernel, final),
        out_shape=jax.ShapeDtypeStruct(x1.shape, F32),
        compiler_params=pltpu.CompilerParams(vmem_limit_bytes=VMEM_LIMIT),
        name=name,
    )(x1, o, g_ffn, g_final, wo, wup, wdown)


def kernel(x_prompt, x_sample, state_pool, state_conv, cache_mem_k, cache_mem_v, mem_prompt,
           g_mix, g_attn, g_mem, g_ffn, g_final, w_pool, pool_scale,
           w_conv_in, w_conv, w_conv_out, w_q, w_kv, w_o, w_up, w_down):
    batch, seq, _ = x_prompt.shape
    n_sample = x_sample.shape[0]
    assert DEPTH == 2 and seq % ROW_TILE == 0 and ROW_TILE // SUB_TILES >= POOL_HALO
    assert x_sample.shape[1] == 1 and n_sample % SAMPLE_BLOCK == 0
    assert (batch * N_MEM) % KV_ROW_TILE == 0

    row = lambda a: a.reshape(1, D_MODEL)
    cast = lambda a: a.astype(BF16)
    wpool_b, win_b, wout_b = cast(w_pool[0]), cast(w_conv_in[0]), cast(w_conv_out[0])
    wq_b = [cast(w_q[i]) for i in range(DEPTH)]
    wo_b = [cast(w_o[i]) for i in range(DEPTH)]
    wup_b = [cast(w_up[i]) for i in range(DEPTH)]
    wdown_b = [cast(w_down[i]) for i in range(DEPTH)]
    scale0, wconv0, gfin = row(pool_scale[0]), w_conv[0], row(g_final)

    k_all, v_all, kb_all, vb_all = _mem_kv(mem_prompt.reshape(batch * N_MEM, D_MODEL), g_mem, cast(w_kv))
    kb_all = kb_all.reshape(DEPTH, batch, N_MEM, D_MODEL)
    vb_all = vb_all.reshape(DEPTH, batch, N_MEM, D_MODEL)
    x1, pool_p = _prompt_layer(
        _pool_layer_kernel, "prompt_pool_layer", 0, x_prompt, kb_all, vb_all,
        [row(g_mix[0]), row(g_attn[0]), row(g_ffn[0])],
        [wpool_b, scale0, wq_b[0], wo_b[0], wup_b[0], wdown_b[0]], POOL_STATE, POOL_HALO)
    y_prompt, conv_p = _prompt_layer(
        _conv_layer_kernel, "prompt_conv_layer", 1, x1, kb_all, vb_all,
        [row(g_mix[1]), row(g_attn[1]), row(g_ffn[1]), gfin],
        [win_b, wconv0, wout_b, wq_b[1], wo_b[1], wup_b[1], wdown_b[1]], CONV_STATE, CONV_HALO)

    xs = x_sample.reshape(n_sample, D_MODEL)
    ck, cv = _cache_rows(cache_mem_k), _cache_rows(cache_mem_v)
    xs1, q0, pool_s = _sample_pre(
        _sample_pool_pre_kernel, "sample_pool_pre", xs,
        state_pool[0].transpose(1, 0, 2),
        [row(g_mix[0]), row(g_attn[0])], [wpool_b, scale0, wq_b[0]])
    o0 = _sample_attn(q0, ck, cv, 0, "sample_attn0")
    xs2 = _sample_post(xs1, o0, row(g_ffn[0]), gfin, wo_b[0], wup_b[0], wdown_b[0], False, "sample_post0")
    xs3, q1, conv_s = _sample_pre(
        _sample_conv_pre_kernel, "sample_conv_pre", xs2,
        state_conv[0].reshape(n_sample, CONV_STATE * D_MODEL),
        [row(g_mix[1]), row(g_attn[1])], [win_b, wconv0, wout_b, wq_b[1]])
    o1 = _sample_attn(q1, ck, cv, 1, "sample_attn1")
    y_sample = _sample_post(xs3, o1, row(g_ffn[1]), gfin, wo_b[1], wup_b[1], wdown_b[1], True, "sample_post1")

    mem_shape = (DEPTH, batch, N_MEM, MEM_HEADS, MEM_HEAD_DIM)
    return (y_prompt,
            y_sample.reshape(n_sample, 1, D_MODEL),
            pool_p.reshape(1, batch, POOL_STATE, D_MODEL),
            conv_p.reshape(1, batch, CONV_STATE, D_MODEL),
            k_all.reshape(mem_shape),
            v_all.reshape(mem_shape),
            pool_s.transpose(1, 0, 2)[None],
            conv_s.reshape(1, n_sample, CONV_STATE, D_MODEL))
```

```python
import functools

import jax
import jax.numpy as jnp
from jax import lax
from jax.experimental import pallas as pl
from jax.experimental.pallas import tpu as pltpu

D_MODEL = 1024
DEPTH = 2
PAST_LEN = 16384
POOL_WINDOWS = (2, 4, 8, 16)
POOL_GROUP = D_MODEL // len(POOL_WINDOWS)
POOL_STATE = max(POOL_WINDOWS) - 1
CONV_WIDTH = 3
CONV_STATE = CONV_WIDTH - 1
D_FF = 4 * D_MODEL
N_MEM = 256
MEM_HEADS = 4
MEM_HEAD_DIM = D_MODEL // MEM_HEADS
EPS = 1e-6

SUBLANES = 8
LANES = 128
HEAD_ROWS = D_MODEL // LANES
POOL_HALO = 16
CONV_HALO = SUBLANES
POOL_ROW_TILE = 512
CONV_ROW_TILE = 512
SUB_TILES = 2
KV_ROW_TILE = 512
FF_CHUNK = 1024
POOL_LEAD = 7
CONV_LEAD = 2
VMEM_LIMIT = 56 * 1024 * 1024
CONV_VMEM_LIMIT = 62 * 1024 * 1024

F32 = jnp.float32
BF16 = jnp.bfloat16

_dot = functools.partial(jnp.dot, preferred_element_type=F32)


def _rms(x, g):
    y = x * lax.rsqrt(jnp.mean(x * x, axis=-1, keepdims=True) + EPS)
    return y * g


def _vmem():
    return pl.BlockSpec(memory_space=pltpu.VMEM)


def _finish(stages):
    try:
        while True:
            next(stages)
    except StopIteration as done:
        return done.value


def _weave(main, side, lead):
    live = {"main": main, "side": side}

    def step(name):
        if name in live:
            try:
                next(live[name])
            except StopIteration:
                del live[name]

    for _ in range(lead):
        step("main")
    while live:
        step("main")
        step("side")


def _mlp(xs, g_ffn, wup_ref, wdown_ref):
    us = [_rms(x, g_ffn).astype(BF16) for x in xs]
    yield
    accs = list(xs)
    for c in range(D_FF // FF_CHUNK):
        sl = slice(c * FF_CHUNK, (c + 1) * FF_CHUNK)
        hs = [jnp.maximum(_dot(u, wup_ref[:, sl]), 0.0) for u in us]
        yield
        accs = [a + _dot((h * h).astype(BF16), wdown_ref[sl, :]) for a, h in zip(accs, hs)]
        yield
    return accs


def _prompt_attend(xs, g_attn, k_ref, v_ref, wq_ref, wo_ref):
    us = [_rms(x, g_attn).astype(BF16) for x in xs]
    yield
    qs = [_dot(u, wq_ref[...]) * (MEM_HEAD_DIM ** -0.5) for u in us]
    yield
    heads = [[] for _ in xs]
    for h in range(MEM_HEADS):
        sl = slice(h * MEM_HEAD_DIM, (h + 1) * MEM_HEAD_DIM)
        ss = [lax.dot_general(q[:, sl].astype(BF16), k_ref[:, sl],
                              (((1,), (1,)), ((), ())), preferred_element_type=F32) for q in qs]
        es = [jnp.exp(s - jnp.max(s, axis=-1, keepdims=True)) for s in ss]
        for i, e in enumerate(es):
            inv = 1.0 / jnp.sum(e, axis=-1, keepdims=True)
            heads[i].append((_dot(e.astype(BF16), v_ref[:, sl]) * inv).astype(BF16))
        yield
    os_ = [jnp.concatenate(hd, axis=-1) for hd in heads]
    return [x + _dot(o, wo_ref[...]) for x, o in zip(xs, os_)]


def _sample_attend(q_ref, k_ref, v_ref, o_ref):
    for j in range(q_ref.shape[0]):
        prod = k_ref[j] * q_ref[j][None]
        part = jnp.sum(prod, axis=-1, keepdims=True)
        s = part + pltpu.roll(part, MEM_HEADS, axis=1)
        e = jnp.exp(s - jnp.max(s, axis=0, keepdims=True))
        inv = 1.0 / jnp.sum(e, axis=0)
        yield
        o_ref[j] = jnp.sum(e * v_ref[j], axis=0) * inv
        yield


def _conv_tap(wconv_ref, back):
    k = CONV_WIDTH - 1 - back
    return wconv_ref[k:k + 1, :]


def _pool_project(pooled, wpool_ref, scale):
    ys = []
    for g in range(len(POOL_WINDOWS)):
        sl = slice(g * POOL_GROUP, (g + 1) * POOL_GROUP)
        ys.append(_dot(pooled[:, sl].astype(BF16), wpool_ref[g]))
    return jnp.concatenate(ys, axis=-1) * scale


def _carried(prev_ref, t):
    @pl.when(t == 0)
    def _():
        prev_ref[...] = jnp.zeros_like(prev_ref)

    return prev_ref[...]


def _split_rows(x_ref):
    rows = x_ref.shape[0] // SUB_TILES
    return [x_ref[i * rows:(i + 1) * rows, :] for i in range(SUB_TILES)]


def _store_rows(y_ref, ys):
    rows = y_ref.shape[0] // SUB_TILES
    for i, y in enumerate(ys):
        y_ref[i * rows:(i + 1) * rows, :] = y


def _memkv_kernel(m_ref, g_ref, w_ref, k_ref, v_ref, kb_ref, vb_ref):
    u = _rms(m_ref[...], g_ref[...]).astype(BF16)
    kv = _dot(u, w_ref[...])
    k = kv[:, :D_MODEL]
    v = kv[:, D_MODEL:]
    k_ref[...] = k
    v_ref[...] = v
    kb_ref[...] = k.astype(BF16)
    vb_ref[...] = v.astype(BF16)


def _mem_kv(mem, g_mem, w_kv):
    rows = mem.shape[0]
    out = jax.ShapeDtypeStruct((DEPTH, rows, D_MODEL), F32)
    outb = jax.ShapeDtypeStruct((DEPTH, rows, D_MODEL), BF16)
    ospec = pl.BlockSpec((None, KV_ROW_TILE, D_MODEL), lambda l, r: (l, r, 0))
    return pl.pallas_call(
        _memkv_kernel,
        grid=(DEPTH, rows // KV_ROW_TILE),
        in_specs=[
            pl.BlockSpec((KV_ROW_TILE, D_MODEL), lambda l, r: (r, 0)),
            pl.BlockSpec((None, 1, D_MODEL), lambda l, r: (l, 0, 0)),
            pl.BlockSpec((None, D_MODEL, 2 * D_MODEL), lambda l, r: (l, 0, 0)),
        ],
        out_specs=[ospec, ospec, ospec, ospec],
        out_shape=[out, out, outb, outb],
        compiler_params=pltpu.CompilerParams(
            dimension_semantics=("arbitrary", "arbitrary"), vmem_limit_bytes=VMEM_LIMIT),
        name="mem_kv",
    )(mem, g_mem.reshape(DEPTH, 1, D_MODEL), w_kv)


def _pool_layer_kernel(x_ref, gm_ref, ga_ref, gf_ref, k_ref, v_ref, sq_ref, sk_ref, sv_ref,
                       wpool_ref, scale_ref, wq_ref, wo_ref, wup_ref, wdown_ref,
                       y_ref, state_ref, so_ref, prev_ref):
    t = pl.program_id(1)
    tm = x_ref.shape[0]
    rows = tm // SUB_TILES
    carried = _carried(prev_ref, t)

    def prompt():
        xs = _split_rows(x_ref)
        us = [_rms(x, gm_ref[...]) for x in xs]
        yield
        halos = [carried] + [u[rows - POOL_HALO:] for u in us[:-1]]
        x1 = []
        for i, (x, u, halo) in enumerate(zip(xs, us, halos)):
            ext = jnp.concatenate([halo, u], axis=0)
            pos = t * tm + i * rows + lax.broadcasted_iota(jnp.int32, (rows, 1), 0)
            pooled = []
            for g, w in enumerate(POOL_WINDOWS):
                sl = slice(g * POOL_GROUP, (g + 1) * POOL_GROUP)
                s = ext[:, sl]
                shift = 1
                while shift < w:
                    s = s + pltpu.roll(s, shift, axis=0)
                    shift *= 2
                inv_cnt = 1.0 / jnp.minimum(pos + 1, w).astype(F32)
                pooled.append(s[POOL_HALO:] * inv_cnt - u[:, sl])
                if g % 2:
                    yield
            pooled = jnp.concatenate(pooled, axis=-1)
            x1.append(x + _pool_project(pooled, wpool_ref, scale_ref[...]))
        prev_ref[...] = us[-1][rows - POOL_HALO:]
        state_ref[...] = us[-1][rows - POOL_STATE:]
        yield
        x2 = yield from _prompt_attend(x1, ga_ref[...], k_ref, v_ref, wq_ref, wo_ref)
        _store_rows(y_ref, (yield from _mlp(x2, gf_ref[...], wup_ref, wdown_ref)))

    _weave(prompt(), _sample_attend(sq_ref, sk_ref, sv_ref, so_ref), POOL_LEAD)


def _conv_layer_kernel(x_ref, gm_ref, ga_ref, gf_ref, gfin_ref, k_ref, v_ref, sq_ref, sk_ref,
                       sv_ref, win_ref, wconv_ref, wout_ref, wq_ref, wo_ref, wup_ref, wdown_ref,
                       y_ref, state_ref, so_ref, prev_ref):
    t = pl.program_id(1)
    rows = x_ref.shape[0] // SUB_TILES
    carried = _carried(prev_ref, t)

    def prompt():
        xs = _split_rows(x_ref)
        us = [_rms(x, gm_ref[...]).astype(BF16) for x in xs]
        yield
        gate_bs = [_dot(u, win_ref[:, :D_MODEL]) for u in us]
        yield
        gate_cs = [_dot(u, win_ref[:, D_MODEL:2 * D_MODEL]) for u in us]
        yield
        chs = [gate_c * _dot(u, win_ref[:, 2 * D_MODEL:]) for gate_c, u in zip(gate_cs, us)]
        yield
        halos = [carried] + [ch[rows - CONV_HALO:] for ch in chs[:-1]]
        x1 = []
        for x, gate_b, ch, halo in zip(xs, gate_bs, chs, halos):
            ext = jnp.concatenate([halo, ch], axis=0)
            conv = ext * _conv_tap(wconv_ref, 0)
            for back in range(1, CONV_WIDTH):
                conv = conv + pltpu.roll(ext, back, axis=0) * _conv_tap(wconv_ref, back)
            conv = conv[CONV_HALO:]
            x1.append(x + _dot((gate_b * conv).astype(BF16), wout_ref[...]))
        prev_ref[...] = chs[-1][rows - CONV_HALO:]
        state_ref[...] = chs[-1][rows - CONV_STATE:]
        yield
        x2 = yield from _prompt_attend(x1, ga_ref[...], k_ref, v_ref, wq_ref, wo_ref)
        x3 = yield from _mlp(x2, gf_ref[...], wup_ref, wdown_ref)
        _store_rows(y_ref, [_rms(x, gfin_ref[...]) for x in x3])

    _weave(prompt(), _sample_attend(sq_ref, sk_ref, sv_ref, so_ref), CONV_LEAD)


def _prompt_layer(kernel, name, layer, row_tile, vmem_limit, x, kb, vb, sq, sk, sv,
                  small, weights, state_rows, halo):
    batch, seq, _ = x.shape
    tiles_per_seq = seq // row_tile
    n_sample = sq.shape[0]
    per_step = n_sample // (batch * tiles_per_seq)
    assert seq % row_tile == 0 and row_tile // SUB_TILES >= POOL_HALO
    assert per_step * batch * tiles_per_seq == n_sample

    step = lambda b, t: b * tiles_per_seq + t
    xspec = pl.BlockSpec((None, row_tile, D_MODEL), lambda b, t: (b, t, 0))
    kvspec = pl.BlockSpec((None, None, N_MEM, D_MODEL), lambda b, t: (layer, b, 0, 0))
    sqspec = pl.BlockSpec((per_step, HEAD_ROWS, LANES), lambda b, t: (step(b, t), 0, 0))
    skvspec = pl.BlockSpec((None, per_step, N_MEM, HEAD_ROWS, LANES),
                           lambda b, t: (layer, step(b, t), 0, 0, 0))
    return pl.pallas_call(
        kernel,
        grid=(batch, tiles_per_seq),
        in_specs=([xspec] + [_vmem()] * len(small) + [kvspec, kvspec, sqspec, skvspec, skvspec]
                  + [_vmem()] * len(weights)),
        out_specs=[xspec, pl.BlockSpec((None, state_rows, D_MODEL), lambda b, t: (b, 0, 0)), sqspec],
        out_shape=[jax.ShapeDtypeStruct(x.shape, F32),
                   jax.ShapeDtypeStruct((batch, state_rows, D_MODEL), F32),
                   jax.ShapeDtypeStruct(sq.shape, F32)],
        scratch_shapes=[pltpu.VMEM((halo, D_MODEL), F32)],
        compiler_params=pltpu.CompilerParams(
            dimension_semantics=("arbitrary", "arbitrary"), vmem_limit_bytes=vmem_limit),
        name=name,
    )(x, *small, kb, vb, sq, sk, sv, *weights)


def _sample_pool_pre_kernel(x_ref, st_ref, gm_ref, ga_ref, wpool_ref, scale_ref, wq_ref,
                            x1_ref, q_ref, newst_ref):
    x = x_ref[...]
    u = _rms(x, gm_ref[...])
    pooled = []
    for g, w in enumerate(POOL_WINDOWS):
        s = u[:, g * POOL_GROUP:(g + 1) * POOL_GROUP]
        for back in range(1, w):
            s = s + st_ref[POOL_STATE - back, :, g * POOL_GROUP:(g + 1) * POOL_GROUP]
        cnt = min(PAST_LEN + 1, w)
        pooled.append(s * (1.0 / cnt) - u[:, g * POOL_GROUP:(g + 1) * POOL_GROUP])
    pooled = jnp.concatenate(pooled, axis=-1)
    x = x + _pool_project(pooled, wpool_ref, scale_ref[...])
    x1_ref[...] = x
    q_ref[...] = _dot(_rms(x, ga_ref[...]).astype(BF16), wq_ref[...]) * (MEM_HEAD_DIM ** -0.5)
    newst_ref[:POOL_STATE - 1] = st_ref[1:]
    newst_ref[POOL_STATE - 1] = u


def _sample_conv_pre_kernel(x_ref, st_ref, gm_ref, ga_ref, win_ref, wconv_ref, wout_ref, wq_ref,
                            x1_ref, q_ref, newst_ref):
    x = x_ref[...]
    u = _rms(x, gm_ref[...]).astype(BF16)
    gate_b = _dot(u, win_ref[:, :D_MODEL])
    ch = _dot(u, win_ref[:, D_MODEL:2 * D_MODEL]) * _dot(u, win_ref[:, 2 * D_MODEL:])
    conv = ch * _conv_tap(wconv_ref, 0)
    for back in range(1, CONV_WIDTH):
        lo = (CONV_STATE - back) * D_MODEL
        conv = conv + st_ref[:, lo:lo + D_MODEL] * _conv_tap(wconv_ref, back)
    x = x + _dot((gate_b * conv).astype(BF16), wout_ref[...])
    x1_ref[...] = x
    q_ref[...] = _dot(_rms(x, ga_ref[...]).astype(BF16), wq_ref[...]) * (MEM_HEAD_DIM ** -0.5)
    keep = (CONV_STATE - 1) * D_MODEL
    newst_ref[:, :keep] = st_ref[:, D_MODEL:]
    newst_ref[:, keep:] = ch


def _sample_pre(kernel, name, x, state, small, weights):
    n = x.shape[0]
    act = jax.ShapeDtypeStruct((n, D_MODEL), F32)
    return pl.pallas_call(
        kernel,
        out_shape=[act, act, jax.ShapeDtypeStruct(state.shape, F32)],
        compiler_params=pltpu.CompilerParams(vmem_limit_bytes=VMEM_LIMIT),
        name=name,
    )(x, state, *small, *weights)


def _head_rows(a):
    n = a.shape[0]
    a = a.reshape(n, MEM_HEADS, MEM_HEAD_DIM // LANES, LANES)
    return a.transpose(0, 2, 1, 3).reshape(n, HEAD_ROWS, LANES)


def _unhead_rows(a):
    n = a.shape[0]
    a = a.reshape(n, MEM_HEAD_DIM // LANES, MEM_HEADS, LANES)
    return a.transpose(0, 2, 1, 3).reshape(n, D_MODEL)


def _cache_rows(c):
    depth, n = c.shape[:2]
    c = c.reshape(depth, n, N_MEM, MEM_HEADS, MEM_HEAD_DIM // LANES, LANES)
    return c.transpose(0, 1, 2, 4, 3, 5).reshape(depth, n, N_MEM, HEAD_ROWS, LANES)


def _sample_post_kernel(final, x_ref, o_ref, gf_ref, gfin_ref, wo_ref, wup_ref, wdown_ref, y_ref):
    x = x_ref[...] + _dot(o_ref[...].astype(BF16), wo_ref[...])
    (x,) = _finish(_mlp([x], gf_ref[...], wup_ref, wdown_ref))
    y_ref[...] = _rms(x, gfin_ref[...]) if final else x


def _sample_post(x1, o, g_ffn, g_final, wo, wup, wdown, final, name):
    return pl.pallas_call(
        functools.partial(_sample_post_kernel, final),
        out_shape=jax.ShapeDtypeStruct(x1.shape, F32),
        compiler_params=pltpu.CompilerParams(vmem_limit_bytes=VMEM_LIMIT),
        name=name,
    )(x1, o, g_ffn, g_final, wo, wup, wdown)


def kernel(x_prompt, x_sample, state_pool, state_conv, cache_mem_k, cache_mem_v, mem_prompt,
           g_mix, g_attn, g_mem, g_ffn, g_final, w_pool, pool_scale,
           w_conv_in, w_conv, w_conv_out, w_q, w_kv, w_o, w_up, w_down):
    batch, seq, _ = x_prompt.shape
    n_sample = x_sample.shape[0]
    assert DEPTH == 2 and x_sample.shape[1] == 1
    assert (batch * N_MEM) % KV_ROW_TILE == 0

    row = lambda a: a.reshape(1, D_MODEL)
    cast = lambda a: a.astype(BF16)
    wpool_b, win_b, wout_b = cast(w_pool[0]), cast(w_conv_in[0]), cast(w_conv_out[0])
    wq_b = [cast(w_q[i]) for i in range(DEPTH)]
    wo_b = [cast(w_o[i]) for i in range(DEPTH)]
    wup_b = [cast(w_up[i]) for i in range(DEPTH)]
    wdown_b = [cast(w_down[i]) for i in range(DEPTH)]
    scale0, wconv0, gfin = row(pool_scale[0]), w_conv[0], row(g_final)

    k_all, v_all, kb_all, vb_all = _mem_kv(mem_prompt.reshape(batch * N_MEM, D_MODEL), g_mem, cast(w_kv))
    kb_all = kb_all.reshape(DEPTH, batch, N_MEM, D_MODEL)
    vb_all = vb_all.reshape(DEPTH, batch, N_MEM, D_MODEL)
    xs = x_sample.reshape(n_sample, D_MODEL)
    ck, cv = _cache_rows(cache_mem_k), _cache_rows(cache_mem_v)

    xs1, q0, pool_s = _sample_pre(
        _sample_pool_pre_kernel, "sample_pool_pre", xs,
        state_pool[0].transpose(1, 0, 2),
        [row(g_mix[0]), row(g_attn[0])], [wpool_b, scale0, wq_b[0]])
    x1, pool_p, o0 = _prompt_layer(
        _pool_layer_kernel, "prompt_pool_layer", 0, POOL_ROW_TILE, VMEM_LIMIT,
        x_prompt, kb_all, vb_all, _head_rows(q0), ck, cv,
        [row(g_mix[0]), row(g_attn[0]), row(g_ffn[0])],
        [wpool_b, scale0, wq_b[0], wo_b[0], wup_b[0], wdown_b[0]], POOL_STATE, POOL_HALO)
    xs2 = _sample_post(xs1, _unhead_rows(o0), row(g_ffn[0]), gfin, wo_b[0], wup_b[0], wdown_b[0],
                       False, "sample_post0")

    xs3, q1, conv_s = _sample_pre(
        _sample_conv_pre_kernel, "sample_conv_pre", xs2,
        state_conv[0].reshape(n_sample, CONV_STATE * D_MODEL),
        [row(g_mix[1]), row(g_attn[1])], [win_b, wconv0, wout_b, wq_b[1]])
    y_prompt, conv_p, o1 = _prompt_layer(
        _conv_layer_kernel, "prompt_conv_layer", 1, CONV_ROW_TILE, CONV_VMEM_LIMIT,
        x1, kb_all, vb_all, _head_rows(q1), ck, cv,
        [row(g_mix[1]), row(g_attn[1]), row(g_ffn[1]), gfin],
        [win_b, wconv0, wout_b, wq_b[1], wo_b[1], wup_b[1], wdown_b[1]], CONV_STATE, CONV_HALO)
    y_sample = _sample_post(xs3, _unhead_rows(o1), row(g_ffn[1]), gfin, wo_b[1], wup_b[1], wdown_b[1],
                            True, "sample_post1")

    mem_shape = (DEPTH, batch, N_MEM, MEM_HEADS, MEM_HEAD_DIM)
    return (y_prompt,
            y_sample.reshape(n_sample, 1, D_MODEL),
            pool_p.reshape(1, batch, POOL_STATE, D_MODEL),
            conv_p.reshape(1, batch, CONV_STATE, D_MODEL),
            k_all.reshape(mem_shape),
            v_all.reshape(mem_shape),
            pool_s.transpose(1, 0, 2)[None],
            conv_s.reshape(1, n_sample, CONV_STATE, D_MODEL))
```

```python
import functools

import jax
import jax.numpy as jnp
from jax import lax
from jax.experimental import pallas as pl
from jax.experimental.pallas import tpu as pltpu

D_MODEL = 1024
DEPTH = 2
PAST_LEN = 16384
POOL_WINDOWS = (2, 4, 8, 16)
POOL_GROUP = D_MODEL // len(POOL_WINDOWS)
POOL_STATE = max(POOL_WINDOWS) - 1
CONV_WIDTH = 3
CONV_STATE = CONV_WIDTH - 1
D_FF = 4 * D_MODEL
N_MEM = 256
MEM_HEADS = 4
MEM_HEAD_DIM = D_MODEL // MEM_HEADS
EPS = 1e-6

SUBLANES = 8
LANES = 128
HEAD_ROWS = D_MODEL // LANES
POOL_HALO = 16
CONV_HALO = SUBLANES
POOL_ROW_TILE = 512
CONV_ROW_TILE = 512
SUB_TILES = 2
KV_ROW_TILE = 512
FF_CHUNK = 1024
POOL_STAGGER = 10
CONV_STAGGER = 0
POOL_SAMPLE_START = 14
CONV_SAMPLE_START = 2
VMEM_LIMIT = 56 * 1024 * 1024
CONV_VMEM_LIMIT = 62 * 1024 * 1024

F32 = jnp.float32
BF16 = jnp.bfloat16

_dot = functools.partial(jnp.dot, preferred_element_type=F32)


def _rms(x, g):
    y = x * lax.rsqrt(jnp.mean(x * x, axis=-1, keepdims=True) + EPS)
    return y * g


def _vmem():
    return pl.BlockSpec(memory_space=pltpu.VMEM)


def _finish(stages):
    try:
        while True:
            next(stages)
    except StopIteration as done:
        return done.value


def _weave(*delayed):
    live = list(delayed)
    round_ = 0
    while live:
        for item in list(live):
            stages, first_round = item
            if round_ >= first_round:
                try:
                    next(stages)
                except StopIteration:
                    live.remove(item)
        round_ += 1


def _mlp(xs, g_ffn, wup_ref, wdown_ref):
    us = [_rms(x, g_ffn).astype(BF16) for x in xs]
    yield
    accs = list(xs)
    for c in range(D_FF // FF_CHUNK):
        sl = slice(c * FF_CHUNK, (c + 1) * FF_CHUNK)
        hs = [jnp.maximum(_dot(u, wup_ref[:, sl]), 0.0) for u in us]
        yield
        accs = [a + _dot((h * h).astype(BF16), wdown_ref[sl, :]) for a, h in zip(accs, hs)]
        yield
    return accs


def _prompt_attend(xs, g_attn, k_ref, v_ref, wq_ref, wo_ref):
    us = [_rms(x, g_attn).astype(BF16) for x in xs]
    yield
    qs = [_dot(u, wq_ref[...]) * (MEM_HEAD_DIM ** -0.5) for u in us]
    yield
    heads = [[] for _ in xs]
    for h in range(MEM_HEADS):
        sl = slice(h * MEM_HEAD_DIM, (h + 1) * MEM_HEAD_DIM)
        ss = [lax.dot_general(q[:, sl].astype(BF16), k_ref[:, sl],
                              (((1,), (1,)), ((), ())), preferred_element_type=F32) for q in qs]
        es = [jnp.exp(s - jnp.max(s, axis=-1, keepdims=True)) for s in ss]
        for i, e in enumerate(es):
            inv = 1.0 / jnp.sum(e, axis=-1, keepdims=True)
            heads[i].append((_dot(e.astype(BF16), v_ref[:, sl]) * inv).astype(BF16))
        yield
    os_ = [jnp.concatenate(hd, axis=-1) for hd in heads]
    return [x + _dot(o, wo_ref[...]) for x, o in zip(xs, os_)]


def _sample_attend(q_ref, k_ref, v_ref, o_ref):
    for j in range(q_ref.shape[0]):
        prod = k_ref[j] * q_ref[j][None]
        part = jnp.sum(prod, axis=-1, keepdims=True)
        s = part + pltpu.roll(part, MEM_HEADS, axis=1)
        e = jnp.exp(s - jnp.max(s, axis=0, keepdims=True))
        inv = 1.0 / jnp.sum(e, axis=0)
        yield
        o_ref[j] = jnp.sum(e * v_ref[j], axis=0) * inv
        yield


def _conv_tap(wconv_ref, back):
    k = CONV_WIDTH - 1 - back
    return wconv_ref[k:k + 1, :]


def _pool_project(pooled, wpool_ref, scale):
    ys = []
    for g in range(len(POOL_WINDOWS)):
        sl = slice(g * POOL_GROUP, (g + 1) * POOL_GROUP)
        ys.append(_dot(pooled[:, sl].astype(BF16), wpool_ref[g]))
    return jnp.concatenate(ys, axis=-1) * scale


def _carried(prev_ref, t):
    @pl.when(t == 0)
    def _():
        prev_ref[...] = jnp.zeros_like(prev_ref)

    return prev_ref[...]


def _to_head_rows(a):
    tiles = [a[:, (c % MEM_HEADS) * MEM_HEAD_DIM + (c // MEM_HEADS) * LANES:][:, :LANES]
             for c in range(HEAD_ROWS)]
    return jnp.concatenate(tiles, axis=-1).reshape(a.shape[0], HEAD_ROWS, LANES)


def _memkv_kernel(m_ref, g_ref, w_ref, k_ref, v_ref, kb_ref, vb_ref):
    u = _rms(m_ref[...], g_ref[...]).astype(BF16)
    kv = _dot(u, w_ref[...])
    k = kv[:, :D_MODEL]
    v = kv[:, D_MODEL:]
    k_ref[...] = _to_head_rows(k)
    v_ref[...] = _to_head_rows(v)
    kb_ref[...] = k.astype(BF16)
    vb_ref[...] = v.astype(BF16)


def _mem_kv(mem, g_mem, w_kv):
    rows = mem.shape[0]
    out = jax.ShapeDtypeStruct((DEPTH, rows, HEAD_ROWS, LANES), F32)
    outb = jax.ShapeDtypeStruct((DEPTH, rows, D_MODEL), BF16)
    hspec = pl.BlockSpec((None, KV_ROW_TILE, HEAD_ROWS, LANES), lambda l, r: (l, r, 0, 0))
    ospec = pl.BlockSpec((None, KV_ROW_TILE, D_MODEL), lambda l, r: (l, r, 0))
    return pl.pallas_call(
        _memkv_kernel,
        grid=(DEPTH, rows // KV_ROW_TILE),
        in_specs=[
            pl.BlockSpec((KV_ROW_TILE, D_MODEL), lambda l, r: (r, 0)),
            pl.BlockSpec((None, 1, D_MODEL), lambda l, r: (l, 0, 0)),
            pl.BlockSpec((None, D_MODEL, 2 * D_MODEL), lambda l, r: (l, 0, 0)),
        ],
        out_specs=[hspec, hspec, ospec, ospec],
        out_shape=[out, out, outb, outb],
        compiler_params=pltpu.CompilerParams(
            dimension_semantics=("arbitrary", "arbitrary"), vmem_limit_bytes=VMEM_LIMIT),
        name="mem_kv",
    )(mem, g_mem.reshape(DEPTH, 1, D_MODEL), w_kv)


def _pool_layer_kernel(x_ref, gm_ref, ga_ref, gf_ref, k_ref, v_ref, sq_ref, sk_ref, sv_ref,
                       wpool_ref, scale_ref, wq_ref, wo_ref, wup_ref, wdown_ref,
                       y_ref, state_ref, so_ref, prev_ref):
    t = pl.program_id(1)
    tm = x_ref.shape[0]
    rows = tm // SUB_TILES
    halos = [_carried(prev_ref, t)]

    def chain(i):
        rs = slice(i * rows, (i + 1) * rows)
        x = x_ref[rs, :]
        u = _rms(x, gm_ref[...])
        halos.append(u[rows - POOL_HALO:])
        if i == SUB_TILES - 1:
            prev_ref[...] = u[rows - POOL_HALO:]
            state_ref[...] = u[rows - POOL_STATE:]
        yield
        ext = jnp.concatenate([halos[i], u], axis=0)
        pos = t * tm + i * rows + lax.broadcasted_iota(jnp.int32, (rows, 1), 0)
        pooled = []
        for g, w in enumerate(POOL_WINDOWS):
            sl = slice(g * POOL_GROUP, (g + 1) * POOL_GROUP)
            s = ext[:, sl]
            shift = 1
            while shift < w:
                s = s + pltpu.roll(s, shift, axis=0)
                shift *= 2
            inv_cnt = 1.0 / jnp.minimum(pos + 1, w).astype(F32)
            pooled.append(s[POOL_HALO:] * inv_cnt - u[:, sl])
            if g % 2:
                yield
        x1 = x + _pool_project(jnp.concatenate(pooled, axis=-1), wpool_ref, scale_ref[...])
        yield
        x2 = yield from _prompt_attend([x1], ga_ref[...], k_ref, v_ref, wq_ref, wo_ref)
        (y,) = yield from _mlp(x2, gf_ref[...], wup_ref, wdown_ref)
        y_ref[rs, :] = y

    _weave(*[(chain(i), i * POOL_STAGGER) for i in range(SUB_TILES)],
           (_sample_attend(sq_ref, sk_ref, sv_ref, so_ref), POOL_SAMPLE_START))


def _conv_layer_kernel(x_ref, gm_ref, ga_ref, gf_ref, gfin_ref, k_ref, v_ref, sq_ref, sk_ref,
                       sv_ref, win_ref, wconv_ref, wout_ref, wq_ref, wo_ref, wup_ref, wdown_ref,
                       y_ref, state_ref, so_ref, prev_ref):
    t = pl.program_id(1)
    rows = x_ref.shape[0] // SUB_TILES
    halos = [_carried(prev_ref, t)]

    def chain(i):
        rs = slice(i * rows, (i + 1) * rows)
        x = x_ref[rs, :]
        u = _rms(x, gm_ref[...]).astype(BF16)
        yield
        gate_b = _dot(u, win_ref[:, :D_MODEL])
        yield
        gate_c = _dot(u, win_ref[:, D_MODEL:2 * D_MODEL])
        yield
        ch = gate_c * _dot(u, win_ref[:, 2 * D_MODEL:])
        halos.append(ch[rows - CONV_HALO:])
        if i == SUB_TILES - 1:
            prev_ref[...] = ch[rows - CONV_HALO:]
            state_ref[...] = ch[rows - CONV_STATE:]
        yield
        ext = jnp.concatenate([halos[i], ch], axis=0)
        conv = ext * _conv_tap(wconv_ref, 0)
        for back in range(1, CONV_WIDTH):
            conv = conv + pltpu.roll(ext, back, axis=0) * _conv_tap(wconv_ref, back)
        conv = conv[CONV_HALO:]
        x1 = x + _dot((gate_b * conv).astype(BF16), wout_ref[...])
        yield
        x2 = yield from _prompt_attend([x1], ga_ref[...], k_ref, v_ref, wq_ref, wo_ref)
        (x3,) = yield from _mlp(x2, gf_ref[...], wup_ref, wdown_ref)
        y_ref[rs, :] = _rms(x3, gfin_ref[...])

    _weave(*[(chain(i), i * CONV_STAGGER) for i in range(SUB_TILES)],
           (_sample_attend(sq_ref, sk_ref, sv_ref, so_ref), CONV_SAMPLE_START))


def _prompt_layer(kernel, name, layer, row_tile, vmem_limit, x, kb, vb, sq, sk, sv,
                  small, weights, state_rows, halo):
    batch, seq, _ = x.shape
    tiles_per_seq = seq // row_tile
    n_sample = sq.shape[0]
    per_step = n_sample // (batch * tiles_per_seq)
    assert seq % row_tile == 0 and row_tile // SUB_TILES >= POOL_HALO
    assert per_step * batch * tiles_per_seq == n_sample

    step = lambda b, t: b * tiles_per_seq + t
    xspec = pl.BlockSpec((None, row_tile, D_MODEL), lambda b, t: (b, t, 0))
    kvspec = pl.BlockSpec((None, None, N_MEM, D_MODEL), lambda b, t: (layer, b, 0, 0))
    sqspec = pl.BlockSpec((per_step, HEAD_ROWS, LANES), lambda b, t: (step(b, t), 0, 0))
    skvspec = pl.BlockSpec((None, per_step, N_MEM, HEAD_ROWS, LANES),
                           lambda b, t: (layer, step(b, t), 0, 0, 0))
    return pl.pallas_call(
        kernel,
        grid=(batch, tiles_per_seq),
        in_specs=([xspec] + [_vmem()] * len(small) + [kvspec, kvspec, sqspec, skvspec, skvspec]
                  + [_vmem()] * len(weights)),
        out_specs=[xspec, pl.BlockSpec((None, state_rows, D_MODEL), lambda b, t: (b, 0, 0)), sqspec],
        out_shape=[jax.ShapeDtypeStruct(x.shape, F32),
                   jax.ShapeDtypeStruct((batch, state_rows, D_MODEL), F32),
                   jax.ShapeDtypeStruct(sq.shape, F32)],
        scratch_shapes=[pltpu.VMEM((halo, D_MODEL), F32)],
        compiler_params=pltpu.CompilerParams(
            dimension_semantics=("arbitrary", "arbitrary"), vmem_limit_bytes=vmem_limit),
        name=name,
    )(x, *small, kb, vb, sq, sk, sv, *weights)


def _sample_pool_pre_kernel(x_ref, st_ref, gm_ref, ga_ref, wpool_ref, scale_ref, wq_ref,
                            x1_ref, q_ref, newst_ref):
    x = x_ref[...]
    u = _rms(x, gm_ref[...])
    pooled = []
    for g, w in enumerate(POOL_WINDOWS):
        s = u[:, g * POOL_GROUP:(g + 1) * POOL_GROUP]
        for back in range(1, w):
            s = s + st_ref[POOL_STATE - back, :, g * POOL_GROUP:(g + 1) * POOL_GROUP]
        cnt = min(PAST_LEN + 1, w)
        pooled.append(s * (1.0 / cnt) - u[:, g * POOL_GROUP:(g + 1) * POOL_GROUP])
    pooled = jnp.concatenate(pooled, axis=-1)
    x = x + _pool_project(pooled, wpool_ref, scale_ref[...])
    x1_ref[...] = x
    q_ref[...] = _dot(_rms(x, ga_ref[...]).astype(BF16), wq_ref[...]) * (MEM_HEAD_DIM ** -0.5)
    newst_ref[:POOL_STATE - 1] = st_ref[1:]
    newst_ref[POOL_STATE - 1] = u


def _sample_conv_pre_kernel(x_ref, st_ref, gm_ref, ga_ref, win_ref, wconv_ref, wout_ref, wq_ref,
                            x1_ref, q_ref, newst_ref):
    x = x_ref[...]
    u = _rms(x, gm_ref[...]).astype(BF16)
    gate_b = _dot(u, win_ref[:, :D_MODEL])
    ch = _dot(u, win_ref[:, D_MODEL:2 * D_MODEL]) * _dot(u, win_ref[:, 2 * D_MODEL:])
    conv = ch * _conv_tap(wconv_ref, 0)
    for back in range(1, CONV_WIDTH):
        lo = (CONV_STATE - back) * D_MODEL
        conv = conv + st_ref[:, lo:lo + D_MODEL] * _conv_tap(wconv_ref, back)
    x = x + _dot((gate_b * conv).astype(BF16), wout_ref[...])
    x1_ref[...] = x
    q_ref[...] = _dot(_rms(x, ga_ref[...]).astype(BF16), wq_ref[...]) * (MEM_HEAD_DIM ** -0.5)
    keep = (CONV_STATE - 1) * D_MODEL
    newst_ref[:, :keep] = st_ref[:, D_MODEL:]
    newst_ref[:, keep:] = ch


def _sample_pre(kernel, name, x, state, small, weights):
    n = x.shape[0]
    act = jax.ShapeDtypeStruct((n, D_MODEL), F32)
    return pl.pallas_call(
        kernel,
        out_shape=[act, act, jax.ShapeDtypeStruct(state.shape, F32)],
        compiler_params=pltpu.CompilerParams(vmem_limit_bytes=VMEM_LIMIT),
        name=name,
    )(x, state, *small, *weights)


def _head_rows(a):
    n = a.shape[0]
    a = a.reshape(n, MEM_HEADS, MEM_HEAD_DIM // LANES, LANES)
    return a.transpose(0, 2, 1, 3).reshape(n, HEAD_ROWS, LANES)


def _unhead_rows(a):
    n = a.shape[0]
    a = a.reshape(n, MEM_HEAD_DIM // LANES, MEM_HEADS, LANES)
    return a.transpose(0, 2, 1, 3).reshape(n, D_MODEL)


def _cache_rows(c):
    depth, n = c.shape[:2]
    c = c.reshape(depth, n, N_MEM, MEM_HEADS, MEM_HEAD_DIM // LANES, LANES)
    return c.transpose(0, 1, 2, 4, 3, 5).reshape(depth, n, N_MEM, HEAD_ROWS, LANES)


def _sample_post_kernel(final, x_ref, o_ref, gf_ref, gfin_ref, wo_ref, wup_ref, wdown_ref, y_ref):
    x = x_ref[...] + _dot(o_ref[...].astype(BF16), wo_ref[...])
    (x,) = _finish(_mlp([x], gf_ref[...], wup_ref, wdown_ref))
    y_ref[...] = _rms(x, gfin_ref[...]) if final else x


def _sample_post(x1, o, g_ffn, g_final, wo, wup, wdown, final, name):
    return pl.pallas_call(
        functools.partial(_sample_post_kernel, final),
        out_shape=jax.ShapeDtypeStruct(x1.shape, F32),
        compiler_params=pltpu.CompilerParams(vmem_limit_bytes=VMEM_LIMIT),
        name=name,
    )(x1, o, g_ffn, g_final, wo, wup, wdown)


def kernel(x_prompt, x_sample, state_pool, state_conv, cache_mem_k, cache_mem_v, mem_prompt,
           g_mix, g_attn, g_mem, g_ffn, g_final, w_pool, pool_scale,
           w_conv_in, w_conv, w_conv_out, w_q, w_kv, w_o, w_up, w_down):
    batch, seq, _ = x_prompt.shape
    n_sample = x_sample.shape[0]
    assert DEPTH == 2 and x_sample.shape[1] == 1
    assert (batch * N_MEM) % KV_ROW_TILE == 0

    row = lambda a: a.reshape(1, D_MODEL)
    cast = lambda a: a.astype(BF16)
    wpool_b, win_b, wout_b = cast(w_pool[0]), cast(w_conv_in[0]), cast(w_conv_out[0])
    wq_b = [cast(w_q[i]) for i in range(DEPTH)]
    wo_b = [cast(w_o[i]) for i in range(DEPTH)]
    wup_b = [cast(w_up[i]) for i in range(DEPTH)]
    wdown_b = [cast(w_down[i]) for i in range(DEPTH)]
    scale0, wconv0, gfin = row(pool_scale[0]), w_conv[0], row(g_final)

    k_all, v_all, kb_all, vb_all = _mem_kv(mem_prompt.reshape(batch * N_MEM, D_MODEL), g_mem, cast(w_kv))
    kb_all = kb_all.reshape(DEPTH, batch, N_MEM, D_MODEL)
    vb_all = vb_all.reshape(DEPTH, batch, N_MEM, D_MODEL)
    xs = x_sample.reshape(n_sample, D_MODEL)
    ck, cv = _cache_rows(cache_mem_k), _cache_rows(cache_mem_v)

    xs1, q0, pool_s = _sample_pre(
        _sample_pool_pre_kernel, "sample_pool_pre", xs,
        state_pool[0].transpose(1, 0, 2),
        [row(g_mix[0]), row(g_attn[0])], [wpool_b, scale0, wq_b[0]])
    x1, pool_p, o0 = _prompt_layer(
        _pool_layer_kernel, "prompt_pool_layer", 0, POOL_ROW_TILE, VMEM_LIMIT,
        x_prompt, kb_all, vb_all, _head_rows(q0), ck, cv,
        [row(g_mix[0]), row(g_attn[0]), row(g_ffn[0])],
        [wpool_b, scale0, wq_b[0], wo_b[0], wup_b[0], wdown_b[0]], POOL_STATE, POOL_HALO)
    xs2 = _sample_post(xs1, _unhead_rows(o0), row(g_ffn[0]), gfin, wo_b[0], wup_b[0], wdown_b[0],
                       False, "sample_post0")

    xs3, q1, conv_s = _sample_pre(
        _sample_conv_pre_kernel, "sample_conv_pre", xs2,
        state_conv[0].reshape(n_sample, CONV_STATE * D_MODEL),
        [row(g_mix[1]), row(g_attn[1])], [win_b, wconv0, wout_b, wq_b[1]])
    y_prompt, conv_p, o1 = _prompt_layer(
        _conv_layer_kernel, "prompt_conv_layer", 1, CONV_ROW_TILE, CONV_VMEM_LIMIT,
        x1, kb_all, vb_all, _head_rows(q1), ck, cv,
        [row(g_mix[1]), row(g_attn[1]), row(g_ffn[1]), gfin],
        [win_b, wconv0, wout_b, wq_b[1], wo_b[1], wup_b[1], wdown_b[1]], CONV_STATE, CONV_HALO)
    y_sample = _sample_post(xs3, _unhead_rows(o1), row(g_ffn[1]), gfin, wo_b[1], wup_b[1], wdown_b[1],
                            True, "sample_post1")

    mem_shape = (DEPTH, batch, N_MEM, MEM_HEADS, MEM_HEAD_DIM)
    return (y_prompt,
            y_sample.reshape(n_sample, 1, D_MODEL),
            pool_p.reshape(1, batch, POOL_STATE, D_MODEL),
            conv_p.reshape(1, batch, CONV_STATE, D_MODEL),
            _unhead_rows(k_all.reshape(-1, HEAD_ROWS, LANES)).reshape(mem_shape),
            _unhead_rows(v_all.reshape(-1, HEAD_ROWS, LANES)).reshape(mem_shape),
            pool_s.transpose(1, 0, 2)[None],
            conv_s.reshape(1, n_sample, CONV_STATE, D_MODEL))
```

```python
import functools

import jax
import jax.numpy as jnp
from jax import lax
from jax.experimental import pallas as pl
from jax.experimental.pallas import tpu as pltpu

D_MODEL = 1024
DEPTH = 2
PAST_LEN = 16384
POOL_WINDOWS = (2, 4, 8, 16)
POOL_GROUP = D_MODEL // len(POOL_WINDOWS)
POOL_STATE = max(POOL_WINDOWS) - 1
CONV_WIDTH = 3
CONV_STATE = CONV_WIDTH - 1
D_FF = 4 * D_MODEL
N_MEM = 256
MEM_HEADS = 4
MEM_HEAD_DIM = D_MODEL // MEM_HEADS
EPS = 1e-6

SUBLANES = 8
LANES = 128
HEAD_ROWS = D_MODEL // LANES
POOL_HALO = 16
CONV_HALO = SUBLANES
POOL_ROW_TILE = 512
CONV_ROW_TILE = 512
SUB_TILES = 2
KV_ROW_TILE = 512
FF_CHUNK = 1024
POOL_FF_CHUNK = 512
CONV_STAGGER = 0
POOL_SAMPLE_START = 4
POOL_CHUNK = 64
CONV_SAMPLE_START = 2
VMEM_LIMIT = 56 * 1024 * 1024
POOL_VMEM_LIMIT = 60 * 1024 * 1024
CONV_VMEM_LIMIT = 62 * 1024 * 1024

F32 = jnp.float32
BF16 = jnp.bfloat16

_dot = functools.partial(jnp.dot, preferred_element_type=F32)


def _rms(x, g):
    y = x * lax.rsqrt(jnp.mean(x * x, axis=-1, keepdims=True) + EPS)
    return y * g


def _vmem():
    return pl.BlockSpec(memory_space=pltpu.VMEM)


def _finish(stages):
    try:
        while True:
            next(stages)
    except StopIteration as done:
        return done.value


def _weave(*delayed):
    live = list(delayed)
    round_ = 0
    while live:
        for item in list(live):
            stages, first_round = item
            if round_ >= first_round:
                try:
                    next(stages)
                except StopIteration:
                    live.remove(item)
        round_ += 1


def _mlp(xs, g_ffn, wup_ref, wdown_ref, chunk=FF_CHUNK):
    us = [_rms(x, g_ffn).astype(BF16) for x in xs]
    yield
    accs = list(xs)
    for c in range(D_FF // chunk):
        sl = slice(c * chunk, (c + 1) * chunk)
        hs = [jnp.maximum(_dot(u, wup_ref[:, sl]), 0.0) for u in us]
        yield
        accs = [a + _dot((h * h).astype(BF16), wdown_ref[sl, :]) for a, h in zip(accs, hs)]
        yield
    return accs


def _prompt_attend(xs, g_attn, k_ref, v_ref, wq_ref, wo_ref):
    us = [_rms(x, g_attn).astype(BF16) for x in xs]
    yield
    qs = [_dot(u, wq_ref[...]) * (MEM_HEAD_DIM ** -0.5) for u in us]
    yield
    heads = [[] for _ in xs]
    for h in range(MEM_HEADS):
        sl = slice(h * MEM_HEAD_DIM, (h + 1) * MEM_HEAD_DIM)
        ss = [lax.dot_general(q[:, sl].astype(BF16), k_ref[:, sl],
                              (((1,), (1,)), ((), ())), preferred_element_type=F32) for q in qs]
        es = [jnp.exp(s - jnp.max(s, axis=-1, keepdims=True)) for s in ss]
        for i, e in enumerate(es):
            inv = 1.0 / jnp.sum(e, axis=-1, keepdims=True)
            heads[i].append((_dot(e.astype(BF16), v_ref[:, sl]) * inv).astype(BF16))
        yield
    os_ = [jnp.concatenate(hd, axis=-1) for hd in heads]
    return [x + _dot(o, wo_ref[...]) for x, o in zip(xs, os_)]


def _sample_attend(q_ref, k_ref, v_ref, o_ref):
    for j in range(q_ref.shape[0]):
        prod = k_ref[j] * q_ref[j][None]
        part = jnp.sum(prod, axis=-1, keepdims=True)
        s = part + pltpu.roll(part, MEM_HEADS, axis=1)
        e = jnp.exp(s - jnp.max(s, axis=0, keepdims=True))
        inv = 1.0 / jnp.sum(e, axis=0)
        yield
        o_ref[j] = jnp.sum(e * v_ref[j], axis=0) * inv
        yield


def _conv_tap(wconv_ref, back):
    k = CONV_WIDTH - 1 - back
    return wconv_ref[k:k + 1, :]


def _pool_project(pooled, wpool_ref, scale):
    ys = [_dot(p, wpool_ref[g]) for g, p in enumerate(pooled)]
    return jnp.concatenate(ys, axis=-1) * scale


def _carried(prev_ref, t):
    @pl.when(t == 0)
    def _():
        prev_ref[...] = jnp.zeros_like(prev_ref)

    return prev_ref[...]


def _to_head_rows(a):
    tiles = [a[:, (c % MEM_HEADS) * MEM_HEAD_DIM + (c // MEM_HEADS) * LANES:][:, :LANES]
             for c in range(HEAD_ROWS)]
    return jnp.concatenate(tiles, axis=-1).reshape(a.shape[0], HEAD_ROWS, LANES)


def _memkv_kernel(m_ref, g_ref, w_ref, k_ref, v_ref, kb_ref, vb_ref):
    u = _rms(m_ref[...], g_ref[...]).astype(BF16)
    kv = _dot(u, w_ref[...])
    k = kv[:, :D_MODEL]
    v = kv[:, D_MODEL:]
    k_ref[...] = _to_head_rows(k)
    v_ref[...] = _to_head_rows(v)
    kb_ref[...] = k.astype(BF16)
    vb_ref[...] = v.astype(BF16)


def _mem_kv(mem, g_mem, w_kv):
    rows = mem.shape[0]
    out = jax.ShapeDtypeStruct((DEPTH, rows, HEAD_ROWS, LANES), F32)
    outb = jax.ShapeDtypeStruct((DEPTH, rows, D_MODEL), BF16)
    hspec = pl.BlockSpec((None, KV_ROW_TILE, HEAD_ROWS, LANES), lambda l, r: (l, r, 0, 0))
    ospec = pl.BlockSpec((None, KV_ROW_TILE, D_MODEL), lambda l, r: (l, r, 0))
    return pl.pallas_call(
        _memkv_kernel,
        grid=(DEPTH, rows // KV_ROW_TILE),
        in_specs=[
            pl.BlockSpec((KV_ROW_TILE, D_MODEL), lambda l, r: (r, 0)),
            pl.BlockSpec((None, 1, D_MODEL), lambda l, r: (l, 0, 0)),
            pl.BlockSpec((None, D_MODEL, 2 * D_MODEL), lambda l, r: (l, 0, 0)),
        ],
        out_specs=[hspec, hspec, ospec, ospec],
        out_shape=[out, out, outb, outb],
        compiler_params=pltpu.CompilerParams(
            dimension_semantics=("arbitrary", "arbitrary"), vmem_limit_bytes=VMEM_LIMIT),
        name="mem_kv",
    )(mem, g_mem.reshape(DEPTH, 1, D_MODEL), w_kv)


def _pool_layer_kernel(tiles_per_seq, x_ref, gm_ref, ga_ref, gf_ref, k_ref, v_ref, sq_ref, sk_ref,
                       sv_ref, wpool_ref, scale_ref, wq_ref, wo_ref, wup_ref, wdown_ref,
                       y_ref, state_ref, so_ref, prev_ref, mid_ref):
    step = pl.program_id(0)
    last = pl.num_programs(0) - 1
    t = lax.rem(jnp.minimum(step, last - 1), tiles_per_seq)
    tm = x_ref.shape[0]
    rows = tm // SUB_TILES
    carried = _carried(prev_ref, t)

    def start(i, halos):
        rs = slice(i * rows, (i + 1) * rows)
        x = x_ref[rs, :]
        u = _rms(x, gm_ref[...])
        halos.append(u[rows - POOL_HALO:])
        if i == SUB_TILES - 1:
            prev_ref[...] = u[rows - POOL_HALO:]
            state_ref[...] = u[rows - POOL_STATE:]
        yield
        ext = jnp.concatenate([halos[i], u], axis=0)
        x1 = []
        for lo in range(0, rows, POOL_CHUNK):
            hi = lo + POOL_CHUNK
            pos = t * tm + i * rows + lo + lax.broadcasted_iota(jnp.int32, (POOL_CHUNK, 1), 0)
            pooled = []
            for g, w in enumerate(POOL_WINDOWS):
                sl = slice(g * POOL_GROUP, (g + 1) * POOL_GROUP)
                s = ext[lo:hi + POOL_HALO, sl]
                shift = 1
                while shift < w:
                    s = s + pltpu.roll(s, shift, axis=0)
                    shift *= 2
                inv_cnt = 1.0 / jnp.minimum(pos + 1, w).astype(F32)
                pooled.append((s[POOL_HALO:] * inv_cnt - u[lo:hi, sl]).astype(BF16))
            x1.append(x[lo:hi] + _pool_project(pooled, wpool_ref, scale_ref[...]))
            yield
        x1 = jnp.concatenate(x1, axis=0)
        (x2,) = yield from _prompt_attend([x1], ga_ref[...], k_ref, v_ref, wq_ref, wo_ref)
        mid_ref[rs, :] = x2

    def finish(i):
        rs = slice(i * rows, (i + 1) * rows)
        x2 = mid_ref[rs, :]
        (y,) = yield from _mlp([x2], gf_ref[...], wup_ref, wdown_ref, POOL_FF_CHUNK)
        y_ref[rs, :] = y

    def starts():
        halos = [carried]
        return [(start(i, halos), 0) for i in range(SUB_TILES)]

    finishes = lambda: [(finish(i), 0) for i in range(SUB_TILES)]
    sample = lambda: (_sample_attend(sq_ref, sk_ref, sv_ref, so_ref), POOL_SAMPLE_START)

    @pl.when(step == 0)
    def _():
        _weave(*starts(), sample())

    @pl.when(jnp.logical_and(step > 0, step < last))
    def _():
        _weave(*finishes(), *starts(), sample())

    @pl.when(step == last)
    def _():
        _weave(*finishes())


def _conv_layer_kernel(tiles_per_seq, x_ref, gm_ref, ga_ref, gf_ref, gfin_ref, k_ref, v_ref,
                       sq_ref, sk_ref, sv_ref, win_ref, wconv_ref, wout_ref, wq_ref, wo_ref,
                       wup_ref, wdown_ref, y_ref, state_ref, so_ref, prev_ref):
    t = lax.rem(pl.program_id(0), tiles_per_seq)
    rows = x_ref.shape[0] // SUB_TILES
    halos = [_carried(prev_ref, t)]

    def chain(i):
        rs = slice(i * rows, (i + 1) * rows)
        x = x_ref[rs, :]
        u = _rms(x, gm_ref[...]).astype(BF16)
        yield
        gate_b = _dot(u, win_ref[:, :D_MODEL])
        yield
        gate_c = _dot(u, win_ref[:, D_MODEL:2 * D_MODEL])
        yield
        ch = gate_c * _dot(u, win_ref[:, 2 * D_MODEL:])
        halos.append(ch[rows - CONV_HALO:])
        if i == SUB_TILES - 1:
            prev_ref[...] = ch[rows - CONV_HALO:]
            state_ref[...] = ch[rows - CONV_STATE:]
        yield
        ext = jnp.concatenate([halos[i], ch], axis=0)
        conv = ext * _conv_tap(wconv_ref, 0)
        for back in range(1, CONV_WIDTH):
            conv = conv + pltpu.roll(ext, back, axis=0) * _conv_tap(wconv_ref, back)
        conv = conv[CONV_HALO:]
        x1 = x + _dot((gate_b * conv).astype(BF16), wout_ref[...])
        yield
        x2 = yield from _prompt_attend([x1], ga_ref[...], k_ref, v_ref, wq_ref, wo_ref)
        (x3,) = yield from _mlp(x2, gf_ref[...], wup_ref, wdown_ref)
        y_ref[rs, :] = _rms(x3, gfin_ref[...])

    _weave(*[(chain(i), i * CONV_STAGGER) for i in range(SUB_TILES)],
           (_sample_attend(sq_ref, sk_ref, sv_ref, so_ref), CONV_SAMPLE_START))


def _prompt_layer(kernel, name, layer, row_tile, vmem_limit, pipelined, x, kb, vb, sq, sk, sv,
                  small, weights, state_rows, halo):
    batch, seq, _ = x.shape
    tiles_per_seq = seq // row_tile
    n_tiles = batch * tiles_per_seq
    n_sample = sq.shape[0]
    per_step = n_sample // n_tiles
    assert seq % row_tile == 0 and row_tile // SUB_TILES >= POOL_HALO
    assert per_step * n_tiles == n_sample

    started = (lambda i: jnp.minimum(i, n_tiles - 1)) if pipelined else (lambda i: i)
    finished = (lambda i: jnp.maximum(i - 1, 0)) if pipelined else (lambda i: i)
    seq_of = lambda j: lax.div(j, tiles_per_seq)
    tile_of = lambda j: lax.rem(j, tiles_per_seq)

    xspec = pl.BlockSpec((None, row_tile, D_MODEL),
                         lambda i: (seq_of(started(i)), tile_of(started(i)), 0))
    yspec = pl.BlockSpec((None, row_tile, D_MODEL),
                         lambda i: (seq_of(finished(i)), tile_of(finished(i)), 0))
    stspec = pl.BlockSpec((None, state_rows, D_MODEL), lambda i: (seq_of(started(i)), 0, 0))
    kvspec = pl.BlockSpec((None, None, N_MEM, D_MODEL),
                          lambda i: (layer, seq_of(started(i)), 0, 0))
    sqspec = pl.BlockSpec((per_step, HEAD_ROWS, LANES), lambda i: (started(i), 0, 0))
    skvspec = pl.BlockSpec((None, per_step, N_MEM, HEAD_ROWS, LANES),
                           lambda i: (layer, started(i), 0, 0, 0))
    scratch = [pltpu.VMEM((halo, D_MODEL), F32)]
    if pipelined:
        scratch.append(pltpu.VMEM((row_tile, D_MODEL), F32))
    return pl.pallas_call(
        functools.partial(kernel, tiles_per_seq),
        grid=(n_tiles + 1 if pipelined else n_tiles,),
        in_specs=([xspec] + [_vmem()] * len(small) + [kvspec, kvspec, sqspec, skvspec, skvspec]
                  + [_vmem()] * len(weights)),
        out_specs=[yspec, stspec, sqspec],
        out_shape=[jax.ShapeDtypeStruct(x.shape, F32),
                   jax.ShapeDtypeStruct((batch, state_rows, D_MODEL), F32),
                   jax.ShapeDtypeStruct(sq.shape, F32)],
        scratch_shapes=scratch,
        compiler_params=pltpu.CompilerParams(
            dimension_semantics=("arbitrary",), vmem_limit_bytes=vmem_limit),
        name=name,
    )(x, *small, kb, vb, sq, sk, sv, *weights)


def _sample_pool_pre_kernel(x_ref, st_ref, gm_ref, ga_ref, wpool_ref, scale_ref, wq_ref,
                            x1_ref, q_ref, newst_ref):
    x = x_ref[...]
    u = _rms(x, gm_ref[...])
    pooled = []
    for g, w in enumerate(POOL_WINDOWS):
        s = u[:, g * POOL_GROUP:(g + 1) * POOL_GROUP]
        for back in range(1, w):
            s = s + st_ref[POOL_STATE - back, :, g * POOL_GROUP:(g + 1) * POOL_GROUP]
        cnt = min(PAST_LEN + 1, w)
        pooled.append((s * (1.0 / cnt) - u[:, g * POOL_GROUP:(g + 1) * POOL_GROUP]).astype(BF16))
    x = x + _pool_project(pooled, wpool_ref, scale_ref[...])
    x1_ref[...] = x
    q_ref[...] = _dot(_rms(x, ga_ref[...]).astype(BF16), wq_ref[...]) * (MEM_HEAD_DIM ** -0.5)
    newst_ref[:POOL_STATE - 1] = st_ref[1:]
    newst_ref[POOL_STATE - 1] = u


def _sample_conv_pre_kernel(x_ref, st_ref, gm_ref, ga_ref, win_ref, wconv_ref, wout_ref, wq_ref,
                            x1_ref, q_ref, newst_ref):
    x = x_ref[...]
    u = _rms(x, gm_ref[...]).astype(BF16)
    gate_b = _dot(u, win_ref[:, :D_MODEL])
    ch = _dot(u, win_ref[:, D_MODEL:2 * D_MODEL]) * _dot(u, win_ref[:, 2 * D_MODEL:])
    conv = ch * _conv_tap(wconv_ref, 0)
    for back in range(1, CONV_WIDTH):
        lo = (CONV_STATE - back) * D_MODEL
        conv = conv + st_ref[:, lo:lo + D_MODEL] * _conv_tap(wconv_ref, back)
    x = x + _dot((gate_b * conv).astype(BF16), wout_ref[...])
    x1_ref[...] = x
    q_ref[...] = _dot(_rms(x, ga_ref[...]).astype(BF16), wq_ref[...]) * (MEM_HEAD_DIM ** -0.5)
    keep = (CONV_STATE - 1) * D_MODEL
    newst_ref[:, :keep] = st_ref[:, D_MODEL:]
    newst_ref[:, keep:] = ch


def _sample_pre(kernel, name, x, state, small, weights):
    n = x.shape[0]
    act = jax.ShapeDtypeStruct((n, D_MODEL), F32)
    return pl.pallas_call(
        kernel,
        out_shape=[act, act, jax.ShapeDtypeStruct(state.shape, F32)],
        compiler_params=pltpu.CompilerParams(vmem_limit_bytes=VMEM_LIMIT),
        name=name,
    )(x, state, *small, *weights)


def _head_rows(a):
    n = a.shape[0]
    a = a.reshape(n, MEM_HEADS, MEM_HEAD_DIM // LANES, LANES)
    return a.transpose(0, 2, 1, 3).reshape(n, HEAD_ROWS, LANES)


def _unhead_rows(a):
    n = a.shape[0]
    a = a.reshape(n, MEM_HEAD_DIM // LANES, MEM_HEADS, LANES)
    return a.transpose(0, 2, 1, 3).reshape(n, D_MODEL)


def _cache_rows(c):
    depth, n = c.shape[:2]
    c = c.reshape(depth, n, N_MEM, MEM_HEADS, MEM_HEAD_DIM // LANES, LANES)
    return c.transpose(0, 1, 2, 4, 3, 5).reshape(depth, n, N_MEM, HEAD_ROWS, LANES)


def _sample_post_kernel(final, x_ref, o_ref, gf_ref, gfin_ref, wo_ref, wup_ref, wdown_ref, y_ref):
    x = x_ref[...] + _dot(o_ref[...].astype(BF16), wo_ref[...])
    (x,) = _finish(_mlp([x], gf_ref[...], wup_ref, wdown_ref))
    y_ref[...] = _rms(x, gfin_ref[...]) if final else x


def _sample_post(x1, o, g_ffn, g_final, wo, wup, wdown, final, name):
    return pl.pallas_call(
        functools.partial(_sample_post_kernel, final),
        out_shape=jax.ShapeDtypeStruct(x1.shape, F32),
        compiler_params=pltpu.CompilerParams(vmem_limit_bytes=VMEM_LIMIT),
        name=name,
    )(x1, o, g_ffn, g_final, wo, wup, wdown)


def kernel(x_prompt, x_sample, state_pool, state_conv, cache_mem_k, cache_mem_v, mem_prompt,
           g_mix, g_attn, g_mem, g_ffn, g_final, w_pool, pool_scale,
           w_conv_in, w_conv, w_conv_out, w_q, w_kv, w_o, w_up, w_down):
    batch, seq, _ = x_prompt.shape
    n_sample = x_sample.shape[0]
    assert DEPTH == 2 and x_sample.shape[1] == 1
    assert (batch * N_MEM) % KV_ROW_TILE == 0

    row = lambda a: a.reshape(1, D_MODEL)
    cast = lambda a: a.astype(BF16)
    wpool_b, win_b, wout_b = cast(w_pool[0]), cast(w_conv_in[0]), cast(w_conv_out[0])
    wq_b = [cast(w_q[i]) for i in range(DEPTH)]
    wo_b = [cast(w_o[i]) for i in range(DEPTH)]
    wup_b = [cast(w_up[i]) for i in range(DEPTH)]
    wdown_b = [cast(w_down[i]) for i in range(DEPTH)]
    scale0, wconv0, gfin = row(pool_scale[0]), w_conv[0], row(g_final)

    k_all, v_all, kb_all, vb_all = _mem_kv(mem_prompt.reshape(batch * N_MEM, D_MODEL), g_mem, cast(w_kv))
    kb_all = kb_all.reshape(DEPTH, batch, N_MEM, D_MODEL)
    vb_all = vb_all.reshape(DEPTH, batch, N_MEM, D_MODEL)
    xs = x_sample.reshape(n_sample, D_MODEL)
    ck, cv = _cache_rows(cache_mem_k), _cache_rows(cache_mem_v)

    xs1, q0, pool_s = _sample_pre(
        _sample_pool_pre_kernel, "sample_pool_pre", xs,
        state_pool[0].transpose(1, 0, 2),
        [row(g_mix[0]), row(g_attn[0])], [wpool_b, scale0, wq_b[0]])
    x1, pool_p, o0 = _prompt_layer(
        _pool_layer_kernel, "prompt_pool_layer", 0, POOL_ROW_TILE, POOL_VMEM_LIMIT, True,
        x_prompt, kb_all, vb_all, _head_rows(q0), ck, cv,
        [row(g_mix[0]), row(g_attn[0]), row(g_ffn[0])],
        [wpool_b, scale0, wq_b[0], wo_b[0], wup_b[0], wdown_b[0]], POOL_STATE, POOL_HALO)
    xs2 = _sample_post(xs1, _unhead_rows(o0), row(g_ffn[0]), gfin, wo_b[0], wup_b[0], wdown_b[0],
                       False, "sample_post0")

    xs3, q1, conv_s = _sample_pre(
        _sample_conv_pre_kernel, "sample_conv_pre", xs2,
        state_conv[0].reshape(n_sample, CONV_STATE * D_MODEL),
        [row(g_mix[1]), row(g_attn[1])], [win_b, wconv0, wout_b, wq_b[1]])
    y_prompt, conv_p, o1 = _prompt_layer(
        _conv_layer_kernel, "prompt_conv_layer", 1, CONV_ROW_TILE, CONV_VMEM_LIMIT, False,
        x1, kb_all, vb_all, _head_rows(q1), ck, cv,
        [row(g_mix[1]), row(g_attn[1]), row(g_ffn[1]), gfin],
        [win_b, wconv0, wout_b, wq_b[1], wo_b[1], wup_b[1], wdown_b[1]], CONV_STATE, CONV_HALO)
    y_sample = _sample_post(xs3, _unhead_rows(o1), row(g_ffn[1]), gfin, wo_b[1], wup_b[1], wdown_b[1],
                            True, "sample_post1")

    mem_shape = (DEPTH, batch, N_MEM, MEM_HEADS, MEM_HEAD_DIM)
    return (y_prompt,
            y_sample.reshape(n_sample, 1, D_MODEL),
            pool_p.reshape(1, batch, POOL_STATE, D_MODEL),
            conv_p.reshape(1, batch, CONV_STATE, D_MODEL),
            _unhead_rows(k_all.reshape(-1, HEAD_ROWS, LANES)).reshape(mem_shape),
            _unhead_rows(v_all.reshape(-1, HEAD_ROWS, LANES)).reshape(mem_shape),
            pool_s.transpose(1, 0, 2)[None],
            conv_s.reshape(1, n_sample, CONV_STATE, D_MODEL))
```

```python
import functools

import jax
import jax.numpy as jnp
from jax import lax
from jax.experimental import pallas as pl
from jax.experimental.pallas import tpu as pltpu

D_MODEL = 1024
DEPTH = 2
PAST_LEN = 16384
POOL_WINDOWS = (2, 4, 8, 16)
POOL_GROUP = D_MODEL // len(POOL_WINDOWS)
POOL_STATE = max(POOL_WINDOWS) - 1
CONV_WIDTH = 3
CONV_STATE = CONV_WIDTH - 1
D_FF = 4 * D_MODEL
N_MEM = 256
MEM_HEADS = 4
MEM_HEAD_DIM = D_MODEL // MEM_HEADS
EPS = 1e-6

SUBLANES = 8
LANES = 128
HEAD_ROWS = D_MODEL // LANES
POOL_HALO = 16
CONV_HALO = SUBLANES
POOL_ROW_TILE = 512
CONV_ROW_TILE = 512
SUB_TILES = 2
KV_ROW_TILE = 512
FF_CHUNK = 1024
POOL_FF_CHUNK = 512
CONV_FF_CHUNK = 1024
POOL_SAMPLE_START = 4
POOL_CHUNK = 64
CONV_SAMPLE_START = 2
VMEM_LIMIT = 56 * 1024 * 1024
POOL_VMEM_LIMIT = 60 * 1024 * 1024
CONV_VMEM_LIMIT = 62 * 1024 * 1024

F32 = jnp.float32
BF16 = jnp.bfloat16

_dot = functools.partial(jnp.dot, preferred_element_type=F32)


def _rms(x, g):
    y = x * lax.rsqrt(jnp.mean(x * x, axis=-1, keepdims=True) + EPS)
    return y * g


def _vmem():
    return pl.BlockSpec(memory_space=pltpu.VMEM)


def _weave(*delayed):
    live = list(delayed)
    round_ = 0
    while live:
        for item in list(live):
            stages, first_round = item
            if round_ >= first_round:
                try:
                    next(stages)
                except StopIteration:
                    live.remove(item)
        round_ += 1


def _mlp(xs, g_ffn, wup_ref, wdown_ref, chunk=FF_CHUNK):
    us = [_rms(x, g_ffn).astype(BF16) for x in xs]
    yield
    accs = list(xs)
    for c in range(D_FF // chunk):
        sl = slice(c * chunk, (c + 1) * chunk)
        hs = [jnp.maximum(_dot(u, wup_ref[:, sl]), 0.0) for u in us]
        yield
        accs = [a + _dot((h * h).astype(BF16), wdown_ref[sl, :]) for a, h in zip(accs, hs)]
        yield
    return accs


def _prompt_attend(xs, g_attn, k_ref, v_ref, wq_ref, wo_ref):
    us = [_rms(x, g_attn).astype(BF16) for x in xs]
    yield
    qs = [_dot(u, wq_ref[...]) * (MEM_HEAD_DIM ** -0.5) for u in us]
    yield
    heads = [[] for _ in xs]
    for h in range(MEM_HEADS):
        sl = slice(h * MEM_HEAD_DIM, (h + 1) * MEM_HEAD_DIM)
        ss = [lax.dot_general(q[:, sl].astype(BF16), k_ref[:, sl],
                              (((1,), (1,)), ((), ())), preferred_element_type=F32) for q in qs]
        es = [jnp.exp(s - jnp.max(s, axis=-1, keepdims=True)) for s in ss]
        for i, e in enumerate(es):
            inv = 1.0 / jnp.sum(e, axis=-1, keepdims=True)
            heads[i].append((_dot(e.astype(BF16), v_ref[:, sl]) * inv).astype(BF16))
        yield
    os_ = [jnp.concatenate(hd, axis=-1) for hd in heads]
    return [x + _dot(o, wo_ref[...]) for x, o in zip(xs, os_)]


def _sample_attend(q_ref, k_ref, v_ref, o_ref):
    for j in range(q_ref.shape[0]):
        prod = k_ref[j] * q_ref[j][None]
        part = jnp.sum(prod, axis=-1, keepdims=True)
        s = part + pltpu.roll(part, MEM_HEADS, axis=1)
        e = jnp.exp(s - jnp.max(s, axis=0, keepdims=True))
        inv = 1.0 / jnp.sum(e, axis=0)
        yield
        o_ref[j] = jnp.sum(e * v_ref[j], axis=0) * inv
        yield


def _conv_tap(wconv_ref, back):
    k = CONV_WIDTH - 1 - back
    return wconv_ref[k:k + 1, :]


def _pool_project(pooled, wpool_ref, scale):
    ys = [_dot(p, wpool_ref[g]) for g, p in enumerate(pooled)]
    return jnp.concatenate(ys, axis=-1) * scale


def _carried(prev_ref, t):
    @pl.when(t == 0)
    def _():
        prev_ref[...] = jnp.zeros_like(prev_ref)

    return prev_ref[...]


def _to_head_rows(a):
    tiles = [a[:, (c % MEM_HEADS) * MEM_HEAD_DIM + (c // MEM_HEADS) * LANES:][:, :LANES]
             for c in range(HEAD_ROWS)]
    return jnp.concatenate(tiles, axis=-1).reshape(a.shape[0], HEAD_ROWS, LANES)


def _memkv_kernel(m_ref, g_ref, w_ref, k_ref, v_ref, kb_ref, vb_ref, wb_ref):
    @pl.when(pl.program_id(1) == 0)
    def _():
        wb_ref[...] = w_ref[...].astype(BF16)

    u = _rms(m_ref[...], g_ref[...]).astype(BF16)
    kv = _dot(u, wb_ref[...])
    k = kv[:, :D_MODEL]
    v = kv[:, D_MODEL:]
    k_ref[...] = _to_head_rows(k)
    v_ref[...] = _to_head_rows(v)
    kb_ref[...] = k.astype(BF16)
    vb_ref[...] = v.astype(BF16)


def _mem_kv(mem, g_mem, w_kv):
    rows = mem.shape[0]
    out = jax.ShapeDtypeStruct((DEPTH, rows, HEAD_ROWS, LANES), F32)
    outb = jax.ShapeDtypeStruct((DEPTH, rows, D_MODEL), BF16)
    hspec = pl.BlockSpec((None, KV_ROW_TILE, HEAD_ROWS, LANES), lambda l, r: (l, r, 0, 0))
    ospec = pl.BlockSpec((None, KV_ROW_TILE, D_MODEL), lambda l, r: (l, r, 0))
    return pl.pallas_call(
        _memkv_kernel,
        grid=(DEPTH, rows // KV_ROW_TILE),
        in_specs=[
            pl.BlockSpec((KV_ROW_TILE, D_MODEL), lambda l, r: (r, 0)),
            pl.BlockSpec((None, 1, D_MODEL), lambda l, r: (l, 0, 0)),
            pl.BlockSpec((None, D_MODEL, 2 * D_MODEL), lambda l, r: (l, 0, 0)),
        ],
        out_specs=[hspec, hspec, ospec, ospec],
        out_shape=[out, out, outb, outb],
        scratch_shapes=[pltpu.VMEM((D_MODEL, 2 * D_MODEL), BF16)],
        compiler_params=pltpu.CompilerParams(
            dimension_semantics=("arbitrary", "arbitrary"), vmem_limit_bytes=VMEM_LIMIT),
        name="mem_kv",
    )(mem, g_mem.reshape(DEPTH, 1, D_MODEL), w_kv)


def _pipelined_step(start, finish, carried, sample):
    step = pl.program_id(0)
    last = pl.num_programs(0) - 1

    def starts():
        halos = [carried]
        return [(start(i, halos), 0) for i in range(SUB_TILES)]

    finishes = lambda: [(finish(i), 0) for i in range(SUB_TILES)]

    @pl.when(step == 0)
    def _():
        _weave(*starts(), sample())

    @pl.when(jnp.logical_and(step > 0, step < last))
    def _():
        _weave(*finishes(), *starts(), sample())

    @pl.when(step == last)
    def _():
        _weave(*finishes())


def _started_tile(tiles_per_seq):
    step = jnp.minimum(pl.program_id(0), pl.num_programs(0) - 2)
    return lax.rem(step, tiles_per_seq)


def _pool_layer_kernel(tiles_per_seq, x_ref, gm_ref, ga_ref, gf_ref, k_ref, v_ref, sq_ref, sk_ref,
                       sv_ref, wpool_ref, scale_ref, wq_ref, wo_ref, wup_ref, wdown_ref,
                       y_ref, state_ref, so_ref, prev_ref, mid_ref):
    t = _started_tile(tiles_per_seq)
    tm = x_ref.shape[0]
    rows = tm // SUB_TILES
    carried = _carried(prev_ref, t)

    def start(i, halos):
        rs = slice(i * rows, (i + 1) * rows)
        x = x_ref[rs, :]
        u = _rms(x, gm_ref[...])
        halos.append(u[rows - POOL_HALO:])
        if i == SUB_TILES - 1:
            prev_ref[...] = u[rows - POOL_HALO:]
            state_ref[...] = u[rows - POOL_STATE:]
        yield
        ext = jnp.concatenate([halos[i], u], axis=0)
        x1 = []
        for lo in range(0, rows, POOL_CHUNK):
            hi = lo + POOL_CHUNK
            pos = t * tm + i * rows + lo + lax.broadcasted_iota(jnp.int32, (POOL_CHUNK, 1), 0)
            pooled = []
            for g, w in enumerate(POOL_WINDOWS):
                sl = slice(g * POOL_GROUP, (g + 1) * POOL_GROUP)
                s = ext[lo:hi + POOL_HALO, sl]
                shift = 1
                while shift < w:
                    s = s + pltpu.roll(s, shift, axis=0)
                    shift *= 2
                inv_cnt = 1.0 / jnp.minimum(pos + 1, w).astype(F32)
                pooled.append((s[POOL_HALO:] * inv_cnt - u[lo:hi, sl]).astype(BF16))
            x1.append(x[lo:hi] + _pool_project(pooled, wpool_ref, scale_ref[...]))
            yield
        x1 = jnp.concatenate(x1, axis=0)
        (x2,) = yield from _prompt_attend([x1], ga_ref[...], k_ref, v_ref, wq_ref, wo_ref)
        mid_ref[rs, :] = x2

    def finish(i):
        rs = slice(i * rows, (i + 1) * rows)
        x2 = mid_ref[rs, :]
        (y,) = yield from _mlp([x2], gf_ref[...], wup_ref, wdown_ref, POOL_FF_CHUNK)
        y_ref[rs, :] = y

    _pipelined_step(start, finish, carried,
                    lambda: (_sample_attend(sq_ref, sk_ref, sv_ref, so_ref), POOL_SAMPLE_START))


def _conv_layer_kernel(tiles_per_seq, x_ref, gm_ref, ga_ref, gf_ref, gfin_ref, k_ref, v_ref,
                       sq_ref, sk_ref, sv_ref, win_ref, wconv_ref, wout_ref, wq_ref, wo_ref,
                       wup_ref, wdown_ref, y_ref, state_ref, so_ref, prev_ref):
    t = lax.rem(pl.program_id(0), tiles_per_seq)
    rows = x_ref.shape[0] // SUB_TILES
    halos = [_carried(prev_ref, t)]

    def chain(i):
        rs = slice(i * rows, (i + 1) * rows)
        x = x_ref[rs, :]
        u = _rms(x, gm_ref[...]).astype(BF16)
        yield
        gate_b = _dot(u, win_ref[:, :D_MODEL])
        yield
        gate_c = _dot(u, win_ref[:, D_MODEL:2 * D_MODEL])
        yield
        ch = gate_c * _dot(u, win_ref[:, 2 * D_MODEL:])
        halos.append(ch[rows - CONV_HALO:])
        if i == SUB_TILES - 1:
            prev_ref[...] = ch[rows - CONV_HALO:]
            state_ref[...] = ch[rows - CONV_STATE:]
        yield
        ext = jnp.concatenate([halos[i], ch], axis=0)
        conv = ext * _conv_tap(wconv_ref, 0)
        for back in range(1, CONV_WIDTH):
            conv = conv + pltpu.roll(ext, back, axis=0) * _conv_tap(wconv_ref, back)
        conv = conv[CONV_HALO:]
        x1 = x + _dot((gate_b * conv).astype(BF16), wout_ref[...])
        yield
        x2 = yield from _prompt_attend([x1], ga_ref[...], k_ref, v_ref, wq_ref, wo_ref)
        (x3,) = yield from _mlp(x2, gf_ref[...], wup_ref, wdown_ref, CONV_FF_CHUNK)
        y_ref[rs, :] = _rms(x3, gfin_ref[...])

    _weave(*[(chain(i), 0) for i in range(SUB_TILES)],
           (_sample_attend(sq_ref, sk_ref, sv_ref, so_ref), CONV_SAMPLE_START))


def _prompt_layer(kernel, name, layer, row_tile, vmem_limit, pipelined, x, kb, vb, sq, sk, sv,
                  small, weights, state_rows, halo):
    batch, seq, _ = x.shape
    tiles_per_seq = seq // row_tile
    n_tiles = batch * tiles_per_seq
    n_sample = sq.shape[0]
    per_step = n_sample // n_tiles
    assert seq % row_tile == 0 and row_tile // SUB_TILES >= POOL_HALO
    assert per_step * n_tiles == n_sample

    started = (lambda i: jnp.minimum(i, n_tiles - 1)) if pipelined else (lambda i: i)
    finished = (lambda i: jnp.maximum(i - 1, 0)) if pipelined else (lambda i: i)
    seq_of = lambda j: lax.div(j, tiles_per_seq)
    tile_of = lambda j: lax.rem(j, tiles_per_seq)

    xspec = pl.BlockSpec((None, row_tile, D_MODEL),
                         lambda i: (seq_of(started(i)), tile_of(started(i)), 0))
    yspec = pl.BlockSpec((None, row_tile, D_MODEL),
                         lambda i: (seq_of(finished(i)), tile_of(finished(i)), 0))
    stspec = pl.BlockSpec((None, state_rows, D_MODEL), lambda i: (seq_of(started(i)), 0, 0))
    kvspec = pl.BlockSpec((None, None, N_MEM, D_MODEL),
                          lambda i: (layer, seq_of(started(i)), 0, 0))
    sqspec = pl.BlockSpec((per_step, HEAD_ROWS, LANES), lambda i: (started(i), 0, 0))
    skvspec = pl.BlockSpec((None, per_step, N_MEM, HEAD_ROWS, LANES),
                           lambda i: (layer, started(i), 0, 0, 0))
    scratch = [pltpu.VMEM((halo, D_MODEL), F32)]
    if pipelined:
        scratch.append(pltpu.VMEM((row_tile, D_MODEL), F32))
    return pl.pallas_call(
        functools.partial(kernel, tiles_per_seq),
        grid=(n_tiles + 1 if pipelined else n_tiles,),
        in_specs=([xspec] + [_vmem()] * len(small) + [kvspec, kvspec, sqspec, skvspec, skvspec]
                  + [_vmem()] * len(weights)),
        out_specs=[yspec, stspec, sqspec],
        out_shape=[jax.ShapeDtypeStruct(x.shape, F32),
                   jax.ShapeDtypeStruct((batch, state_rows, D_MODEL), F32),
                   jax.ShapeDtypeStruct(sq.shape, F32)],
        scratch_shapes=scratch,
        compiler_params=pltpu.CompilerParams(
            dimension_semantics=("arbitrary",), vmem_limit_bytes=vmem_limit),
        name=name,
    )(x, *small, kb, vb, sq, sk, sv, *weights)


def _sample_pool_pre_kernel(x_ref, st_ref, gm_ref, ga_ref, wpool_ref, scale_ref, wq_ref,
                            x1_ref, q_ref, newst_ref):
    x = x_ref[...]
    u = _rms(x, gm_ref[...])
    pooled = []
    for g, w in enumerate(POOL_WINDOWS):
        s = u[:, g * POOL_GROUP:(g + 1) * POOL_GROUP]
        for back in range(1, w):
            s = s + st_ref[POOL_STATE - back, :, g * POOL_GROUP:(g + 1) * POOL_GROUP]
        cnt = min(PAST_LEN + 1, w)
        pooled.append((s * (1.0 / cnt) - u[:, g * POOL_GROUP:(g + 1) * POOL_GROUP]).astype(BF16))
    x = x + _pool_project(pooled, wpool_ref, scale_ref[...])
    x1_ref[...] = x
    q_ref[...] = _dot(_rms(x, ga_ref[...]).astype(BF16), wq_ref[...]) * (MEM_HEAD_DIM ** -0.5)
    newst_ref[:POOL_STATE - 1] = st_ref[1:]
    newst_ref[POOL_STATE - 1] = u


def _sample_conv_pre_kernel(x_ref, st_ref, gm_ref, ga_ref, win_ref, wconv_ref, wout_ref, wq_ref,
                            x1_ref, q_ref, newst_ref):
    x = x_ref[...]
    u = _rms(x, gm_ref[...]).astype(BF16)
    gate_b = _dot(u, win_ref[:, :D_MODEL])
    ch = _dot(u, win_ref[:, D_MODEL:2 * D_MODEL]) * _dot(u, win_ref[:, 2 * D_MODEL:])
    conv = ch * _conv_tap(wconv_ref, 0)
    for back in range(1, CONV_WIDTH):
        lo = (CONV_STATE - back) * D_MODEL
        conv = conv + st_ref[:, lo:lo + D_MODEL] * _conv_tap(wconv_ref, back)
    x = x + _dot((gate_b * conv).astype(BF16), wout_ref[...])
    x1_ref[...] = x
    q_ref[...] = _dot(_rms(x, ga_ref[...]).astype(BF16), wq_ref[...]) * (MEM_HEAD_DIM ** -0.5)
    keep = (CONV_STATE - 1) * D_MODEL
    newst_ref[:, :keep] = st_ref[:, D_MODEL:]
    newst_ref[:, keep:] = ch


def _sample_pre(kernel, name, x, state, small, weights):
    n = x.shape[0]
    act = jax.ShapeDtypeStruct((n, D_MODEL), F32)
    return pl.pallas_call(
        kernel,
        out_shape=[act, act, jax.ShapeDtypeStruct(state.shape, F32)],
        compiler_params=pltpu.CompilerParams(vmem_limit_bytes=VMEM_LIMIT),
        name=name,
    )(x, state, *small, *weights)


def _head_rows(a):
    n = a.shape[0]
    a = a.reshape(n, MEM_HEADS, MEM_HEAD_DIM // LANES, LANES)
    return a.transpose(0, 2, 1, 3).reshape(n, HEAD_ROWS, LANES)


def _unhead_rows(a):
    n = a.shape[0]
    a = a.reshape(n, MEM_HEAD_DIM // LANES, MEM_HEADS, LANES)
    return a.transpose(0, 2, 1, 3).reshape(n, D_MODEL)


def _cache_rows(c):
    depth, n = c.shape[:2]
    c = c.reshape(depth, n, N_MEM, MEM_HEADS, MEM_HEAD_DIM // LANES, LANES)
    return c.transpose(0, 1, 2, 4, 3, 5).reshape(depth, n, N_MEM, HEAD_ROWS, LANES)


def _sample_post_kernel(final, x_ref, o_ref, gf_ref, gfin_ref, wo_ref, wup_ref, wdown_ref, y_ref,
                        u_ref, acc_ref):
    c = pl.program_id(0)

    @pl.when(c == 0)
    def _():
        x = x_ref[...] + _dot(o_ref[...].astype(BF16), wo_ref[...])
        u_ref[...] = _rms(x, gf_ref[...]).astype(BF16)
        acc_ref[...] = x

    h = jnp.maximum(_dot(u_ref[...], wup_ref[...]), 0.0)
    acc_ref[...] += _dot((h * h).astype(BF16), wdown_ref[...])

    @pl.when(c == pl.num_programs(0) - 1)
    def _():
        x = acc_ref[...]
        y_ref[...] = _rms(x, gfin_ref[...]) if final else x


def _sample_post(x1, o, g_ffn, g_final, wo, wup, wdown, final, name):
    n = x1.shape[0]
    whole = lambda shape: pl.BlockSpec(shape, lambda c: (0,) * len(shape))
    act = whole((n, D_MODEL))
    vec = whole((1, D_MODEL))
    return pl.pallas_call(
        functools.partial(_sample_post_kernel, final),
        grid=(D_FF // FF_CHUNK,),
        in_specs=[act, act, vec, vec, whole((D_MODEL, D_MODEL)),
                  pl.BlockSpec((D_MODEL, FF_CHUNK), lambda c: (0, c)),
                  pl.BlockSpec((FF_CHUNK, D_MODEL), lambda c: (c, 0))],
        out_specs=act,
        out_shape=jax.ShapeDtypeStruct(x1.shape, F32),
        scratch_shapes=[pltpu.VMEM((n, D_MODEL), BF16), pltpu.VMEM((n, D_MODEL), F32)],
        compiler_params=pltpu.CompilerParams(
            dimension_semantics=("arbitrary",), vmem_limit_bytes=VMEM_LIMIT),
        name=name,
    )(x1, o, g_ffn, g_final, wo, wup, wdown)


def kernel(x_prompt, x_sample, state_pool, state_conv, cache_mem_k, cache_mem_v, mem_prompt,
           g_mix, g_attn, g_mem, g_ffn, g_final, w_pool, pool_scale,
           w_conv_in, w_conv, w_conv_out, w_q, w_kv, w_o, w_up, w_down):
    batch, seq, _ = x_prompt.shape
    n_sample = x_sample.shape[0]
    assert DEPTH == 2 and x_sample.shape[1] == 1
    assert (batch * N_MEM) % KV_ROW_TILE == 0

    row = lambda a: a.reshape(1, D_MODEL)
    cast = lambda a: a.astype(BF16)
    wpool_b, win_b, wout_b = cast(w_pool[0]), cast(w_conv_in[0]), cast(w_conv_out[0])
    wq_b = [cast(w_q[i]) for i in range(DEPTH)]
    wo_b = [cast(w_o[i]) for i in range(DEPTH)]
    wup_b = [cast(w_up[i]) for i in range(DEPTH)]
    wdown_b = [cast(w_down[i]) for i in range(DEPTH)]
    scale0, wconv0, gfin = row(pool_scale[0]), w_conv[0], row(g_final)

    k_all, v_all, kb_all, vb_all = _mem_kv(mem_prompt.reshape(batch * N_MEM, D_MODEL), g_mem, w_kv)
    kb_all = kb_all.reshape(DEPTH, batch, N_MEM, D_MODEL)
    vb_all = vb_all.reshape(DEPTH, batch, N_MEM, D_MODEL)
    xs = x_sample.reshape(n_sample, D_MODEL)
    ck, cv = _cache_rows(cache_mem_k), _cache_rows(cache_mem_v)

    xs1, q0, pool_s = _sample_pre(
        _sample_pool_pre_kernel, "sample_pool_pre", xs,
        state_pool[0].transpose(1, 0, 2),
        [row(g_mix[0]), row(g_attn[0])], [wpool_b, scale0, wq_b[0]])
    x1, pool_p, o0 = _prompt_layer(
        _pool_layer_kernel, "prompt_pool_layer", 0, POOL_ROW_TILE, POOL_VMEM_LIMIT, True,
        x_prompt, kb_all, vb_all, _head_rows(q0), ck, cv,
        [row(g_mix[0]), row(g_attn[0]), row(g_ffn[0])],
        [wpool_b, scale0, wq_b[0], wo_b[0], wup_b[0], wdown_b[0]], POOL_STATE, POOL_HALO)
    xs2 = _sample_post(xs1, _unhead_rows(o0), row(g_ffn[0]), gfin, wo_b[0], wup_b[0], wdown_b[0],
                       False, "sample_post0")

    xs3, q1, conv_s = _sample_pre(
        _sample_conv_pre_kernel, "sample_conv_pre", xs2,
        state_conv[0].reshape(n_sample, CONV_STATE * D_MODEL),
        [row(g_mix[1]), row(g_attn[1])], [win_b, wconv0, wout_b, wq_b[1]])
    y_prompt, conv_p, o1 = _prompt_layer(
        _conv_layer_kernel, "prompt_conv_layer", 1, CONV_ROW_TILE, CONV_VMEM_LIMIT, False,
        x1, kb_all, vb_all, _head_rows(q1), ck, cv,
        [row(g_mix[1]), row(g_attn[1]), row(g_ffn[1]), gfin],
        [win_b, wconv0, wout_b, wq_b[1], wo_b[1], wup_b[1], wdown_b[1]], CONV_STATE, CONV_HALO)
    y_sample = _sample_post(xs3, _unhead_rows(o1), row(g_ffn[1]), gfin, wo_b[1], wup_b[1], wdown_b[1],
                            True, "sample_post1")

    mem_shape = (DEPTH, batch, N_MEM, MEM_HEADS, MEM_HEAD_DIM)
    return (y_prompt,
            y_sample.reshape(n_sample, 1, D_MODEL),
            pool_p.reshape(1, batch, POOL_STATE, D_MODEL),
            conv_p.reshape(1, batch, CONV_STATE, D_MODEL),
            _unhead_rows(k_all.reshape(-1, HEAD_ROWS, LANES)).reshape(mem_shape),
            _unhead_rows(v_all.reshape(-1, HEAD_ROWS, LANES)).reshape(mem_shape),
            pool_s.transpose(1, 0, 2)[None],
            conv_s.reshape(1, n_sample, CONV_STATE, D_MODEL))
```

```python
import functools

import jax
import jax.numpy as jnp
from jax import lax
from jax.experimental import pallas as pl
from jax.experimental.pallas import tpu as pltpu

D_MODEL = 1024
DEPTH = 2
PAST_LEN = 16384
POOL_WINDOWS = (2, 4, 8, 16)
POOL_GROUP = D_MODEL // len(POOL_WINDOWS)
POOL_STATE = max(POOL_WINDOWS) - 1
CONV_WIDTH = 3
CONV_STATE = CONV_WIDTH - 1
D_FF = 4 * D_MODEL
N_MEM = 256
MEM_HEADS = 4
MEM_HEAD_DIM = D_MODEL // MEM_HEADS
EPS = 1e-6

SUBLANES = 8
LANES = 128
HEAD_ROWS = D_MODEL // LANES
POOL_HALO = 16
CONV_HALO = SUBLANES
POOL_ROW_TILE = 512
CONV_ROW_TILE = 512
SUB_TILES = 2
POOL_SUB_TILES = 1
KV_ROW_TILE = 512
FF_CHUNK = 1024
POOL_FF_CHUNK = 512
CONV_FF_CHUNK = 1024
POOL_SAMPLE_START = 4
POOL_CHUNK = 64
CONV_SAMPLE_START = 2
VMEM_LIMIT = 56 * 1024 * 1024
POOL_VMEM_LIMIT = 60 * 1024 * 1024
CONV_VMEM_LIMIT = 62 * 1024 * 1024

F32 = jnp.float32
BF16 = jnp.bfloat16

_dot = functools.partial(jnp.dot, preferred_element_type=F32)


def _rms(x, g):
    y = x * lax.rsqrt(jnp.mean(x * x, axis=-1, keepdims=True) + EPS)
    return y * g


def _vmem():
    return pl.BlockSpec(memory_space=pltpu.VMEM)


def _weave(*delayed):
    live = list(delayed)
    round_ = 0
    while live:
        for item in list(live):
            stages, first_round = item
            if round_ >= first_round:
                try:
                    next(stages)
                except StopIteration:
                    live.remove(item)
        round_ += 1


def _mlp(xs, g_ffn, wup_ref, wdown_ref, chunk=FF_CHUNK):
    us = [_rms(x, g_ffn).astype(BF16) for x in xs]
    yield
    accs = list(xs)
    for c in range(D_FF // chunk):
        sl = slice(c * chunk, (c + 1) * chunk)
        hs = [jnp.maximum(_dot(u, wup_ref[:, sl]), 0.0) for u in us]
        yield
        accs = [a + _dot((h * h).astype(BF16), wdown_ref[sl, :]) for a, h in zip(accs, hs)]
        yield
    return accs


def _prompt_attend(xs, g_attn, k_ref, v_ref, wq_ref, wo_ref):
    us = [_rms(x, g_attn).astype(BF16) for x in xs]
    yield
    qs = [_dot(u, wq_ref[...]) * (MEM_HEAD_DIM ** -0.5) for u in us]
    yield
    heads = [[] for _ in xs]
    for h in range(MEM_HEADS):
        sl = slice(h * MEM_HEAD_DIM, (h + 1) * MEM_HEAD_DIM)
        ss = [lax.dot_general(q[:, sl].astype(BF16), k_ref[:, sl],
                              (((1,), (1,)), ((), ())), preferred_element_type=F32) for q in qs]
        es = [jnp.exp(s - jnp.max(s, axis=-1, keepdims=True)) for s in ss]
        for i, e in enumerate(es):
            inv = 1.0 / jnp.sum(e, axis=-1, keepdims=True)
            heads[i].append((_dot(e.astype(BF16), v_ref[:, sl]) * inv).astype(BF16))
        yield
    os_ = [jnp.concatenate(hd, axis=-1) for hd in heads]
    return [x + _dot(o, wo_ref[...]) for x, o in zip(xs, os_)]


def _sample_attend(q_ref, k_ref, v_ref, o_ref):
    for j in range(q_ref.shape[0]):
        prod = k_ref[j] * q_ref[j][None]
        part = jnp.sum(prod, axis=-1, keepdims=True)
        s = part + pltpu.roll(part, MEM_HEADS, axis=1)
        e = jnp.exp(s - jnp.max(s, axis=0, keepdims=True))
        inv = 1.0 / jnp.sum(e, axis=0)
        yield
        o_ref[j] = jnp.sum(e * v_ref[j], axis=0) * inv
        yield


def _conv_tap(wconv_ref, back):
    k = CONV_WIDTH - 1 - back
    return wconv_ref[k:k + 1, :]


def _pool_project(pooled, wpool_ref, scale):
    ys = [_dot(p, wpool_ref[g]) for g, p in enumerate(pooled)]
    return jnp.concatenate(ys, axis=-1) * scale


def _carried(prev_ref, t):
    @pl.when(t == 0)
    def _():
        prev_ref[...] = jnp.zeros_like(prev_ref)

    return prev_ref[...]


def _to_head_rows(a):
    tiles = [a[:, (c % MEM_HEADS) * MEM_HEAD_DIM + (c // MEM_HEADS) * LANES:][:, :LANES]
             for c in range(HEAD_ROWS)]
    return jnp.concatenate(tiles, axis=-1).reshape(a.shape[0], HEAD_ROWS, LANES)


def _memkv_kernel(m_ref, g_ref, w_ref, k_ref, v_ref, kb_ref, vb_ref, wb_ref):
    @pl.when(pl.program_id(1) == 0)
    def _():
        wb_ref[...] = w_ref[...].astype(BF16)

    u = _rms(m_ref[...], g_ref[...]).astype(BF16)
    kv = _dot(u, wb_ref[...])
    k = kv[:, :D_MODEL]
    v = kv[:, D_MODEL:]
    k_ref[...] = _to_head_rows(k)
    v_ref[...] = _to_head_rows(v)
    kb_ref[...] = k.astype(BF16)
    vb_ref[...] = v.astype(BF16)


def _mem_kv(mem, g_mem, w_kv):
    rows = mem.shape[0]
    out = jax.ShapeDtypeStruct((DEPTH, rows, HEAD_ROWS, LANES), F32)
    outb = jax.ShapeDtypeStruct((DEPTH, rows, D_MODEL), BF16)
    hspec = pl.BlockSpec((None, KV_ROW_TILE, HEAD_ROWS, LANES), lambda l, r: (l, r, 0, 0))
    ospec = pl.BlockSpec((None, KV_ROW_TILE, D_MODEL), lambda l, r: (l, r, 0))
    return pl.pallas_call(
        _memkv_kernel,
        grid=(DEPTH, rows // KV_ROW_TILE),
        in_specs=[
            pl.BlockSpec((KV_ROW_TILE, D_MODEL), lambda l, r: (r, 0)),
            pl.BlockSpec((None, 1, D_MODEL), lambda l, r: (l, 0, 0)),
            pl.BlockSpec((None, D_MODEL, 2 * D_MODEL), lambda l, r: (l, 0, 0)),
        ],
        out_specs=[hspec, hspec, ospec, ospec],
        out_shape=[out, out, outb, outb],
        scratch_shapes=[pltpu.VMEM((D_MODEL, 2 * D_MODEL), BF16)],
        compiler_params=pltpu.CompilerParams(
            dimension_semantics=("arbitrary", "arbitrary"), vmem_limit_bytes=VMEM_LIMIT),
        name="mem_kv",
    )(mem, g_mem.reshape(DEPTH, 1, D_MODEL), w_kv)


def _pipelined_step(start, finish, carried, sample):
    step = pl.program_id(0)
    last = pl.num_programs(0) - 1

    def starts():
        halos = [carried]
        return [(start(i, halos), 0) for i in range(POOL_SUB_TILES)]

    finishes = lambda: [(finish(i), 0) for i in range(POOL_SUB_TILES)]

    @pl.when(step == 0)
    def _():
        _weave(*starts(), sample())

    @pl.when(jnp.logical_and(step > 0, step < last))
    def _():
        _weave(*finishes(), *starts(), sample())

    @pl.when(step == last)
    def _():
        _weave(*finishes())


def _started_tile(tiles_per_seq):
    step = jnp.minimum(pl.program_id(0), pl.num_programs(0) - 2)
    return lax.rem(step, tiles_per_seq)


def _pool_layer_kernel(tiles_per_seq, x_ref, gm_ref, ga_ref, gf_ref, k_ref, v_ref, sq_ref, sk_ref,
                       sv_ref, wpool_ref, scale_ref, wq_ref, wo_ref, wup_ref, wdown_ref,
                       y_ref, state_ref, so_ref, prev_ref, mid_ref):
    t = _started_tile(tiles_per_seq)
    tm = x_ref.shape[0]
    rows = tm // POOL_SUB_TILES
    carried = _carried(prev_ref, t)

    def start(i, halos):
        rs = slice(i * rows, (i + 1) * rows)
        x = x_ref[rs, :]
        u = _rms(x, gm_ref[...])
        halos.append(u[rows - POOL_HALO:])
        if i == POOL_SUB_TILES - 1:
            prev_ref[...] = u[rows - POOL_HALO:]
            state_ref[...] = u[rows - POOL_STATE:]
        yield
        ext = jnp.concatenate([halos[i], u], axis=0)
        x1 = []
        for lo in range(0, rows, POOL_CHUNK):
            hi = lo + POOL_CHUNK
            pos = t * tm + i * rows + lo + lax.broadcasted_iota(jnp.int32, (POOL_CHUNK, 1), 0)
            pooled = []
            for g, w in enumerate(POOL_WINDOWS):
                sl = slice(g * POOL_GROUP, (g + 1) * POOL_GROUP)
                s = ext[lo:hi + POOL_HALO, sl]
                shift = 1
                while shift < w:
                    s = s + pltpu.roll(s, shift, axis=0)
                    shift *= 2
                inv_cnt = 1.0 / jnp.minimum(pos + 1, w).astype(F32)
                pooled.append((s[POOL_HALO:] * inv_cnt - u[lo:hi, sl]).astype(BF16))
            x1.append(x[lo:hi] + _pool_project(pooled, wpool_ref, scale_ref[...]))
            yield
        x1 = jnp.concatenate(x1, axis=0)
        (x2,) = yield from _prompt_attend([x1], ga_ref[...], k_ref, v_ref, wq_ref, wo_ref)
        mid_ref[rs, :] = x2

    def finish(i):
        rs = slice(i * rows, (i + 1) * rows)
        x2 = mid_ref[rs, :]
        (y,) = yield from _mlp([x2], gf_ref[...], wup_ref, wdown_ref, POOL_FF_CHUNK)
        y_ref[rs, :] = y

    _pipelined_step(start, finish, carried,
                    lambda: (_sample_attend(sq_ref, sk_ref, sv_ref, so_ref), POOL_SAMPLE_START))


def _conv_layer_kernel(tiles_per_seq, x_ref, gm_ref, ga_ref, gf_ref, gfin_ref, k_ref, v_ref,
                       sq_ref, sk_ref, sv_ref, win_ref, wconv_ref, wout_ref, wq_ref, wo_ref,
                       wup_ref, wdown_ref, y_ref, state_ref, so_ref, prev_ref):
    t = lax.rem(pl.program_id(0), tiles_per_seq)
    rows = x_ref.shape[0] // SUB_TILES
    halos = [_carried(prev_ref, t)]

    def chain(i):
        rs = slice(i * rows, (i + 1) * rows)
        x = x_ref[rs, :]
        u = _rms(x, gm_ref[...]).astype(BF16)
        yield
        gate_b = _dot(u, win_ref[:, :D_MODEL])
        yield
        gate_c = _dot(u, win_ref[:, D_MODEL:2 * D_MODEL])
        yield
        ch = gate_c * _dot(u, win_ref[:, 2 * D_MODEL:])
        halos.append(ch[rows - CONV_HALO:])
        if i == SUB_TILES - 1:
            prev_ref[...] = ch[rows - CONV_HALO:]
            state_ref[...] = ch[rows - CONV_STATE:]
        yield
        ext = jnp.concatenate([halos[i], ch], axis=0)
        conv = ext * _conv_tap(wconv_ref, 0)
        for back in range(1, CONV_WIDTH):
            conv = conv + pltpu.roll(ext, back, axis=0) * _conv_tap(wconv_ref, back)
        conv = conv[CONV_HALO:]
        x1 = x + _dot((gate_b * conv).astype(BF16), wout_ref[...])
        yield
        x2 = yield from _prompt_attend([x1], ga_ref[...], k_ref, v_ref, wq_ref, wo_ref)
        (x3,) = yield from _mlp(x2, gf_ref[...], wup_ref, wdown_ref, CONV_FF_CHUNK)
        y_ref[rs, :] = _rms(x3, gfin_ref[...])

    _weave(*[(chain(i), 0) for i in range(SUB_TILES)],
           (_sample_attend(sq_ref, sk_ref, sv_ref, so_ref), CONV_SAMPLE_START))


def _prompt_layer(kernel, name, layer, row_tile, vmem_limit, pipelined, x, kb, vb, sq, sk, sv,
                  small, weights, state_rows, halo):
    batch, seq, _ = x.shape
    tiles_per_seq = seq // row_tile
    n_tiles = batch * tiles_per_seq
    n_sample = sq.shape[0]
    per_step = n_sample // n_tiles
    assert seq % row_tile == 0 and row_tile // SUB_TILES >= POOL_HALO
    assert per_step * n_tiles == n_sample

    started = (lambda i: jnp.minimum(i, n_tiles - 1)) if pipelined else (lambda i: i)
    finished = (lambda i: jnp.maximum(i - 1, 0)) if pipelined else (lambda i: i)
    seq_of = lambda j: lax.div(j, tiles_per_seq)
    tile_of = lambda j: lax.rem(j, tiles_per_seq)

    xspec = pl.BlockSpec((None, row_tile, D_MODEL),
                         lambda i: (seq_of(started(i)), tile_of(started(i)), 0))
    yspec = pl.BlockSpec((None, row_tile, D_MODEL),
                         lambda i: (seq_of(finished(i)), tile_of(finished(i)), 0))
    stspec = pl.BlockSpec((None, state_rows, D_MODEL), lambda i: (seq_of(started(i)), 0, 0))
    kvspec = pl.BlockSpec((None, None, N_MEM, D_MODEL),
                          lambda i: (layer, seq_of(started(i)), 0, 0))
    sqspec = pl.BlockSpec((per_step, HEAD_ROWS, LANES), lambda i: (started(i), 0, 0))
    skvspec = pl.BlockSpec((None, per_step, N_MEM, HEAD_ROWS, LANES),
                           lambda i: (layer, started(i), 0, 0, 0))
    scratch = [pltpu.VMEM((halo, D_MODEL), F32)]
    if pipelined:
        scratch.append(pltpu.VMEM((row_tile, D_MODEL), F32))
    return pl.pallas_call(
        functools.partial(kernel, tiles_per_seq),
        grid=(n_tiles + 1 if pipelined else n_tiles,),
        in_specs=([xspec] + [_vmem()] * len(small) + [kvspec, kvspec, sqspec, skvspec, skvspec]
                  + [_vmem()] * len(weights)),
        out_specs=[yspec, stspec, sqspec],
        out_shape=[jax.ShapeDtypeStruct(x.shape, F32),
                   jax.ShapeDtypeStruct((batch, state_rows, D_MODEL), F32),
                   jax.ShapeDtypeStruct(sq.shape, F32)],
        scratch_shapes=scratch,
        compiler_params=pltpu.CompilerParams(
            dimension_semantics=("arbitrary",), vmem_limit_bytes=vmem_limit),
        name=name,
    )(x, *small, kb, vb, sq, sk, sv, *weights)


def _sample_pool_pre_kernel(x_ref, st_ref, gm_ref, ga_ref, wpool_ref, scale_ref, wq_ref,
                            x1_ref, q_ref, newst_ref):
    x = x_ref[...]
    u = _rms(x, gm_ref[...])
    pooled = []
    for g, w in enumerate(POOL_WINDOWS):
        s = u[:, g * POOL_GROUP:(g + 1) * POOL_GROUP]
        for back in range(1, w):
            s = s + st_ref[POOL_STATE - back, :, g * POOL_GROUP:(g + 1) * POOL_GROUP]
        cnt = min(PAST_LEN + 1, w)
        pooled.append((s * (1.0 / cnt) - u[:, g * POOL_GROUP:(g + 1) * POOL_GROUP]).astype(BF16))
    x = x + _pool_project(pooled, wpool_ref, scale_ref[...])
    x1_ref[...] = x
    q_ref[...] = _dot(_rms(x, ga_ref[...]).astype(BF16), wq_ref[...]) * (MEM_HEAD_DIM ** -0.5)
    newst_ref[:POOL_STATE - 1] = st_ref[1:]
    newst_ref[POOL_STATE - 1] = u


def _sample_conv_pre_kernel(x_ref, st_ref, gm_ref, ga_ref, win_ref, wconv_ref, wout_ref, wq_ref,
                            x1_ref, q_ref, newst_ref):
    x = x_ref[...]
    u = _rms(x, gm_ref[...]).astype(BF16)
    gate_b = _dot(u, win_ref[:, :D_MODEL])
    ch = _dot(u, win_ref[:, D_MODEL:2 * D_MODEL]) * _dot(u, win_ref[:, 2 * D_MODEL:])
    conv = ch * _conv_tap(wconv_ref, 0)
    for back in range(1, CONV_WIDTH):
        lo = (CONV_STATE - back) * D_MODEL
        conv = conv + st_ref[:, lo:lo + D_MODEL] * _conv_tap(wconv_ref, back)
    x = x + _dot((gate_b * conv).astype(BF16), wout_ref[...])
    x1_ref[...] = x
    q_ref[...] = _dot(_rms(x, ga_ref[...]).astype(BF16), wq_ref[...]) * (MEM_HEAD_DIM ** -0.5)
    keep = (CONV_STATE - 1) * D_MODEL
    newst_ref[:, :keep] = st_ref[:, D_MODEL:]
    newst_ref[:, keep:] = ch


def _sample_pre(kernel, name, x, state, small, weights):
    n = x.shape[0]
    act = jax.ShapeDtypeStruct((n, D_MODEL), F32)
    return pl.pallas_call(
        kernel,
        out_shape=[act, act, jax.ShapeDtypeStruct(state.shape, F32)],
        compiler_params=pltpu.CompilerParams(vmem_limit_bytes=VMEM_LIMIT),
        name=name,
    )(x, state, *small, *weights)


def _head_rows(a):
    n = a.shape[0]
    a = a.reshape(n, MEM_HEADS, MEM_HEAD_DIM // LANES, LANES)
    return a.transpose(0, 2, 1, 3).reshape(n, HEAD_ROWS, LANES)


def _unhead_rows(a):
    n = a.shape[0]
    a = a.reshape(n, MEM_HEAD_DIM // LANES, MEM_HEADS, LANES)
    return a.transpose(0, 2, 1, 3).reshape(n, D_MODEL)


def _cache_rows(c):
    depth, n = c.shape[:2]
    c = c.reshape(depth, n, N_MEM, MEM_HEADS, MEM_HEAD_DIM // LANES, LANES)
    return c.transpose(0, 1, 2, 4, 3, 5).reshape(depth, n, N_MEM, HEAD_ROWS, LANES)


def _sample_post_kernel(final, x_ref, o_ref, gf_ref, gfin_ref, wo_ref, wup_ref, wdown_ref, y_ref,
                        u_ref, acc_ref):
    c = pl.program_id(0)

    @pl.when(c == 0)
    def _():
        x = x_ref[...] + _dot(o_ref[...].astype(BF16), wo_ref[...])
        u_ref[...] = _rms(x, gf_ref[...]).astype(BF16)
        acc_ref[...] = x

    h = jnp.maximum(_dot(u_ref[...], wup_ref[...]), 0.0)
    acc_ref[...] += _dot((h * h).astype(BF16), wdown_ref[...])

    @pl.when(c == pl.num_programs(0) - 1)
    def _():
        x = acc_ref[...]
        y_ref[...] = _rms(x, gfin_ref[...]) if final else x


def _sample_post(x1, o, g_ffn, g_final, wo, wup, wdown, final, name):
    n = x1.shape[0]
    whole = lambda shape: pl.BlockSpec(shape, lambda c: (0,) * len(shape))
    act = whole((n, D_MODEL))
    vec = whole((1, D_MODEL))
    return pl.pallas_call(
        functools.partial(_sample_post_kernel, final),
        grid=(D_FF // FF_CHUNK,),
        in_specs=[act, act, vec, vec, whole((D_MODEL, D_MODEL)),
                  pl.BlockSpec((D_MODEL, FF_CHUNK), lambda c: (0, c)),
                  pl.BlockSpec((FF_CHUNK, D_MODEL), lambda c: (c, 0))],
        out_specs=act,
        out_shape=jax.ShapeDtypeStruct(x1.shape, F32),
        scratch_shapes=[pltpu.VMEM((n, D_MODEL), BF16), pltpu.VMEM((n, D_MODEL), F32)],
        compiler_params=pltpu.CompilerParams(
            dimension_semantics=("arbitrary",), vmem_limit_bytes=VMEM_LIMIT),
        name=name,
    )(x1, o, g_ffn, g_final, wo, wup, wdown)


def kernel(x_prompt, x_sample, state_pool, state_conv, cache_mem_k, cache_mem_v, mem_prompt,
           g_mix, g_attn, g_mem, g_ffn, g_final, w_pool, pool_scale,
           w_conv_in, w_conv, w_conv_out, w_q, w_kv, w_o, w_up, w_down):
    batch, seq, _ = x_prompt.shape
    n_sample = x_sample.shape[0]
    assert DEPTH == 2 and x_sample.shape[1] == 1
    assert (batch * N_MEM) % KV_ROW_TILE == 0

    row = lambda a: a.reshape(1, D_MODEL)
    cast = lambda a: a.astype(BF16)
    wpool_b, win_b, wout_b = cast(w_pool[0]), cast(w_conv_in[0]), cast(w_conv_out[0])
    wq_b = [cast(w_q[i]) for i in range(DEPTH)]
    wo_b = [cast(w_o[i]) for i in range(DEPTH)]
    wup_b = [cast(w_up[i]) for i in range(DEPTH)]
    wdown_b = [cast(w_down[i]) for i in range(DEPTH)]
    scale0, wconv0, gfin = row(pool_scale[0]), w_conv[0], row(g_final)

    k_all, v_all, kb_all, vb_all = _mem_kv(mem_prompt.reshape(batch * N_MEM, D_MODEL), g_mem, w_kv)
    kb_all = kb_all.reshape(DEPTH, batch, N_MEM, D_MODEL)
    vb_all = vb_all.reshape(DEPTH, batch, N_MEM, D_MODEL)
    xs = x_sample.reshape(n_sample, D_MODEL)
    ck, cv = _cache_rows(cache_mem_k), _cache_rows(cache_mem_v)

    xs1, q0, pool_s = _sample_pre(
        _sample_pool_pre_kernel, "sample_pool_pre", xs,
        state_pool[0].transpose(1, 0, 2),
        [row(g_mix[0]), row(g_attn[0])], [wpool_b, scale0, wq_b[0]])
    x1, pool_p, o0 = _prompt_layer(
        _pool_layer_kernel, "prompt_pool_layer", 0, POOL_ROW_TILE, POOL_VMEM_LIMIT, True,
        x_prompt, kb_all, vb_all, _head_rows(q0), ck, cv,
        [row(g_mix[0]), row(g_attn[0]), row(g_ffn[0])],
        [wpool_b, scale0, wq_b[0], wo_b[0], wup_b[0], wdown_b[0]], POOL_STATE, POOL_HALO)
    xs2 = _sample_post(xs1, _unhead_rows(o0), row(g_ffn[0]), gfin, wo_b[0], wup_b[0], wdown_b[0],
                       False, "sample_post0")

    xs3, q1, conv_s = _sample_pre(
        _sample_conv_pre_kernel, "sample_conv_pre", xs2,
        state_conv[0].reshape(n_sample, CONV_STATE * D_MODEL),
        [row(g_mix[1]), row(g_attn[1])], [win_b, wconv0, wout_b, wq_b[1]])
    y_prompt, conv_p, o1 = _prompt_layer(
        _conv_layer_kernel, "prompt_conv_layer", 1, CONV_ROW_TILE, CONV_VMEM_LIMIT, False,
        x1, kb_all, vb_all, _head_rows(q1), ck, cv,
        [row(g_mix[1]), row(g_attn[1]), row(g_ffn[1]), gfin],
        [win_b, wconv0, wout_b, wq_b[1], wo_b[1], wup_b[1], wdown_b[1]], CONV_STATE, CONV_HALO)
    y_sample = _sample_post(xs3, _unhead_rows(o1), row(g_ffn[1]), gfin, wo_b[1], wup_b[1], wdown_b[1],
                            True, "sample_post1")

    mem_shape = (DEPTH, batch, N_MEM, MEM_HEADS, MEM_HEAD_DIM)
    return (y_prompt,
            y_sample.reshape(n_sample, 1, D_MODEL),
            pool_p.reshape(1, batch, POOL_STATE, D_MODEL),
            conv_p.reshape(1, batch, CONV_STATE, D_MODEL),
            _unhead_rows(k_all.reshape(-1, HEAD_ROWS, LANES)).reshape(mem_shape),
            _unhead_rows(v_all.reshape(-1, HEAD_ROWS, LANES)).reshape(mem_shape),
            pool_s.transpose(1, 0, 2)[None],
            conv_s.reshape(1, n_sample, CONV_STATE, D_MODEL))
```

```python
import functools

import jax
import jax.numpy as jnp
from jax import lax
from jax.experimental import pallas as pl
from jax.experimental.pallas import tpu as pltpu

D_MODEL = 1024
DEPTH = 2
PAST_LEN = 16384
POOL_WINDOWS = (2, 4, 8, 16)
POOL_GROUP = D_MODEL // len(POOL_WINDOWS)
POOL_STATE = max(POOL_WINDOWS) - 1
CONV_WIDTH = 3
CONV_STATE = CONV_WIDTH - 1
D_FF = 4 * D_MODEL
N_MEM = 256
MEM_HEADS = 4
MEM_HEAD_DIM = D_MODEL // MEM_HEADS
EPS = 1e-6

SUBLANES = 8
LANES = 128
HEAD_ROWS = D_MODEL // LANES
POOL_HALO = 16
CONV_HALO = SUBLANES
POOL_ROW_TILE = 512
CONV_ROW_TILE = 512
SUB_TILES = 2
POOL_SUB_TILES = 1
KV_ROW_TILE = 512
FF_CHUNK = 1024
POOL_FF_CHUNK = 512
CONV_FF_CHUNK = 1024
POOL_SAMPLE_START = 4
POOL_CHUNK = 64
CONV_SAMPLE_START = 2
VMEM_LIMIT = 56 * 1024 * 1024
POOL_VMEM_LIMIT = 60 * 1024 * 1024
CONV_VMEM_LIMIT = 62 * 1024 * 1024

F32 = jnp.float32
BF16 = jnp.bfloat16

_dot = functools.partial(jnp.dot, preferred_element_type=F32)


def _rms(x, g):
    y = x * lax.rsqrt(jnp.mean(x * x, axis=-1, keepdims=True) + EPS)
    return y * g


def _vmem():
    return pl.BlockSpec(memory_space=pltpu.VMEM)


def _of_layer(stacked, layer):
    zeros = (0,) * (stacked.ndim - 1)
    return pl.BlockSpec((None, *stacked.shape[1:]), lambda i: (layer, *zeros),
                        pipeline_mode=pl.Buffered(1))


def _gain(g_ref, layer):
    return g_ref[layer:layer + 1, :]


def _weave(*delayed):
    live = list(delayed)
    round_ = 0
    while live:
        for item in list(live):
            stages, first_round = item
            if round_ >= first_round:
                try:
                    next(stages)
                except StopIteration:
                    live.remove(item)
        round_ += 1


def _mlp(xs, g_ffn, wup_ref, wdown_ref, chunk=FF_CHUNK):
    us = [_rms(x, g_ffn).astype(BF16) for x in xs]
    yield
    accs = list(xs)
    for c in range(D_FF // chunk):
        sl = slice(c * chunk, (c + 1) * chunk)
        hs = [jnp.maximum(_dot(u, wup_ref[:, sl]), 0.0) for u in us]
        yield
        accs = [a + _dot((h * h).astype(BF16), wdown_ref[sl, :]) for a, h in zip(accs, hs)]
        yield
    return accs


def _prompt_attend(xs, g_attn, k_ref, v_ref, wq_ref, wo_ref):
    us = [_rms(x, g_attn).astype(BF16) for x in xs]
    yield
    qs = [_dot(u, wq_ref[...]) * (MEM_HEAD_DIM ** -0.5) for u in us]
    yield
    heads = [[] for _ in xs]
    for h in range(MEM_HEADS):
        sl = slice(h * MEM_HEAD_DIM, (h + 1) * MEM_HEAD_DIM)
        ss = [lax.dot_general(q[:, sl].astype(BF16), k_ref[:, sl],
                              (((1,), (1,)), ((), ())), preferred_element_type=F32) for q in qs]
        es = [jnp.exp(s - jnp.max(s, axis=-1, keepdims=True)) for s in ss]
        for i, e in enumerate(es):
            inv = 1.0 / jnp.sum(e, axis=-1, keepdims=True)
            heads[i].append((_dot(e.astype(BF16), v_ref[:, sl]) * inv).astype(BF16))
        yield
    os_ = [jnp.concatenate(hd, axis=-1) for hd in heads]
    return [x + _dot(o, wo_ref[...]) for x, o in zip(xs, os_)]


def _sample_attend(q_ref, k_ref, v_ref, o_ref):
    for j in range(q_ref.shape[0]):
        prod = k_ref[j] * q_ref[j][None]
        part = jnp.sum(prod, axis=-1, keepdims=True)
        s = part + pltpu.roll(part, MEM_HEADS, axis=1)
        e = jnp.exp(s - jnp.max(s, axis=0, keepdims=True))
        inv = 1.0 / jnp.sum(e, axis=0)
        yield
        o_ref[j] = jnp.sum(e * v_ref[j], axis=0) * inv
        yield


def _conv_tap(wconv_ref, back):
    k = CONV_WIDTH - 1 - back
    return wconv_ref[k:k + 1, :]


def _pool_project(pooled, wpool_ref, scale):
    ys = [_dot(p, wpool_ref[g]) for g, p in enumerate(pooled)]
    return jnp.concatenate(ys, axis=-1) * scale


def _carried(prev_ref, t):
    @pl.when(t == 0)
    def _():
        prev_ref[...] = jnp.zeros_like(prev_ref)

    return prev_ref[...]


def _to_head_rows(a):
    tiles = [a[:, (c % MEM_HEADS) * MEM_HEAD_DIM + (c // MEM_HEADS) * LANES:][:, :LANES]
             for c in range(HEAD_ROWS)]
    return jnp.concatenate(tiles, axis=-1).reshape(a.shape[0], HEAD_ROWS, LANES)


def _from_head_rows(a):
    flat = a.reshape(a.shape[0], D_MODEL)
    per_head = MEM_HEAD_DIM // LANES
    tiles = [flat[:, ((c % per_head) * MEM_HEADS + c // per_head) * LANES:][:, :LANES]
             for c in range(HEAD_ROWS)]
    return jnp.concatenate(tiles, axis=-1)


def _unhead_rows(a):
    n = a.shape[0]
    a = a.reshape(n, MEM_HEAD_DIM // LANES, MEM_HEADS, LANES)
    return a.transpose(0, 2, 1, 3).reshape(n, D_MODEL)


def _cache_rows(c):
    depth, n = c.shape[:2]
    c = c.reshape(depth, n, N_MEM, MEM_HEADS, MEM_HEAD_DIM // LANES, LANES)
    return c.transpose(0, 1, 2, 4, 3, 5).reshape(depth, n, N_MEM, HEAD_ROWS, LANES)


def _memkv_kernel(m_ref, g_ref, w_ref, k_ref, v_ref, kb_ref, vb_ref, wb_ref):
    @pl.when(pl.program_id(1) == 0)
    def _():
        wb_ref[...] = w_ref[...].astype(BF16)

    u = _rms(m_ref[...], g_ref[...]).astype(BF16)
    kv = _dot(u, wb_ref[...])
    k = kv[:, :D_MODEL]
    v = kv[:, D_MODEL:]
    k_ref[...] = _to_head_rows(k)
    v_ref[...] = _to_head_rows(v)
    kb_ref[...] = k.astype(BF16)
    vb_ref[...] = v.astype(BF16)


def _mem_kv(mem, g_mem, w_kv):
    rows = mem.shape[0]
    out = jax.ShapeDtypeStruct((DEPTH, rows, HEAD_ROWS, LANES), F32)
    outb = jax.ShapeDtypeStruct((DEPTH, rows, D_MODEL), BF16)
    hspec = pl.BlockSpec((None, KV_ROW_TILE, HEAD_ROWS, LANES), lambda l, r: (l, r, 0, 0))
    ospec = pl.BlockSpec((None, KV_ROW_TILE, D_MODEL), lambda l, r: (l, r, 0))
    return pl.pallas_call(
        _memkv_kernel,
        grid=(DEPTH, rows // KV_ROW_TILE),
        in_specs=[
            pl.BlockSpec((KV_ROW_TILE, D_MODEL), lambda l, r: (r, 0)),
            pl.BlockSpec((None, 1, D_MODEL), lambda l, r: (l, 0, 0)),
            pl.BlockSpec((None, D_MODEL, 2 * D_MODEL), lambda l, r: (l, 0, 0)),
        ],
        out_specs=[hspec, hspec, ospec, ospec],
        out_shape=[out, out, outb, outb],
        scratch_shapes=[pltpu.VMEM((D_MODEL, 2 * D_MODEL), BF16)],
        compiler_params=pltpu.CompilerParams(
            dimension_semantics=("arbitrary", "arbitrary"), vmem_limit_bytes=VMEM_LIMIT),
        name="mem_kv",
    )(mem, g_mem.reshape(DEPTH, 1, D_MODEL), w_kv)


def _pipelined_step(start, finish, carried, sample):
    step = pl.program_id(0)
    last = pl.num_programs(0) - 1

    def starts():
        halos = [carried]
        return [(start(i, halos), 0) for i in range(POOL_SUB_TILES)]

    finishes = lambda: [(finish(i), 0) for i in range(POOL_SUB_TILES)]

    @pl.when(step == 0)
    def _():
        _weave(*starts(), sample())

    @pl.when(jnp.logical_and(step > 0, step < last))
    def _():
        _weave(*finishes(), *starts(), sample())

    @pl.when(step == last)
    def _():
        _weave(*finishes())


def _started_tile(tiles_per_seq):
    step = jnp.minimum(pl.program_id(0), pl.num_programs(0) - 2)
    return lax.rem(step, tiles_per_seq)


def _pool_layer_kernel(tiles_per_seq, layer, x_ref, gm_ref, ga_ref, gf_ref, k_ref, v_ref, sq_ref,
                       sk_ref, sv_ref, wpool_ref, scale_ref, wq_ref, wo_ref, wup_ref, wdown_ref,
                       y_ref, state_ref, so_ref, prev_ref, mid_ref):
    t = _started_tile(tiles_per_seq)
    tm = x_ref.shape[0]
    rows = tm // POOL_SUB_TILES
    carried = _carried(prev_ref, t)

    def start(i, halos):
        rs = slice(i * rows, (i + 1) * rows)
        x = x_ref[rs, :]
        u = _rms(x, _gain(gm_ref, layer))
        halos.append(u[rows - POOL_HALO:])
        if i == POOL_SUB_TILES - 1:
            prev_ref[...] = u[rows - POOL_HALO:]
            state_ref[...] = u[rows - POOL_STATE:]
        yield
        ext = jnp.concatenate([halos[i], u], axis=0)
        x1 = []
        for lo in range(0, rows, POOL_CHUNK):
            hi = lo + POOL_CHUNK
            pos = t * tm + i * rows + lo + lax.broadcasted_iota(jnp.int32, (POOL_CHUNK, 1), 0)
            pooled = []
            for g, w in enumerate(POOL_WINDOWS):
                sl = slice(g * POOL_GROUP, (g + 1) * POOL_GROUP)
                s = ext[lo:hi + POOL_HALO, sl]
                shift = 1
                while shift < w:
                    s = s + pltpu.roll(s, shift, axis=0)
                    shift *= 2
                inv_cnt = 1.0 / jnp.minimum(pos + 1, w).astype(F32)
                pooled.append((s[POOL_HALO:] * inv_cnt - u[lo:hi, sl]).astype(BF16))
            x1.append(x[lo:hi] + _pool_project(pooled, wpool_ref, scale_ref[...]))
            yield
        x1 = jnp.concatenate(x1, axis=0)
        (x2,) = yield from _prompt_attend([x1], _gain(ga_ref, layer), k_ref, v_ref, wq_ref, wo_ref)
        mid_ref[rs, :] = x2

    def finish(i):
        rs = slice(i * rows, (i + 1) * rows)
        x2 = mid_ref[rs, :]
        (y,) = yield from _mlp([x2], _gain(gf_ref, layer), wup_ref, wdown_ref, POOL_FF_CHUNK)
        y_ref[rs, :] = y

    _pipelined_step(start, finish, carried,
                    lambda: (_sample_attend(sq_ref, sk_ref, sv_ref, so_ref), POOL_SAMPLE_START))


def _conv_layer_kernel(tiles_per_seq, layer, x_ref, gm_ref, ga_ref, gf_ref, gfin_ref, k_ref, v_ref,
                       sq_ref, sk_ref, sv_ref, win_ref, wconv_ref, wout_ref, wq_ref, wo_ref,
                       wup_ref, wdown_ref, y_ref, state_ref, so_ref, prev_ref):
    t = lax.rem(pl.program_id(0), tiles_per_seq)
    rows = x_ref.shape[0] // SUB_TILES
    halos = [_carried(prev_ref, t)]

    def chain(i):
        rs = slice(i * rows, (i + 1) * rows)
        x = x_ref[rs, :]
        u = _rms(x, _gain(gm_ref, layer)).astype(BF16)
        yield
        gate_b = _dot(u, win_ref[:, :D_MODEL])
        yield
        gate_c = _dot(u, win_ref[:, D_MODEL:2 * D_MODEL])
        yield
        ch = gate_c * _dot(u, win_ref[:, 2 * D_MODEL:])
        halos.append(ch[rows - CONV_HALO:])
        if i == SUB_TILES - 1:
            prev_ref[...] = ch[rows - CONV_HALO:]
            state_ref[...] = ch[rows - CONV_STATE:]
        yield
        ext = jnp.concatenate([halos[i], ch], axis=0)
        conv = ext * _conv_tap(wconv_ref, 0)
        for back in range(1, CONV_WIDTH):
            conv = conv + pltpu.roll(ext, back, axis=0) * _conv_tap(wconv_ref, back)
        conv = conv[CONV_HALO:]
        x1 = x + _dot((gate_b * conv).astype(BF16), wout_ref[...])
        yield
        x2 = yield from _prompt_attend([x1], _gain(ga_ref, layer), k_ref, v_ref, wq_ref, wo_ref)
        (x3,) = yield from _mlp(x2, _gain(gf_ref, layer), wup_ref, wdown_ref, CONV_FF_CHUNK)
        y_ref[rs, :] = _rms(x3, gfin_ref[...])

    _weave(*[(chain(i), 0) for i in range(SUB_TILES)],
           (_sample_attend(sq_ref, sk_ref, sv_ref, so_ref), CONV_SAMPLE_START))


def _prompt_layer(kernel, name, layer, row_tile, vmem_limit, pipelined, x, kb, vb, sq, sk, sv,
                  small, weights, stacked, state_rows, halo):
    batch, seq, _ = x.shape
    tiles_per_seq = seq // row_tile
    n_tiles = batch * tiles_per_seq
    n_sample = sq.shape[0]
    per_step = n_sample // n_tiles
    assert seq % row_tile == 0 and row_tile // SUB_TILES >= POOL_HALO
    assert per_step * n_tiles == n_sample

    started = (lambda i: jnp.minimum(i, n_tiles - 1)) if pipelined else (lambda i: i)
    finished = (lambda i: jnp.maximum(i - 1, 0)) if pipelined else (lambda i: i)
    seq_of = lambda j: lax.div(j, tiles_per_seq)
    tile_of = lambda j: lax.rem(j, tiles_per_seq)

    xspec = pl.BlockSpec((None, row_tile, D_MODEL),
                         lambda i: (seq_of(started(i)), tile_of(started(i)), 0))
    yspec = pl.BlockSpec((None, row_tile, D_MODEL),
                         lambda i: (seq_of(finished(i)), tile_of(finished(i)), 0))
    stspec = pl.BlockSpec((None, state_rows, D_MODEL), lambda i: (seq_of(started(i)), 0, 0))
    kvspec = pl.BlockSpec((None, None, N_MEM, D_MODEL),
                          lambda i: (layer, seq_of(started(i)), 0, 0))
    sqspec = pl.BlockSpec((per_step, HEAD_ROWS, LANES), lambda i: (started(i), 0, 0))
    skvspec = pl.BlockSpec((None, per_step, N_MEM, HEAD_ROWS, LANES),
                           lambda i: (layer, started(i), 0, 0, 0))
    scratch = [pltpu.VMEM((halo, D_MODEL), F32)]
    if pipelined:
        scratch.append(pltpu.VMEM((row_tile, D_MODEL), F32))
    return pl.pallas_call(
        functools.partial(kernel, tiles_per_seq, layer),
        grid=(n_tiles + 1 if pipelined else n_tiles,),
        in_specs=([xspec] + [_vmem()] * len(small) + [kvspec, kvspec, sqspec, skvspec, skvspec]
                  + [_vmem()] * len(weights) + [_of_layer(w, layer) for w in stacked]),
        out_specs=[yspec, stspec, sqspec],
        out_shape=[jax.ShapeDtypeStruct(x.shape, F32),
                   jax.ShapeDtypeStruct((batch, state_rows, D_MODEL), F32),
                   jax.ShapeDtypeStruct(sq.shape, F32)],
        scratch_shapes=scratch,
        compiler_params=pltpu.CompilerParams(
            dimension_semantics=("arbitrary",), vmem_limit_bytes=vmem_limit),
        name=name,
    )(x, *small, kb, vb, sq, sk, sv, *weights, *stacked)


def _sample_query(x, g_attn, wq_ref):
    q = _dot(_rms(x, g_attn).astype(BF16), wq_ref[...]) * (MEM_HEAD_DIM ** -0.5)
    return _to_head_rows(q)


def _sample_pool_pre_kernel(layer, x_ref, st_ref, gm_ref, ga_ref, wpool_ref, scale_ref, wq_ref,
                            x1_ref, q_ref, newst_ref):
    x = x_ref[...]
    u = _rms(x, _gain(gm_ref, layer))
    pooled = []
    for g, w in enumerate(POOL_WINDOWS):
        s = u[:, g * POOL_GROUP:(g + 1) * POOL_GROUP]
        for back in range(1, w):
            s = s + st_ref[POOL_STATE - back, :, g * POOL_GROUP:(g + 1) * POOL_GROUP]
        cnt = min(PAST_LEN + 1, w)
        pooled.append((s * (1.0 / cnt) - u[:, g * POOL_GROUP:(g + 1) * POOL_GROUP]).astype(BF16))
    x = x + _pool_project(pooled, wpool_ref, scale_ref[...])
    x1_ref[...] = x
    q_ref[...] = _sample_query(x, _gain(ga_ref, layer), wq_ref)
    newst_ref[:POOL_STATE - 1] = st_ref[1:]
    newst_ref[POOL_STATE - 1] = u


def _sample_conv_pre_kernel(layer, x_ref, st_ref, gm_ref, ga_ref, win_ref, wconv_ref, wout_ref,
                            wq_ref, x1_ref, q_ref, newst_ref):
    x = x_ref[...]
    u = _rms(x, _gain(gm_ref, layer)).astype(BF16)
    gate_b = _dot(u, win_ref[:, :D_MODEL])
    ch = _dot(u, win_ref[:, D_MODEL:2 * D_MODEL]) * _dot(u, win_ref[:, 2 * D_MODEL:])
    conv = ch * _conv_tap(wconv_ref, 0)
    for back in range(1, CONV_WIDTH):
        lo = (CONV_STATE - back) * D_MODEL
        conv = conv + st_ref[:, lo:lo + D_MODEL] * _conv_tap(wconv_ref, back)
    x = x + _dot((gate_b * conv).astype(BF16), wout_ref[...])
    x1_ref[...] = x
    q_ref[...] = _sample_query(x, _gain(ga_ref, layer), wq_ref)
    keep = (CONV_STATE - 1) * D_MODEL
    newst_ref[:, :keep] = st_ref[:, D_MODEL:]
    newst_ref[:, keep:] = ch


def _sample_pre(kernel, name, layer, x, state, small, weights, w_q):
    n = x.shape[0]
    return pl.pallas_call(
        functools.partial(kernel, layer),
        grid=(1,),
        in_specs=[_vmem()] * (2 + len(small) + len(weights)) + [_of_layer(w_q, layer)],
        out_specs=[_vmem()] * 3,
        out_shape=[jax.ShapeDtypeStruct((n, D_MODEL), F32),
                   jax.ShapeDtypeStruct((n, HEAD_ROWS, LANES), F32),
                   jax.ShapeDtypeStruct(state.shape, F32)],
        compiler_params=pltpu.CompilerParams(
            dimension_semantics=("arbitrary",), vmem_limit_bytes=VMEM_LIMIT),
        name=name,
    )(x, state, *small, *weights, w_q)


def _sample_post_kernel(final, layer, x_ref, o_ref, gf_ref, gfin_ref, wo_ref, wup_ref, wdown_ref,
                        y_ref, u_ref, acc_ref):
    c = pl.program_id(0)

    @pl.when(c == 0)
    def _():
        x = x_ref[...] + _dot(_from_head_rows(o_ref[...]).astype(BF16), wo_ref[...])
        u_ref[...] = _rms(x, _gain(gf_ref, layer)).astype(BF16)
        acc_ref[...] = x

    h = jnp.maximum(_dot(u_ref[...], wup_ref[...]), 0.0)
    acc_ref[...] += _dot((h * h).astype(BF16), wdown_ref[...])

    @pl.when(c == pl.num_programs(0) - 1)
    def _():
        x = acc_ref[...]
        y_ref[...] = _rms(x, gfin_ref[...]) if final else x


def _sample_post(x1, o, g_ffn, g_final, wo, wup, wdown, layer, final, name):
    n = x1.shape[0]
    whole = lambda shape: pl.BlockSpec(shape, lambda c: (0,) * len(shape))
    act = whole((n, D_MODEL))
    return pl.pallas_call(
        functools.partial(_sample_post_kernel, final, layer),
        grid=(D_FF // FF_CHUNK,),
        in_specs=[act, whole((n, HEAD_ROWS, LANES)), whole(g_ffn.shape), whole(g_final.shape),
                  pl.BlockSpec((None, D_MODEL, D_MODEL), lambda c: (layer, 0, 0)),
                  pl.BlockSpec((None, D_MODEL, FF_CHUNK), lambda c: (layer, 0, c)),
                  pl.BlockSpec((None, FF_CHUNK, D_MODEL), lambda c: (layer, c, 0))],
        out_specs=act,
        out_shape=jax.ShapeDtypeStruct(x1.shape, F32),
        scratch_shapes=[pltpu.VMEM((n, D_MODEL), BF16), pltpu.VMEM((n, D_MODEL), F32)],
        compiler_params=pltpu.CompilerParams(
            dimension_semantics=("arbitrary",), vmem_limit_bytes=VMEM_LIMIT),
        name=name,
    )(x1, o, g_ffn, g_final, wo, wup, wdown)


def kernel(x_prompt, x_sample, state_pool, state_conv, cache_mem_k, cache_mem_v, mem_prompt,
           g_mix, g_attn, g_mem, g_ffn, g_final, w_pool, pool_scale,
           w_conv_in, w_conv, w_conv_out, w_q, w_kv, w_o, w_up, w_down):
    batch, seq, _ = x_prompt.shape
    n_sample = x_sample.shape[0]
    assert DEPTH == 2 and x_sample.shape[1] == 1
    assert (batch * N_MEM) % KV_ROW_TILE == 0

    cast = lambda a: a.astype(BF16)
    wpool_b, win_b, wout_b = cast(w_pool[0]), cast(w_conv_in[0]), cast(w_conv_out[0])
    wq_b, wo_b, wup_b, wdown_b = cast(w_q), cast(w_o), cast(w_up), cast(w_down)
    scale0, wconv0, gfin = pool_scale[0].reshape(1, D_MODEL), w_conv[0], g_final.reshape(1, D_MODEL)

    k_all, v_all, kb_all, vb_all = _mem_kv(mem_prompt.reshape(batch * N_MEM, D_MODEL), g_mem, w_kv)
    kb_all = kb_all.reshape(DEPTH, batch, N_MEM, D_MODEL)
    vb_all = vb_all.reshape(DEPTH, batch, N_MEM, D_MODEL)
    xs = x_sample.reshape(n_sample, D_MODEL)
    ck, cv = _cache_rows(cache_mem_k), _cache_rows(cache_mem_v)

    xs1, q0, pool_s = _sample_pre(
        _sample_pool_pre_kernel, "sample_pool_pre", 0, xs,
        state_pool[0].transpose(1, 0, 2),
        [g_mix, g_attn], [wpool_b, scale0], wq_b)
    x1, pool_p, o0 = _prompt_layer(
        _pool_layer_kernel, "prompt_pool_layer", 0, POOL_ROW_TILE, POOL_VMEM_LIMIT, True,
        x_prompt, kb_all, vb_all, q0, ck, cv, [g_mix, g_attn, g_ffn],
        [wpool_b, scale0], [wq_b, wo_b, wup_b, wdown_b], POOL_STATE, POOL_HALO)
    xs2 = _sample_post(xs1, o0, g_ffn, gfin, wo_b, wup_b, wdown_b, 0, False, "sample_post0")

    xs3, q1, conv_s = _sample_pre(
        _sample_conv_pre_kernel, "sample_conv_pre", 1, xs2,
        state_conv[0].reshape(n_sample, CONV_STATE * D_MODEL),
        [g_mix, g_attn], [win_b, wconv0, wout_b], wq_b)
    y_prompt, conv_p, o1 = _prompt_layer(
        _conv_layer_kernel, "prompt_conv_layer", 1, CONV_ROW_TILE, CONV_VMEM_LIMIT, False,
        x1, kb_all, vb_all, q1, ck, cv, [g_mix, g_attn, g_ffn, gfin],
        [win_b, wconv0, wout_b], [wq_b, wo_b, wup_b, wdown_b], CONV_STATE, CONV_HALO)
    y_sample = _sample_post(xs3, o1, g_ffn, gfin, wo_b, wup_b, wdown_b, 1, True, "sample_post1")

    mem_shape = (DEPTH, batch, N_MEM, MEM_HEADS, MEM_HEAD_DIM)
    return (y_prompt,
            y_sample.reshape(n_sample, 1, D_MODEL),
            pool_p.reshape(1, batch, POOL_STATE, D_MODEL),
            conv_p.reshape(1, batch, CONV_STATE, D_MODEL),
            _unhead_rows(k_all.reshape(-1, HEAD_ROWS, LANES)).reshape(mem_shape),
            _unhead_rows(v_all.reshape(-1, HEAD_ROWS, LANES)).reshape(mem_shape),
            pool_s.transpose(1, 0, 2)[None],
            conv_s.reshape(1, n_sample, CONV_STATE, D_MODEL))
```

```python
import functools

import jax
import jax.numpy as jnp
from jax import lax
from jax.experimental import pallas as pl
from jax.experimental.pallas import tpu as pltpu

D_MODEL = 1024
DEPTH = 2
PAST_LEN = 16384
POOL_WINDOWS = (2, 4, 8, 16)
POOL_GROUP = D_MODEL // len(POOL_WINDOWS)
POOL_STATE = max(POOL_WINDOWS) - 1
CONV_WIDTH = 3
CONV_STATE = CONV_WIDTH - 1
D_FF = 4 * D_MODEL
N_MEM = 256
MEM_HEADS = 4
MEM_HEAD_DIM = D_MODEL // MEM_HEADS
EPS = 1e-6

SUBLANES = 8
LANES = 128
HEAD_ROWS = D_MODEL // LANES
POOL_HALO = 16
CONV_HALO = SUBLANES
POOL_ROW_TILE = 512
CONV_ROW_TILE = 512
SUB_TILES = 2
POOL_SUB_TILES = 1
KV_ROW_TILE = 512
FF_CHUNK = 1024
POOL_FF_CHUNK = 512
CONV_FF_CHUNK = 1024
POOL_SAMPLE_START = 4
POOL_CHUNK = 64
CONV_SAMPLE_START = 2
VMEM_LIMIT = 56 * 1024 * 1024
POOL_VMEM_LIMIT = 62 * 1024 * 1024
CONV_VMEM_LIMIT = 62 * 1024 * 1024

F32 = jnp.float32
BF16 = jnp.bfloat16

_dot = functools.partial(jnp.dot, preferred_element_type=F32)


def _rms(x, g):
    y = x * lax.rsqrt(jnp.mean(x * x, axis=-1, keepdims=True) + EPS)
    return y * g


def _vmem():
    return pl.BlockSpec(memory_space=pltpu.VMEM)


def _of_layer(stacked, layer):
    zeros = (0,) * (stacked.ndim - 1)
    return pl.BlockSpec((None, *stacked.shape[1:]), lambda i: (layer, *zeros),
                        pipeline_mode=pl.Buffered(1))


def _gain(g_ref, layer):
    return g_ref[layer:layer + 1, :]


def _weave(*delayed):
    live = list(delayed)
    round_ = 0
    while live:
        for item in list(live):
            stages, first_round = item
            if round_ >= first_round:
                try:
                    next(stages)
                except StopIteration:
                    live.remove(item)
        round_ += 1


def _mlp(xs, g_ffn, wup_ref, wdown_ref, chunk=FF_CHUNK):
    us = [_rms(x, g_ffn).astype(BF16) for x in xs]
    yield
    accs = list(xs)
    for c in range(D_FF // chunk):
        sl = slice(c * chunk, (c + 1) * chunk)
        hs = [jnp.maximum(_dot(u, wup_ref[:, sl]), 0.0) for u in us]
        yield
        accs = [a + _dot((h * h).astype(BF16), wdown_ref[sl, :]) for a, h in zip(accs, hs)]
        yield
    return accs


def _prompt_attend(xs, g_attn, k_ref, v_ref, wq_ref, wo_ref):
    us = [_rms(x, g_attn).astype(BF16) for x in xs]
    yield
    qs = [_dot(u, wq_ref[...]) * (MEM_HEAD_DIM ** -0.5) for u in us]
    yield
    heads = [[] for _ in xs]
    for h in range(MEM_HEADS):
        sl = slice(h * MEM_HEAD_DIM, (h + 1) * MEM_HEAD_DIM)
        ss = [lax.dot_general(q[:, sl].astype(BF16), k_ref[:, sl],
                              (((1,), (1,)), ((), ())), preferred_element_type=F32) for q in qs]
        es = [jnp.exp(s - jnp.max(s, axis=-1, keepdims=True)) for s in ss]
        for i, e in enumerate(es):
            inv = 1.0 / jnp.sum(e, axis=-1, keepdims=True)
            heads[i].append((_dot(e.astype(BF16), v_ref[:, sl]) * inv).astype(BF16))
        yield
    os_ = [jnp.concatenate(hd, axis=-1) for hd in heads]
    return [x + _dot(o, wo_ref[...]) for x, o in zip(xs, os_)]


def _sample_attend(q_ref, k_ref, v_ref, o_ref):
    for j in range(q_ref.shape[0]):
        prod = k_ref[j] * q_ref[j][None]
        part = jnp.sum(prod, axis=-1, keepdims=True)
        s = part + pltpu.roll(part, MEM_HEADS, axis=1)
        e = jnp.exp(s - jnp.max(s, axis=0, keepdims=True))
        inv = 1.0 / jnp.sum(e, axis=0)
        yield
        o_ref[j] = jnp.sum(e * v_ref[j], axis=0) * inv
        yield


def _conv_tap(wconv_ref, back):
    k = CONV_WIDTH - 1 - back
    return wconv_ref[k:k + 1, :]


def _pool_project(pooled, wpool_ref, scale):
    ys = [_dot(p, wpool_ref[g]) for g, p in enumerate(pooled)]
    return jnp.concatenate(ys, axis=-1) * scale


def _carried(prev_ref, t):
    @pl.when(t == 0)
    def _():
        prev_ref[...] = jnp.zeros_like(prev_ref)

    return prev_ref[...]


def _to_head_rows(a):
    tiles = [a[:, (c % MEM_HEADS) * MEM_HEAD_DIM + (c // MEM_HEADS) * LANES:][:, :LANES]
             for c in range(HEAD_ROWS)]
    return jnp.concatenate(tiles, axis=-1).reshape(a.shape[0], HEAD_ROWS, LANES)


def _from_head_rows(a):
    flat = a.reshape(a.shape[0], D_MODEL)
    per_head = MEM_HEAD_DIM // LANES
    tiles = [flat[:, ((c % per_head) * MEM_HEADS + c // per_head) * LANES:][:, :LANES]
             for c in range(HEAD_ROWS)]
    return jnp.concatenate(tiles, axis=-1)


def _unhead_rows(a):
    n = a.shape[0]
    a = a.reshape(n, MEM_HEAD_DIM // LANES, MEM_HEADS, LANES)
    return a.transpose(0, 2, 1, 3).reshape(n, D_MODEL)


def _cache_rows(c):
    depth, n = c.shape[:2]
    c = c.reshape(depth, n, N_MEM, MEM_HEADS, MEM_HEAD_DIM // LANES, LANES)
    return c.transpose(0, 1, 2, 4, 3, 5).reshape(depth, n, N_MEM, HEAD_ROWS, LANES)


def _memkv_kernel(m_ref, g_ref, w_ref, wup_ref, k_ref, v_ref, kb_ref, vb_ref, wupb_ref, wb_ref):
    @pl.when(pl.program_id(1) == 0)
    def _():
        wb_ref[...] = w_ref[...].astype(BF16)

    wupb_ref[...] = wup_ref[...].astype(BF16)

    u = _rms(m_ref[...], g_ref[...]).astype(BF16)
    kv = _dot(u, wb_ref[...])
    k = kv[:, :D_MODEL]
    v = kv[:, D_MODEL:]
    k_ref[...] = _to_head_rows(k)
    v_ref[...] = _to_head_rows(v)
    kb_ref[...] = k.astype(BF16)
    vb_ref[...] = v.astype(BF16)


def _mem_kv(mem, g_mem, w_kv, w_up):
    rows = mem.shape[0]
    row_tiles = rows // KV_ROW_TILE
    slab = D_MODEL // row_tiles
    assert slab * row_tiles == D_MODEL
    out = jax.ShapeDtypeStruct((DEPTH, rows, HEAD_ROWS, LANES), F32)
    outb = jax.ShapeDtypeStruct((DEPTH, rows, D_MODEL), BF16)
    hspec = pl.BlockSpec((None, KV_ROW_TILE, HEAD_ROWS, LANES), lambda l, r: (l, r, 0, 0))
    ospec = pl.BlockSpec((None, KV_ROW_TILE, D_MODEL), lambda l, r: (l, r, 0))
    wspec = pl.BlockSpec((None, slab, D_FF), lambda l, r: (l, r, 0))
    return pl.pallas_call(
        _memkv_kernel,
        grid=(DEPTH, row_tiles),
        in_specs=[
            pl.BlockSpec((KV_ROW_TILE, D_MODEL), lambda l, r: (r, 0)),
            pl.BlockSpec((None, 1, D_MODEL), lambda l, r: (l, 0, 0)),
            pl.BlockSpec((None, D_MODEL, 2 * D_MODEL), lambda l, r: (l, 0, 0)),
            wspec,
        ],
        out_specs=[hspec, hspec, ospec, ospec, wspec],
        out_shape=[out, out, outb, outb, jax.ShapeDtypeStruct(w_up.shape, BF16)],
        scratch_shapes=[pltpu.VMEM((D_MODEL, 2 * D_MODEL), BF16)],
        compiler_params=pltpu.CompilerParams(
            dimension_semantics=("arbitrary", "arbitrary"), vmem_limit_bytes=VMEM_LIMIT),
        name="mem_kv",
    )(mem, g_mem.reshape(DEPTH, 1, D_MODEL), w_kv, w_up)


def _pipelined_step(start, finish, carried, sample):
    step = pl.program_id(0)
    last = pl.num_programs(0) - 1

    def starts():
        halos = [carried]
        return [(start(i, halos), 0) for i in range(POOL_SUB_TILES)]

    finishes = lambda: [(finish(i), 0) for i in range(POOL_SUB_TILES)]

    @pl.when(step == 0)
    def _():
        _weave(*starts(), sample())

    @pl.when(jnp.logical_and(step > 0, step < last))
    def _():
        _weave(*finishes(), *starts(), sample())

    @pl.when(step == last)
    def _():
        _weave(*finishes())


def _started_tile(tiles_per_seq):
    step = jnp.minimum(pl.program_id(0), pl.num_programs(0) - 2)
    return lax.rem(step, tiles_per_seq)


def _pool_layer_kernel(tiles_per_seq, layer, x_ref, gm_ref, ga_ref, gf_ref, k_ref, v_ref, sq_ref,
                       sk_ref, sv_ref, wpool_ref, scale_ref, wq_ref, wo_ref, wup_ref, wdown_ref,
                       win_ref, wout_ref, y_ref, state_ref, so_ref, winb_ref, woutb_ref,
                       prev_ref, mid_ref):
    t = _started_tile(tiles_per_seq)
    tm = x_ref.shape[0]
    rows = tm // POOL_SUB_TILES
    carried = _carried(prev_ref, t)
    winb_ref[...] = win_ref[...].astype(BF16)
    woutb_ref[...] = wout_ref[...].astype(BF16)

    def start(i, halos):
        rs = slice(i * rows, (i + 1) * rows)
        x = x_ref[rs, :]
        u = _rms(x, _gain(gm_ref, layer))
        halos.append(u[rows - POOL_HALO:])
        if i == POOL_SUB_TILES - 1:
            prev_ref[...] = u[rows - POOL_HALO:]
            state_ref[...] = u[rows - POOL_STATE:]
        yield
        ext = jnp.concatenate([halos[i], u], axis=0)
        x1 = []
        for lo in range(0, rows, POOL_CHUNK):
            hi = lo + POOL_CHUNK
            pos = t * tm + i * rows + lo + lax.broadcasted_iota(jnp.int32, (POOL_CHUNK, 1), 0)
            pooled = []
            for g, w in enumerate(POOL_WINDOWS):
                sl = slice(g * POOL_GROUP, (g + 1) * POOL_GROUP)
                s = ext[lo:hi + POOL_HALO, sl]
                shift = 1
                while shift < w:
                    s = s + pltpu.roll(s, shift, axis=0)
                    shift *= 2
                inv_cnt = 1.0 / jnp.minimum(pos + 1, w).astype(F32)
                pooled.append((s[POOL_HALO:] * inv_cnt - u[lo:hi, sl]).astype(BF16))
            x1.append(x[lo:hi] + _pool_project(pooled, wpool_ref, scale_ref[...]))
            yield
        x1 = jnp.concatenate(x1, axis=0)
        (x2,) = yield from _prompt_attend([x1], _gain(ga_ref, layer), k_ref, v_ref, wq_ref, wo_ref)
        mid_ref[rs, :] = x2

    def finish(i):
        rs = slice(i * rows, (i + 1) * rows)
        x2 = mid_ref[rs, :]
        (y,) = yield from _mlp([x2], _gain(gf_ref, layer), wup_ref, wdown_ref, POOL_FF_CHUNK)
        y_ref[rs, :] = y

    _pipelined_step(start, finish, carried,
                    lambda: (_sample_attend(sq_ref, sk_ref, sv_ref, so_ref), POOL_SAMPLE_START))


def _conv_layer_kernel(tiles_per_seq, layer, x_ref, gm_ref, ga_ref, gf_ref, gfin_ref, k_ref, v_ref,
                       sq_ref, sk_ref, sv_ref, win_ref, wconv_ref, wout_ref, wq_ref, wo_ref,
                       wup_ref, wdown_ref, y_ref, state_ref, so_ref, prev_ref):
    t = lax.rem(pl.program_id(0), tiles_per_seq)
    rows = x_ref.shape[0] // SUB_TILES
    halos = [_carried(prev_ref, t)]

    def chain(i):
        rs = slice(i * rows, (i + 1) * rows)
        x = x_ref[rs, :]
        u = _rms(x, _gain(gm_ref, layer)).astype(BF16)
        yield
        gate_b = _dot(u, win_ref[:, :D_MODEL])
        yield
        gate_c = _dot(u, win_ref[:, D_MODEL:2 * D_MODEL])
        yield
        ch = gate_c * _dot(u, win_ref[:, 2 * D_MODEL:])
        halos.append(ch[rows - CONV_HALO:])
        if i == SUB_TILES - 1:
            prev_ref[...] = ch[rows - CONV_HALO:]
            state_ref[...] = ch[rows - CONV_STATE:]
        yield
        ext = jnp.concatenate([halos[i], ch], axis=0)
        conv = ext * _conv_tap(wconv_ref, 0)
        for back in range(1, CONV_WIDTH):
            conv = conv + pltpu.roll(ext, back, axis=0) * _conv_tap(wconv_ref, back)
        conv = conv[CONV_HALO:]
        x1 = x + _dot((gate_b * conv).astype(BF16), wout_ref[...])
        yield
        x2 = yield from _prompt_attend([x1], _gain(ga_ref, layer), k_ref, v_ref, wq_ref, wo_ref)
        (x3,) = yield from _mlp(x2, _gain(gf_ref, layer), wup_ref, wdown_ref, CONV_FF_CHUNK)
        y_ref[rs, :] = _rms(x3, gfin_ref[...])

    _weave(*[(chain(i), 0) for i in range(SUB_TILES)],
           (_sample_attend(sq_ref, sk_ref, sv_ref, so_ref), CONV_SAMPLE_START))


def _prompt_layer(kernel, name, layer, row_tile, vmem_limit, pipelined, x, kb, vb, sq, sk, sv,
                  small, weights, stacked, state_rows, halo, cast_along=()):
    batch, seq, _ = x.shape
    tiles_per_seq = seq // row_tile
    n_tiles = batch * tiles_per_seq
    n_sample = sq.shape[0]
    per_step = n_sample // n_tiles
    assert seq % row_tile == 0 and row_tile // SUB_TILES >= POOL_HALO
    assert per_step * n_tiles == n_sample

    started = (lambda i: jnp.minimum(i, n_tiles - 1)) if pipelined else (lambda i: i)
    finished = (lambda i: jnp.maximum(i - 1, 0)) if pipelined else (lambda i: i)
    seq_of = lambda j: lax.div(j, tiles_per_seq)
    tile_of = lambda j: lax.rem(j, tiles_per_seq)

    xspec = pl.BlockSpec((None, row_tile, D_MODEL),
                         lambda i: (seq_of(started(i)), tile_of(started(i)), 0))
    yspec = pl.BlockSpec((None, row_tile, D_MODEL),
                         lambda i: (seq_of(finished(i)), tile_of(finished(i)), 0))
    stspec = pl.BlockSpec((None, state_rows, D_MODEL), lambda i: (seq_of(started(i)), 0, 0))
    kvspec = pl.BlockSpec((None, None, N_MEM, D_MODEL),
                          lambda i: (layer, seq_of(started(i)), 0, 0))
    sqspec = pl.BlockSpec((per_step, HEAD_ROWS, LANES), lambda i: (started(i), 0, 0))
    skvspec = pl.BlockSpec((None, per_step, N_MEM, HEAD_ROWS, LANES),
                           lambda i: (layer, started(i), 0, 0, 0))
    slabs = [pl.BlockSpec((a.shape[0] // n_tiles, a.shape[1]), lambda i: (started(i), 0))
             for a in cast_along]
    scratch = [pltpu.VMEM((halo, D_MODEL), F32)]
    if pipelined:
        scratch.append(pltpu.VMEM((row_tile, D_MODEL), F32))
    return pl.pallas_call(
        functools.partial(kernel, tiles_per_seq, layer),
        grid=(n_tiles + 1 if pipelined else n_tiles,),
        in_specs=([xspec] + [_vmem()] * len(small) + [kvspec, kvspec, sqspec, skvspec, skvspec]
                  + [_vmem()] * len(weights) + [_of_layer(w, layer) for w in stacked] + slabs),
        out_specs=[yspec, stspec, sqspec] + slabs,
        out_shape=[jax.ShapeDtypeStruct(x.shape, F32),
                   jax.ShapeDtypeStruct((batch, state_rows, D_MODEL), F32),
                   jax.ShapeDtypeStruct(sq.shape, F32)]
                  + [jax.ShapeDtypeStruct(a.shape, BF16) for a in cast_along],
        scratch_shapes=scratch,
        compiler_params=pltpu.CompilerParams(
            dimension_semantics=("arbitrary",), vmem_limit_bytes=vmem_limit),
        name=name,
    )(x, *small, kb, vb, sq, sk, sv, *weights, *stacked, *cast_along)


def _sample_query(x, g_attn, wq_ref):
    q = _dot(_rms(x, g_attn).astype(BF16), wq_ref[...]) * (MEM_HEAD_DIM ** -0.5)
    return _to_head_rows(q)


def _sample_pool_pre_kernel(layer, x_ref, st_ref, gm_ref, ga_ref, wpool_ref, scale_ref, wq_ref,
                            x1_ref, q_ref, newst_ref):
    x = x_ref[...]
    u = _rms(x, _gain(gm_ref, layer))
    pooled = []
    for g, w in enumerate(POOL_WINDOWS):
        s = u[:, g * POOL_GROUP:(g + 1) * POOL_GROUP]
        for back in range(1, w):
            s = s + st_ref[POOL_STATE - back, :, g * POOL_GROUP:(g + 1) * POOL_GROUP]
        cnt = min(PAST_LEN + 1, w)
        pooled.append((s * (1.0 / cnt) - u[:, g * POOL_GROUP:(g + 1) * POOL_GROUP]).astype(BF16))
    x = x + _pool_project(pooled, wpool_ref, scale_ref[...])
    x1_ref[...] = x
    q_ref[...] = _sample_query(x, _gain(ga_ref, layer), wq_ref)
    newst_ref[:POOL_STATE - 1] = st_ref[1:]
    newst_ref[POOL_STATE - 1] = u


def _sample_conv_pre_kernel(layer, x_ref, st_ref, gm_ref, ga_ref, win_ref, wconv_ref, wout_ref,
                            wq_ref, x1_ref, q_ref, newst_ref):
    x = x_ref[...]
    u = _rms(x, _gain(gm_ref, layer)).astype(BF16)
    gate_b = _dot(u, win_ref[:, :D_MODEL])
    ch = _dot(u, win_ref[:, D_MODEL:2 * D_MODEL]) * _dot(u, win_ref[:, 2 * D_MODEL:])
    conv = ch * _conv_tap(wconv_ref, 0)
    for back in range(1, CONV_WIDTH):
        lo = (CONV_STATE - back) * D_MODEL
        conv = conv + st_ref[:, lo:lo + D_MODEL] * _conv_tap(wconv_ref, back)
    x = x + _dot((gate_b * conv).astype(BF16), wout_ref[...])
    x1_ref[...] = x
    q_ref[...] = _sample_query(x, _gain(ga_ref, layer), wq_ref)
    keep = (CONV_STATE - 1) * D_MODEL
    newst_ref[:, :keep] = st_ref[:, D_MODEL:]
    newst_ref[:, keep:] = ch


def _sample_pre(kernel, name, layer, x, state, small, weights, w_q):
    n = x.shape[0]
    return pl.pallas_call(
        functools.partial(kernel, layer),
        grid=(1,),
        in_specs=[_vmem()] * (2 + len(small) + len(weights)) + [_of_layer(w_q, layer)],
        out_specs=[_vmem()] * 3,
        out_shape=[jax.ShapeDtypeStruct((n, D_MODEL), F32),
                   jax.ShapeDtypeStruct((n, HEAD_ROWS, LANES), F32),
                   jax.ShapeDtypeStruct(state.shape, F32)],
        compiler_params=pltpu.CompilerParams(
            dimension_semantics=("arbitrary",), vmem_limit_bytes=VMEM_LIMIT),
        name=name,
    )(x, state, *small, *weights, w_q)


def _sample_post_kernel(final, layer, x_ref, o_ref, gf_ref, gfin_ref, wo_ref, wup_ref, wdown_ref,
                        y_ref, u_ref, acc_ref):
    c = pl.program_id(0)

    @pl.when(c == 0)
    def _():
        x = x_ref[...] + _dot(_from_head_rows(o_ref[...]).astype(BF16), wo_ref[...])
        u_ref[...] = _rms(x, _gain(gf_ref, layer)).astype(BF16)
        acc_ref[...] = x

    h = jnp.maximum(_dot(u_ref[...], wup_ref[...]), 0.0)
    acc_ref[...] += _dot((h * h).astype(BF16), wdown_ref[...])

    @pl.when(c == pl.num_programs(0) - 1)
    def _():
        x = acc_ref[...]
        y_ref[...] = _rms(x, gfin_ref[...]) if final else x


def _sample_post(x1, o, g_ffn, g_final, wo, wup, wdown, layer, final, name):
    n = x1.shape[0]
    whole = lambda shape: pl.BlockSpec(shape, lambda c: (0,) * len(shape))
    act = whole((n, D_MODEL))
    return pl.pallas_call(
        functools.partial(_sample_post_kernel, final, layer),
        grid=(D_FF // FF_CHUNK,),
        in_specs=[act, whole((n, HEAD_ROWS, LANES)), whole(g_ffn.shape), whole(g_final.shape),
                  pl.BlockSpec((None, D_MODEL, D_MODEL), lambda c: (layer, 0, 0)),
                  pl.BlockSpec((None, D_MODEL, FF_CHUNK), lambda c: (layer, 0, c)),
                  pl.BlockSpec((None, FF_CHUNK, D_MODEL), lambda c: (layer, c, 0))],
        out_specs=act,
        out_shape=jax.ShapeDtypeStruct(x1.shape, F32),
        scratch_shapes=[pltpu.VMEM((n, D_MODEL), BF16), pltpu.VMEM((n, D_MODEL), F32)],
        compiler_params=pltpu.CompilerParams(
            dimension_semantics=("arbitrary",), vmem_limit_bytes=VMEM_LIMIT),
        name=name,
    )(x1, o, g_ffn, g_final, wo, wup, wdown)


def kernel(x_prompt, x_sample, state_pool, state_conv, cache_mem_k, cache_mem_v, mem_prompt,
           g_mix, g_attn, g_mem, g_ffn, g_final, w_pool, pool_scale,
           w_conv_in, w_conv, w_conv_out, w_q, w_kv, w_o, w_up, w_down):
    batch, seq, _ = x_prompt.shape
    n_sample = x_sample.shape[0]
    assert DEPTH == 2 and x_sample.shape[1] == 1
    assert (batch * N_MEM) % KV_ROW_TILE == 0

    cast = lambda a: a.astype(BF16)
    wpool_b = cast(w_pool[0])
    wq_b, wo_b, wdown_b = cast(w_q), cast(w_o), cast(w_down)
    scale0, wconv0, gfin = pool_scale[0].reshape(1, D_MODEL), w_conv[0], g_final.reshape(1, D_MODEL)

    k_all, v_all, kb_all, vb_all, wup_b = _mem_kv(
        mem_prompt.reshape(batch * N_MEM, D_MODEL), g_mem, w_kv, w_up)
    kb_all = kb_all.reshape(DEPTH, batch, N_MEM, D_MODEL)
    vb_all = vb_all.reshape(DEPTH, batch, N_MEM, D_MODEL)
    xs = x_sample.reshape(n_sample, D_MODEL)
    ck, cv = _cache_rows(cache_mem_k), _cache_rows(cache_mem_v)

    xs1, q0, pool_s = _sample_pre(
        _sample_pool_pre_kernel, "sample_pool_pre", 0, xs,
        state_pool[0].transpose(1, 0, 2),
        [g_mix, g_attn], [wpool_b, scale0], wq_b)
    x1, pool_p, o0, win_b, wout_b = _prompt_layer(
        _pool_layer_kernel, "prompt_pool_layer", 0, POOL_ROW_TILE, POOL_VMEM_LIMIT, True,
        x_prompt, kb_all, vb_all, q0, ck, cv, [g_mix, g_attn, g_ffn],
        [wpool_b, scale0], [wq_b, wo_b, wup_b, wdown_b], POOL_STATE, POOL_HALO,
        cast_along=[w_conv_in[0], w_conv_out[0]])
    xs2 = _sample_post(xs1, o0, g_ffn, gfin, wo_b, wup_b, wdown_b, 0, False, "sample_post0")

    xs3, q1, conv_s = _sample_pre(
        _sample_conv_pre_kernel, "sample_conv_pre", 1, xs2,
        state_conv[0].reshape(n_sample, CONV_STATE * D_MODEL),
        [g_mix, g_attn], [win_b, wconv0, wout_b], wq_b)
    y_prompt, conv_p, o1 = _prompt_layer(
        _conv_layer_kernel, "prompt_conv_layer", 1, CONV_ROW_TILE, CONV_VMEM_LIMIT, False,
        x1, kb_all, vb_all, q1, ck, cv, [g_mix, g_attn, g_ffn, gfin],
        [win_b, wconv0, wout_b], [wq_b, wo_b, wup_b, wdown_b], CONV_STATE, CONV_HALO)
    y_sample = _sample_post(xs3, o1, g_ffn, gfin, wo_b, wup_b, wdown_b, 1, True, "sample_post1")

    mem_shape = (DEPTH, batch, N_MEM, MEM_HEADS, MEM_HEAD_DIM)
    return (y_prompt,
            y_sample.reshape(n_sample, 1, D_MODEL),
            pool_p.reshape(1, batch, POOL_STATE, D_MODEL),
            conv_p.reshape(1, batch, CONV_STATE, D_MODEL),
            _unhead_rows(k_all.reshape(-1, HEAD_ROWS, LANES)).reshape(mem_shape),
            _unhead_rows(v_all.reshape(-1, HEAD_ROWS, LANES)).reshape(mem_shape),
            pool_s.transpose(1, 0, 2)[None],
            conv_s.reshape(1, n_sample, CONV_STATE, D_MODEL))
```

```python
import functools

import jax
import jax.numpy as jnp
from jax import lax
from jax.experimental import pallas as pl
from jax.experimental.pallas import tpu as pltpu

D_MODEL = 1024
DEPTH = 2
PAST_LEN = 16384
POOL_WINDOWS = (2, 4, 8, 16)
POOL_GROUP = D_MODEL // len(POOL_WINDOWS)
POOL_STATE = max(POOL_WINDOWS) - 1
CONV_WIDTH = 3
CONV_STATE = CONV_WIDTH - 1
D_FF = 4 * D_MODEL
N_MEM = 256
MEM_HEADS = 4
MEM_HEAD_DIM = D_MODEL // MEM_HEADS
EPS = 1e-6

SUBLANES = 8
LANES = 128
HEAD_ROWS = D_MODEL // LANES
POOL_HALO = 16
CONV_HALO = SUBLANES
POOL_ROW_TILE = 512
CONV_ROW_TILE = 512
SUB_TILES = 2
POOL_SUB_TILES = 1
KV_ROW_TILE = 512
FF_CHUNK = 1024
POOL_FF_CHUNK = 512
CONV_FF_CHUNK = 1024
POOL_SAMPLE_START = 4
POOL_CHUNK = 64
CONV_SAMPLE_START = 2
VMEM_LIMIT = 56 * 1024 * 1024
POOL_VMEM_LIMIT = 64 * 1024 * 1024 - 512 * 1024
CONV_VMEM_LIMIT = 62 * 1024 * 1024

F32 = jnp.float32
BF16 = jnp.bfloat16

_dot = functools.partial(jnp.dot, preferred_element_type=F32)


def _rms(x, g):
    y = x * lax.rsqrt(jnp.mean(x * x, axis=-1, keepdims=True) + EPS)
    return y * g


def _vmem():
    return pl.BlockSpec(memory_space=pltpu.VMEM)


def _of_layer(w, layer):
    if w.ndim == 2:
        return pl.BlockSpec(w.shape, lambda i: (0, 0), pipeline_mode=pl.Buffered(1))
    return pl.BlockSpec((None, *w.shape[1:]), lambda i: (layer, 0, 0),
                        pipeline_mode=pl.Buffered(1))


def _gain(g_ref, layer):
    return g_ref[layer:layer + 1, :]


def _weave(*delayed):
    live = list(delayed)
    round_ = 0
    while live:
        for item in list(live):
            stages, first_round = item
            if round_ >= first_round:
                try:
                    next(stages)
                except StopIteration:
                    live.remove(item)
        round_ += 1


def _mlp(xs, g_ffn, wup_ref, wdown_ref, chunk=FF_CHUNK):
    us = [_rms(x, g_ffn).astype(BF16) for x in xs]
    yield
    accs = list(xs)
    for c in range(D_FF // chunk):
        sl = slice(c * chunk, (c + 1) * chunk)
        hs = [jnp.maximum(_dot(u, wup_ref[:, sl]), 0.0) for u in us]
        yield
        accs = [a + _dot((h * h).astype(BF16), wdown_ref[sl, :]) for a, h in zip(accs, hs)]
        yield
    return accs


def _prompt_attend(xs, g_attn, k_ref, v_ref, wq_ref, wo_ref):
    us = [_rms(x, g_attn).astype(BF16) for x in xs]
    yield
    qs = [_dot(u, wq_ref[...]) * (MEM_HEAD_DIM ** -0.5) for u in us]
    yield
    heads = [[] for _ in xs]
    for h in range(MEM_HEADS):
        sl = slice(h * MEM_HEAD_DIM, (h + 1) * MEM_HEAD_DIM)
        ss = [lax.dot_general(q[:, sl].astype(BF16), k_ref[:, sl],
                              (((1,), (1,)), ((), ())), preferred_element_type=F32) for q in qs]
        es = [jnp.exp(s - jnp.max(s, axis=-1, keepdims=True)) for s in ss]
        for i, e in enumerate(es):
            inv = 1.0 / jnp.sum(e, axis=-1, keepdims=True)
            heads[i].append((_dot(e.astype(BF16), v_ref[:, sl]) * inv).astype(BF16))
        yield
    os_ = [jnp.concatenate(hd, axis=-1) for hd in heads]
    return [x + _dot(o, wo_ref[...]) for x, o in zip(xs, os_)]


def _sample_attend(q_ref, k_ref, v_ref, o_ref):
    for j in range(q_ref.shape[0]):
        prod = k_ref[j] * q_ref[j][None]
        part = jnp.sum(prod, axis=-1, keepdims=True)
        s = part + pltpu.roll(part, MEM_HEADS, axis=1)
        e = jnp.exp(s - jnp.max(s, axis=0, keepdims=True))
        inv = 1.0 / jnp.sum(e, axis=0)
        yield
        o_ref[j] = jnp.sum(e * v_ref[j], axis=0) * inv
        yield


def _conv_tap(wconv_ref, back):
    k = CONV_WIDTH - 1 - back
    return wconv_ref[k:k + 1, :]


def _pool_project(pooled, wpool_ref, scale):
    ys = [_dot(p, wpool_ref[g]) for g, p in enumerate(pooled)]
    return jnp.concatenate(ys, axis=-1) * scale


def _carried(prev_ref, t):
    @pl.when(t == 0)
    def _():
        prev_ref[...] = jnp.zeros_like(prev_ref)

    return prev_ref[...]


def _to_head_rows(a):
    tiles = [a[:, (c % MEM_HEADS) * MEM_HEAD_DIM + (c // MEM_HEADS) * LANES:][:, :LANES]
             for c in range(HEAD_ROWS)]
    return jnp.concatenate(tiles, axis=-1).reshape(a.shape[0], HEAD_ROWS, LANES)


def _from_head_rows(a):
    flat = a.reshape(a.shape[0], D_MODEL)
    per_head = MEM_HEAD_DIM // LANES
    tiles = [flat[:, ((c % per_head) * MEM_HEADS + c // per_head) * LANES:][:, :LANES]
             for c in range(HEAD_ROWS)]
    return jnp.concatenate(tiles, axis=-1)


def _unhead_rows(a):
    n = a.shape[0]
    a = a.reshape(n, MEM_HEAD_DIM // LANES, MEM_HEADS, LANES)
    return a.transpose(0, 2, 1, 3).reshape(n, D_MODEL)


def _cache_rows(c):
    depth, n = c.shape[:2]
    c = c.reshape(depth, n, N_MEM, MEM_HEADS, MEM_HEAD_DIM // LANES, LANES)
    return c.transpose(0, 1, 2, 4, 3, 5).reshape(depth, n, N_MEM, HEAD_ROWS, LANES)


def _memkv_kernel(m_ref, g_ref, w_ref, wup_ref, k_ref, v_ref, kb_ref, vb_ref, wupb_ref, wb_ref):
    @pl.when(pl.program_id(1) == 0)
    def _():
        wb_ref[...] = w_ref[...].astype(BF16)

    wupb_ref[...] = wup_ref[...].astype(BF16)

    u = _rms(m_ref[...], g_ref[...]).astype(BF16)
    kv = _dot(u, wb_ref[...])
    k = kv[:, :D_MODEL]
    v = kv[:, D_MODEL:]
    k_ref[...] = _to_head_rows(k)
    v_ref[...] = _to_head_rows(v)
    kb_ref[...] = k.astype(BF16)
    vb_ref[...] = v.astype(BF16)


def _mem_kv(mem, g_mem, w_kv, w_up):
    rows = mem.shape[0]
    row_tiles = rows // KV_ROW_TILE
    slab = D_MODEL // row_tiles
    assert slab * row_tiles == D_MODEL
    out = jax.ShapeDtypeStruct((DEPTH, rows, HEAD_ROWS, LANES), F32)
    outb = jax.ShapeDtypeStruct((DEPTH, rows, D_MODEL), BF16)
    hspec = pl.BlockSpec((None, KV_ROW_TILE, HEAD_ROWS, LANES), lambda l, r: (l, r, 0, 0))
    ospec = pl.BlockSpec((None, KV_ROW_TILE, D_MODEL), lambda l, r: (l, r, 0))
    wspec = pl.BlockSpec((None, slab, D_FF), lambda l, r: (l, r, 0))
    return pl.pallas_call(
        _memkv_kernel,
        grid=(DEPTH, row_tiles),
        in_specs=[
            pl.BlockSpec((KV_ROW_TILE, D_MODEL), lambda l, r: (r, 0)),
            pl.BlockSpec((None, 1, D_MODEL), lambda l, r: (l, 0, 0)),
            pl.BlockSpec((None, D_MODEL, 2 * D_MODEL), lambda l, r: (l, 0, 0)),
            wspec,
        ],
        out_specs=[hspec, hspec, ospec, ospec, wspec],
        out_shape=[out, out, outb, outb, jax.ShapeDtypeStruct(w_up.shape, BF16)],
        scratch_shapes=[pltpu.VMEM((D_MODEL, 2 * D_MODEL), BF16)],
        compiler_params=pltpu.CompilerParams(
            dimension_semantics=("arbitrary", "arbitrary"), vmem_limit_bytes=VMEM_LIMIT),
        name="mem_kv",
    )(mem, g_mem.reshape(DEPTH, 1, D_MODEL), w_kv, w_up)


def _pipelined_step(start, finish, carried, sample):
    step = pl.program_id(0)
    last = pl.num_programs(0) - 1

    def starts():
        halos = [carried]
        return [(start(i, halos), 0) for i in range(POOL_SUB_TILES)]

    finishes = lambda: [(finish(i), 0) for i in range(POOL_SUB_TILES)]

    @pl.when(step == 0)
    def _():
        _weave(*starts(), sample())

    @pl.when(jnp.logical_and(step > 0, step < last))
    def _():
        _weave(*finishes(), *starts(), sample())

    @pl.when(step == last)
    def _():
        _weave(*finishes())


def _started_tile(tiles_per_seq):
    step = jnp.minimum(pl.program_id(0), pl.num_programs(0) - 2)
    return lax.rem(step, tiles_per_seq)


def _pool_layer_kernel(tiles_per_seq, layer, x_ref, gm_ref, ga_ref, gf_ref, k_ref, v_ref, sq_ref,
                       sk_ref, sv_ref, wpool_ref, scale_ref, wq_ref, wo_ref, wup_ref, wdown_ref,
                       win_ref, wout_ref, wnext_ref, y_ref, state_ref, so_ref, winb_ref, woutb_ref,
                       wnextb_ref, prev_ref, mid_ref):
    t = _started_tile(tiles_per_seq)
    tm = x_ref.shape[0]
    rows = tm // POOL_SUB_TILES
    carried = _carried(prev_ref, t)
    winb_ref[...] = win_ref[...].astype(BF16)
    woutb_ref[...] = wout_ref[...].astype(BF16)
    wnextb_ref[...] = wnext_ref[...].astype(BF16)

    def start(i, halos):
        rs = slice(i * rows, (i + 1) * rows)
        x = x_ref[rs, :]
        u = _rms(x, _gain(gm_ref, layer))
        halos.append(u[rows - POOL_HALO:])
        if i == POOL_SUB_TILES - 1:
            prev_ref[...] = u[rows - POOL_HALO:]
            state_ref[...] = u[rows - POOL_STATE:]
        yield
        ext = jnp.concatenate([halos[i], u], axis=0)
        x1 = []
        for lo in range(0, rows, POOL_CHUNK):
            hi = lo + POOL_CHUNK
            pos = t * tm + i * rows + lo + lax.broadcasted_iota(jnp.int32, (POOL_CHUNK, 1), 0)
            pooled = []
            for g, w in enumerate(POOL_WINDOWS):
                sl = slice(g * POOL_GROUP, (g + 1) * POOL_GROUP)
                s = ext[lo:hi + POOL_HALO, sl]
                shift = 1
                while shift < w:
                    s = s + pltpu.roll(s, shift, axis=0)
                    shift *= 2
                inv_cnt = 1.0 / jnp.minimum(pos + 1, w).astype(F32)
                pooled.append((s[POOL_HALO:] * inv_cnt - u[lo:hi, sl]).astype(BF16))
            x1.append(x[lo:hi] + _pool_project(pooled, wpool_ref, scale_ref[...]))
            yield
        x1 = jnp.concatenate(x1, axis=0)
        (x2,) = yield from _prompt_attend([x1], _gain(ga_ref, layer), k_ref, v_ref, wq_ref, wo_ref)
        mid_ref[rs, :] = x2

    def finish(i):
        rs = slice(i * rows, (i + 1) * rows)
        x2 = mid_ref[rs, :]
        (y,) = yield from _mlp([x2], _gain(gf_ref, layer), wup_ref, wdown_ref, POOL_FF_CHUNK)
        y_ref[rs, :] = y

    _pipelined_step(start, finish, carried,
                    lambda: (_sample_attend(sq_ref, sk_ref, sv_ref, so_ref), POOL_SAMPLE_START))


def _conv_layer_kernel(tiles_per_seq, layer, x_ref, gm_ref, ga_ref, gf_ref, gfin_ref, k_ref, v_ref,
                       sq_ref, sk_ref, sv_ref, win_ref, wconv_ref, wout_ref, wq_ref, wo_ref,
                       wup_ref, wdown_ref, y_ref, state_ref, so_ref, prev_ref):
    t = lax.rem(pl.program_id(0), tiles_per_seq)
    rows = x_ref.shape[0] // SUB_TILES
    halos = [_carried(prev_ref, t)]

    def chain(i):
        rs = slice(i * rows, (i + 1) * rows)
        x = x_ref[rs, :]
        u = _rms(x, _gain(gm_ref, layer)).astype(BF16)
        yield
        gate_b = _dot(u, win_ref[:, :D_MODEL])
        yield
        gate_c = _dot(u, win_ref[:, D_MODEL:2 * D_MODEL])
        yield
        ch = gate_c * _dot(u, win_ref[:, 2 * D_MODEL:])
        halos.append(ch[rows - CONV_HALO:])
        if i == SUB_TILES - 1:
            prev_ref[...] = ch[rows - CONV_HALO:]
            state_ref[...] = ch[rows - CONV_STATE:]
        yield
        ext = jnp.concatenate([halos[i], ch], axis=0)
        conv = ext * _conv_tap(wconv_ref, 0)
        for back in range(1, CONV_WIDTH):
            conv = conv + pltpu.roll(ext, back, axis=0) * _conv_tap(wconv_ref, back)
        conv = conv[CONV_HALO:]
        x1 = x + _dot((gate_b * conv).astype(BF16), wout_ref[...])
        yield
        x2 = yield from _prompt_attend([x1], _gain(ga_ref, layer), k_ref, v_ref, wq_ref, wo_ref)
        (x3,) = yield from _mlp(x2, _gain(gf_ref, layer), wup_ref, wdown_ref, CONV_FF_CHUNK)
        y_ref[rs, :] = _rms(x3, gfin_ref[...])

    _weave(*[(chain(i), 0) for i in range(SUB_TILES)],
           (_sample_attend(sq_ref, sk_ref, sv_ref, so_ref), CONV_SAMPLE_START))


def _prompt_layer(kernel, name, layer, row_tile, vmem_limit, pipelined, x, kb, vb, sq, sk, sv,
                  small, weights, stacked, state_rows, halo, cast_along=()):
    batch, seq, _ = x.shape
    tiles_per_seq = seq // row_tile
    n_tiles = batch * tiles_per_seq
    n_sample = sq.shape[0]
    per_step = n_sample // n_tiles
    assert seq % row_tile == 0 and row_tile // SUB_TILES >= POOL_HALO
    assert per_step * n_tiles == n_sample

    started = (lambda i: jnp.minimum(i, n_tiles - 1)) if pipelined else (lambda i: i)
    finished = (lambda i: jnp.maximum(i - 1, 0)) if pipelined else (lambda i: i)
    seq_of = lambda j: lax.div(j, tiles_per_seq)
    tile_of = lambda j: lax.rem(j, tiles_per_seq)

    xspec = pl.BlockSpec((None, row_tile, D_MODEL),
                         lambda i: (seq_of(started(i)), tile_of(started(i)), 0))
    yspec = pl.BlockSpec((None, row_tile, D_MODEL),
                         lambda i: (seq_of(finished(i)), tile_of(finished(i)), 0))
    stspec = pl.BlockSpec((None, state_rows, D_MODEL), lambda i: (seq_of(started(i)), 0, 0))
    kvspec = pl.BlockSpec((None, None, N_MEM, D_MODEL),
                          lambda i: (layer, seq_of(started(i)), 0, 0))
    sqspec = pl.BlockSpec((per_step, HEAD_ROWS, LANES), lambda i: (started(i), 0, 0))
    skvspec = pl.BlockSpec((None, per_step, N_MEM, HEAD_ROWS, LANES),
                           lambda i: (layer, started(i), 0, 0, 0))
    slabs = [pl.BlockSpec((a.shape[0] // n_tiles, a.shape[1]), lambda i: (started(i), 0))
             for a in cast_along]
    scratch = [pltpu.VMEM((halo, D_MODEL), F32)]
    if pipelined:
        scratch.append(pltpu.VMEM((row_tile, D_MODEL), F32))
    return pl.pallas_call(
        functools.partial(kernel, tiles_per_seq, layer),
        grid=(n_tiles + 1 if pipelined else n_tiles,),
        in_specs=([xspec] + [_vmem()] * len(small) + [kvspec, kvspec, sqspec, skvspec, skvspec]
                  + [_vmem()] * len(weights) + [_of_layer(w, layer) for w in stacked] + slabs),
        out_specs=[yspec, stspec, sqspec] + slabs,
        out_shape=[jax.ShapeDtypeStruct(x.shape, F32),
                   jax.ShapeDtypeStruct((batch, state_rows, D_MODEL), F32),
                   jax.ShapeDtypeStruct(sq.shape, F32)]
                  + [jax.ShapeDtypeStruct(a.shape, BF16) for a in cast_along],
        scratch_shapes=scratch,
        compiler_params=pltpu.CompilerParams(
            dimension_semantics=("arbitrary",), vmem_limit_bytes=vmem_limit),
        name=name,
    )(x, *small, kb, vb, sq, sk, sv, *weights, *stacked, *cast_along)


def _sample_query(x, g_attn, wq_ref):
    q = _dot(_rms(x, g_attn).astype(BF16), wq_ref[...]) * (MEM_HEAD_DIM ** -0.5)
    return _to_head_rows(q)


def _sample_pool_pre_kernel(layer, x_ref, st_ref, gm_ref, ga_ref, wpool_ref, scale_ref, wq_ref,
                            x1_ref, q_ref, newst_ref):
    x = x_ref[...]
    u = _rms(x, _gain(gm_ref, layer))
    pooled = []
    for g, w in enumerate(POOL_WINDOWS):
        s = u[:, g * POOL_GROUP:(g + 1) * POOL_GROUP]
        for back in range(1, w):
            s = s + st_ref[POOL_STATE - back, :, g * POOL_GROUP:(g + 1) * POOL_GROUP]
        cnt = min(PAST_LEN + 1, w)
        pooled.append((s * (1.0 / cnt) - u[:, g * POOL_GROUP:(g + 1) * POOL_GROUP]).astype(BF16))
    x = x + _pool_project(pooled, wpool_ref, scale_ref[...])
    x1_ref[...] = x
    q_ref[...] = _sample_query(x, _gain(ga_ref, layer), wq_ref)
    newst_ref[:POOL_STATE - 1] = st_ref[1:]
    newst_ref[POOL_STATE - 1] = u


def _sample_conv_pre_kernel(layer, x_ref, st_ref, gm_ref, ga_ref, win_ref, wconv_ref, wout_ref,
                            wq_ref, x1_ref, q_ref, newst_ref):
    x = x_ref[...]
    u = _rms(x, _gain(gm_ref, layer)).astype(BF16)
    gate_b = _dot(u, win_ref[:, :D_MODEL])
    ch = _dot(u, win_ref[:, D_MODEL:2 * D_MODEL]) * _dot(u, win_ref[:, 2 * D_MODEL:])
    conv = ch * _conv_tap(wconv_ref, 0)
    for back in range(1, CONV_WIDTH):
        lo = (CONV_STATE - back) * D_MODEL
        conv = conv + st_ref[:, lo:lo + D_MODEL] * _conv_tap(wconv_ref, back)
    x = x + _dot((gate_b * conv).astype(BF16), wout_ref[...])
    x1_ref[...] = x
    q_ref[...] = _sample_query(x, _gain(ga_ref, layer), wq_ref)
    keep = (CONV_STATE - 1) * D_MODEL
    newst_ref[:, :keep] = st_ref[:, D_MODEL:]
    newst_ref[:, keep:] = ch


def _sample_pre(kernel, name, layer, x, state, small, weights, w_q):
    n = x.shape[0]
    return pl.pallas_call(
        functools.partial(kernel, layer),
        grid=(1,),
        in_specs=[_vmem()] * (2 + len(small) + len(weights)) + [_of_layer(w_q, layer)],
        out_specs=[_vmem()] * 3,
        out_shape=[jax.ShapeDtypeStruct((n, D_MODEL), F32),
                   jax.ShapeDtypeStruct((n, HEAD_ROWS, LANES), F32),
                   jax.ShapeDtypeStruct(state.shape, F32)],
        compiler_params=pltpu.CompilerParams(
            dimension_semantics=("arbitrary",), vmem_limit_bytes=VMEM_LIMIT),
        name=name,
    )(x, state, *small, *weights, w_q)


def _sample_post_kernel(final, layer, x_ref, o_ref, gf_ref, gfin_ref, wo_ref, wup_ref, wdown_ref,
                        y_ref, u_ref, acc_ref):
    c = pl.program_id(0)

    @pl.when(c == 0)
    def _():
        x = x_ref[...] + _dot(_from_head_rows(o_ref[...]).astype(BF16), wo_ref[...])
        u_ref[...] = _rms(x, _gain(gf_ref, layer)).astype(BF16)
        acc_ref[...] = x

    h = jnp.maximum(_dot(u_ref[...], wup_ref[...]), 0.0)
    acc_ref[...] += _dot((h * h).astype(BF16), wdown_ref[...])

    @pl.when(c == pl.num_programs(0) - 1)
    def _():
        x = acc_ref[...]
        y_ref[...] = _rms(x, gfin_ref[...]) if final else x


def _sample_post(x1, o, g_ffn, g_final, wo, wup, wdown, layer, final, name):
    n = x1.shape[0]
    whole = lambda shape: pl.BlockSpec(shape, lambda c: (0,) * len(shape))
    act = whole((n, D_MODEL))
    return pl.pallas_call(
        functools.partial(_sample_post_kernel, final, layer),
        grid=(D_FF // FF_CHUNK,),
        in_specs=[act, whole((n, HEAD_ROWS, LANES)), whole(g_ffn.shape), whole(g_final.shape),
                  pl.BlockSpec((None, D_MODEL, D_MODEL), lambda c: (layer, 0, 0)),
                  pl.BlockSpec((None, D_MODEL, FF_CHUNK), lambda c: (layer, 0, c)),
                  pl.BlockSpec((FF_CHUNK, D_MODEL), lambda c: (c, 0))],
        out_specs=act,
        out_shape=jax.ShapeDtypeStruct(x1.shape, F32),
        scratch_shapes=[pltpu.VMEM((n, D_MODEL), BF16), pltpu.VMEM((n, D_MODEL), F32)],
        compiler_params=pltpu.CompilerParams(
            dimension_semantics=("arbitrary",), vmem_limit_bytes=VMEM_LIMIT),
        name=name,
    )(x1, o, g_ffn, g_final, wo, wup, wdown)


def kernel(x_prompt, x_sample, state_pool, state_conv, cache_mem_k, cache_mem_v, mem_prompt,
           g_mix, g_attn, g_mem, g_ffn, g_final, w_pool, pool_scale,
           w_conv_in, w_conv, w_conv_out, w_q, w_kv, w_o, w_up, w_down):
    batch, seq, _ = x_prompt.shape
    n_sample = x_sample.shape[0]
    assert DEPTH == 2 and x_sample.shape[1] == 1
    assert (batch * N_MEM) % KV_ROW_TILE == 0

    cast = lambda a: a.astype(BF16)
    wpool_b = cast(w_pool[0])
    wq_b, wo_b, wdown0_b = cast(w_q), cast(w_o), cast(w_down[0])
    scale0, wconv0, gfin = pool_scale[0].reshape(1, D_MODEL), w_conv[0], g_final.reshape(1, D_MODEL)

    k_all, v_all, kb_all, vb_all, wup_b = _mem_kv(
        mem_prompt.reshape(batch * N_MEM, D_MODEL), g_mem, w_kv, w_up)
    kb_all = kb_all.reshape(DEPTH, batch, N_MEM, D_MODEL)
    vb_all = vb_all.reshape(DEPTH, batch, N_MEM, D_MODEL)
    xs = x_sample.reshape(n_sample, D_MODEL)
    ck, cv = _cache_rows(cache_mem_k), _cache_rows(cache_mem_v)

    xs1, q0, pool_s = _sample_pre(
        _sample_pool_pre_kernel, "sample_pool_pre", 0, xs,
        state_pool[0].transpose(1, 0, 2),
        [g_mix, g_attn], [wpool_b, scale0], wq_b)
    x1, pool_p, o0, win_b, wout_b, wdown1_b = _prompt_layer(
        _pool_layer_kernel, "prompt_pool_layer", 0, POOL_ROW_TILE, POOL_VMEM_LIMIT, True,
        x_prompt, kb_all, vb_all, q0, ck, cv, [g_mix, g_attn, g_ffn],
        [wpool_b, scale0], [wq_b, wo_b, wup_b, wdown0_b], POOL_STATE, POOL_HALO,
        cast_along=[w_conv_in[0], w_conv_out[0], w_down[1]])
    xs2 = _sample_post(xs1, o0, g_ffn, gfin, wo_b, wup_b, wdown0_b, 0, False, "sample_post0")

    xs3, q1, conv_s = _sample_pre(
        _sample_conv_pre_kernel, "sample_conv_pre", 1, xs2,
        state_conv[0].reshape(n_sample, CONV_STATE * D_MODEL),
        [g_mix, g_attn], [win_b, wconv0, wout_b], wq_b)
    y_prompt, conv_p, o1 = _prompt_layer(
        _conv_layer_kernel, "prompt_conv_layer", 1, CONV_ROW_TILE, CONV_VMEM_LIMIT, False,
        x1, kb_all, vb_all, q1, ck, cv, [g_mix, g_attn, g_ffn, gfin],
        [win_b, wconv0, wout_b], [wq_b, wo_b, wup_b, wdown1_b], CONV_STATE, CONV_HALO)
    y_sample = _sample_post(xs3, o1, g_ffn, gfin, wo_b, wup_b, wdown1_b, 1, True, "sample_post1")

    mem_shape = (DEPTH, batch, N_MEM, MEM_HEADS, MEM_HEAD_DIM)
    return (y_prompt,
            y_sample.reshape(n_sample, 1, D_MODEL),
            pool_p.reshape(1, batch, POOL_STATE, D_MODEL),
            conv_p.reshape(1, batch, CONV_STATE, D_MODEL),
            _unhead_rows(k_all.reshape(-1, HEAD_ROWS, LANES)).reshape(mem_shape),
            _unhead_rows(v_all.reshape(-1, HEAD_ROWS, LANES)).reshape(mem_shape),
            pool_s.transpose(1, 0, 2)[None],
            conv_s.reshape(1, n_sample, CONV_STATE, D_MODEL))
```

```python
import functools

import jax
import jax.numpy as jnp
from jax import lax
from jax.experimental import pallas as pl
from jax.experimental.pallas import tpu as pltpu

D_MODEL = 1024
DEPTH = 2
PAST_LEN = 16384
POOL_WINDOWS = (2, 4, 8, 16)
POOL_GROUP = D_MODEL // len(POOL_WINDOWS)
POOL_STATE = max(POOL_WINDOWS) - 1
CONV_WIDTH = 3
CONV_STATE = CONV_WIDTH - 1
D_FF = 4 * D_MODEL
N_MEM = 256
MEM_HEADS = 4
MEM_HEAD_DIM = D_MODEL // MEM_HEADS
EPS = 1e-6

SUBLANES = 8
LANES = 128
HEAD_ROWS = D_MODEL // LANES
POOL_HALO = 16
CONV_HALO = SUBLANES
POOL_ROW_TILE = 512
CONV_ROW_TILE = 512
SUB_TILES = 2
POOL_SUB_TILES = 1
KV_ROW_TILE = 512
FF_CHUNK = 1024
POOL_FF_CHUNK = 512
CONV_FF_CHUNK = 1024
POOL_SAMPLE_START = 4
POOL_CHUNK = 64
CONV_SAMPLE_START = 2
VMEM_LIMIT = 56 * 1024 * 1024
POOL_VMEM_LIMIT = 64 * 1024 * 1024 - 512 * 1024
CONV_VMEM_LIMIT = 62 * 1024 * 1024

F32 = jnp.float32
BF16 = jnp.bfloat16

_dot = functools.partial(jnp.dot, preferred_element_type=F32)


def _rms(x, g):
    y = x * lax.rsqrt(jnp.mean(x * x, axis=-1, keepdims=True) + EPS)
    return y * g


def _vmem():
    return pl.BlockSpec(memory_space=pltpu.VMEM)


def _of_layer(w, layer):
    if w.ndim == 2:
        return pl.BlockSpec(w.shape, lambda i: (0, 0), pipeline_mode=pl.Buffered(1))
    return pl.BlockSpec((None, *w.shape[1:]), lambda i: (layer, 0, 0),
                        pipeline_mode=pl.Buffered(1))


def _gain(g_ref, layer):
    return g_ref[layer:layer + 1, :]


def _weave(*delayed):
    live = list(delayed)
    round_ = 0
    while live:
        for item in list(live):
            stages, first_round = item
            if round_ >= first_round:
                try:
                    next(stages)
                except StopIteration:
                    live.remove(item)
        round_ += 1


def _mlp(xs, g_ffn, wup_ref, wdown_ref, chunk=FF_CHUNK):
    us = [_rms(x, g_ffn).astype(BF16) for x in xs]
    yield
    accs = list(xs)
    for c in range(D_FF // chunk):
        sl = slice(c * chunk, (c + 1) * chunk)
        hs = [jnp.maximum(_dot(u, wup_ref[:, sl]), 0.0) for u in us]
        yield
        accs = [a + _dot((h * h).astype(BF16), wdown_ref[sl, :]) for a, h in zip(accs, hs)]
        yield
    return accs


def _prompt_attend(xs, g_attn, k_ref, v_ref, wq_ref, wo_ref):
    us = [_rms(x, g_attn).astype(BF16) for x in xs]
    yield
    qs = [_dot(u, wq_ref[...]) * (MEM_HEAD_DIM ** -0.5) for u in us]
    yield
    heads = [[] for _ in xs]
    for h in range(MEM_HEADS):
        sl = slice(h * MEM_HEAD_DIM, (h + 1) * MEM_HEAD_DIM)
        ss = [lax.dot_general(q[:, sl].astype(BF16), k_ref[:, sl],
                              (((1,), (1,)), ((), ())), preferred_element_type=F32) for q in qs]
        es = [jnp.exp(s - jnp.max(s, axis=-1, keepdims=True)) for s in ss]
        for i, e in enumerate(es):
            inv = 1.0 / jnp.sum(e, axis=-1, keepdims=True)
            heads[i].append((_dot(e.astype(BF16), v_ref[:, sl]) * inv).astype(BF16))
        yield
    os_ = [jnp.concatenate(hd, axis=-1) for hd in heads]
    return [x + _dot(o, wo_ref[...]) for x, o in zip(xs, os_)]


def _sample_attend(q_ref, k_ref, v_ref, o_ref):
    for j in range(q_ref.shape[0]):
        prod = k_ref[j] * q_ref[j][None]
        part = jnp.sum(prod, axis=-1, keepdims=True)
        s = part + pltpu.roll(part, MEM_HEADS, axis=1)
        e = jnp.exp(s - jnp.max(s, axis=0, keepdims=True))
        inv = 1.0 / jnp.sum(e, axis=0)
        yield
        o_ref[j] = jnp.sum(e * v_ref[j], axis=0) * inv
        yield


def _conv_tap(wconv_ref, back):
    k = CONV_WIDTH - 1 - back
    return wconv_ref[k:k + 1, :]


def _pool_project(pooled, wpool_ref, scale):
    ys = [_dot(p, wpool_ref[g]) for g, p in enumerate(pooled)]
    return jnp.concatenate(ys, axis=-1) * scale


def _carried(prev_ref, t):
    @pl.when(t == 0)
    def _():
        prev_ref[...] = jnp.zeros_like(prev_ref)

    return prev_ref[...]


def _to_head_rows(a):
    tiles = [a[:, (c % MEM_HEADS) * MEM_HEAD_DIM + (c // MEM_HEADS) * LANES:][:, :LANES]
             for c in range(HEAD_ROWS)]
    return jnp.concatenate(tiles, axis=-1).reshape(a.shape[0], HEAD_ROWS, LANES)


def _from_head_rows(a):
    flat = a.reshape(a.shape[0], D_MODEL)
    per_head = MEM_HEAD_DIM // LANES
    tiles = [flat[:, ((c % per_head) * MEM_HEADS + c // per_head) * LANES:][:, :LANES]
             for c in range(HEAD_ROWS)]
    return jnp.concatenate(tiles, axis=-1)


def _unhead_rows(a):
    n = a.shape[0]
    a = a.reshape(n, MEM_HEAD_DIM // LANES, MEM_HEADS, LANES)
    return a.transpose(0, 2, 1, 3).reshape(n, D_MODEL)


def _cache_rows(c):
    depth, n = c.shape[:2]
    c = c.reshape(depth, n, N_MEM, MEM_HEADS, MEM_HEAD_DIM // LANES, LANES)
    return c.transpose(0, 1, 2, 4, 3, 5).reshape(depth, n, N_MEM, HEAD_ROWS, LANES)


def _memkv_kernel(m_ref, g_ref, w_ref, wup_ref, k_ref, v_ref, kb_ref, vb_ref, wupb_ref, wb_ref):
    @pl.when(pl.program_id(1) == 0)
    def _():
        wb_ref[...] = w_ref[...].astype(BF16)

    wupb_ref[...] = wup_ref[...].astype(BF16)

    u = _rms(m_ref[...], g_ref[...]).astype(BF16)
    kv = _dot(u, wb_ref[...])
    k = kv[:, :D_MODEL]
    v = kv[:, D_MODEL:]
    k_ref[...] = _to_head_rows(k)
    v_ref[...] = _to_head_rows(v)
    kb_ref[...] = k.astype(BF16)
    vb_ref[...] = v.astype(BF16)


def _mem_kv(mem, g_mem, w_kv, w_up):
    rows = mem.shape[0]
    row_tiles = rows // KV_ROW_TILE
    slab = D_MODEL // row_tiles
    assert slab * row_tiles == D_MODEL
    out = jax.ShapeDtypeStruct((DEPTH, rows, HEAD_ROWS, LANES), F32)
    outb = jax.ShapeDtypeStruct((DEPTH, rows, D_MODEL), BF16)
    hspec = pl.BlockSpec((None, KV_ROW_TILE, HEAD_ROWS, LANES), lambda l, r: (l, r, 0, 0))
    ospec = pl.BlockSpec((None, KV_ROW_TILE, D_MODEL), lambda l, r: (l, r, 0))
    wspec = pl.BlockSpec((None, slab, D_FF), lambda l, r: (l, r, 0))
    return pl.pallas_call(
        _memkv_kernel,
        grid=(DEPTH, row_tiles),
        in_specs=[
            pl.BlockSpec((KV_ROW_TILE, D_MODEL), lambda l, r: (r, 0)),
            pl.BlockSpec((None, 1, D_MODEL), lambda l, r: (l, 0, 0)),
            pl.BlockSpec((None, D_MODEL, 2 * D_MODEL), lambda l, r: (l, 0, 0)),
            wspec,
        ],
        out_specs=[hspec, hspec, ospec, ospec, wspec],
        out_shape=[out, out, outb, outb, jax.ShapeDtypeStruct(w_up.shape, BF16)],
        scratch_shapes=[pltpu.VMEM((D_MODEL, 2 * D_MODEL), BF16)],
        compiler_params=pltpu.CompilerParams(
            dimension_semantics=("arbitrary", "arbitrary"), vmem_limit_bytes=VMEM_LIMIT),
        name="mem_kv",
    )(mem, g_mem.reshape(DEPTH, 1, D_MODEL), w_kv, w_up)


def _pipelined_step(start, finish, carried, sample):
    step = pl.program_id(0)
    last = pl.num_programs(0) - 1

    def starts():
        halos = [carried]
        return [(start(i, halos), 0) for i in range(POOL_SUB_TILES)]

    finishes = lambda: [(finish(i), 0) for i in range(POOL_SUB_TILES)]

    @pl.when(step == 0)
    def _():
        _weave(*starts(), sample())

    @pl.when(jnp.logical_and(step > 0, step < last))
    def _():
        _weave(*finishes(), *starts(), sample())

    @pl.when(step == last)
    def _():
        _weave(*finishes())


def _started_tile(tiles_per_seq):
    step = jnp.minimum(pl.program_id(0), pl.num_programs(0) - 2)
    return lax.rem(step, tiles_per_seq)


def _pool_layer_kernel(tiles_per_seq, layer, x_ref, gm_ref, ga_ref, gf_ref, k_ref, v_ref, sq_ref,
                       sk_ref, sv_ref, wpool_ref, scale_ref, wq_ref, wo_ref, wup_ref, wdown_ref,
                       win_ref, wout_ref, wnext_ref, y_ref, state_ref, so_ref, winb_ref, woutb_ref,
                       wnextb_ref, prev_ref, mid_ref):
    t = _started_tile(tiles_per_seq)
    tm = x_ref.shape[0]
    rows = tm // POOL_SUB_TILES
    carried = _carried(prev_ref, t)
    winb_ref[...] = win_ref[...].astype(BF16)
    woutb_ref[...] = wout_ref[...].astype(BF16)
    wnextb_ref[...] = wnext_ref[...].astype(BF16)

    def start(i, halos):
        rs = slice(i * rows, (i + 1) * rows)
        x = x_ref[rs, :]
        u = _rms(x, _gain(gm_ref, layer))
        halos.append(u[rows - POOL_HALO:])
        if i == POOL_SUB_TILES - 1:
            prev_ref[...] = u[rows - POOL_HALO:]
            state_ref[...] = u[rows - POOL_STATE:]
        yield
        ext = jnp.concatenate([halos[i], u], axis=0)
        x1 = []
        for lo in range(0, rows, POOL_CHUNK):
            hi = lo + POOL_CHUNK
            pos = t * tm + i * rows + lo + lax.broadcasted_iota(jnp.int32, (POOL_CHUNK, 1), 0)
            pooled = []
            for g, w in enumerate(POOL_WINDOWS):
                sl = slice(g * POOL_GROUP, (g + 1) * POOL_GROUP)
                s = ext[lo:hi + POOL_HALO, sl]
                shift = 1
                while shift < w:
                    s = s + pltpu.roll(s, shift, axis=0)
                    shift *= 2
                inv_cnt = 1.0 / jnp.minimum(pos + 1, w).astype(F32)
                pooled.append((s[POOL_HALO:] * inv_cnt - u[lo:hi, sl]).astype(BF16))
            x1.append(x[lo:hi] + _pool_project(pooled, wpool_ref, scale_ref[...]))
            yield
        x1 = jnp.concatenate(x1, axis=0)
        (x2,) = yield from _prompt_attend([x1], _gain(ga_ref, layer), k_ref, v_ref, wq_ref, wo_ref)
        mid_ref[rs, :] = x2

    def finish(i):
        rs = slice(i * rows, (i + 1) * rows)
        x2 = mid_ref[rs, :]
        (y,) = yield from _mlp([x2], _gain(gf_ref, layer), wup_ref, wdown_ref, POOL_FF_CHUNK)
        y_ref[rs, :] = y

    _pipelined_step(start, finish, carried,
                    lambda: (_sample_attend(sq_ref, sk_ref, sv_ref, so_ref), POOL_SAMPLE_START))


def _conv_layer_kernel(tiles_per_seq, layer, x_ref, gm_ref, ga_ref, gf_ref, gfin_ref, k_ref, v_ref,
                       sq_ref, sk_ref, sv_ref, win_ref, wconv_ref, wout_ref, wq_ref, wo_ref,
                       wup_ref, wdown_ref, y_ref, state_ref, so_ref, prev_ref):
    t = lax.rem(pl.program_id(0), tiles_per_seq)
    rows = x_ref.shape[0] // SUB_TILES
    halos = [_carried(prev_ref, t)]

    def chain(i):
        rs = slice(i * rows, (i + 1) * rows)
        x = x_ref[rs, :]
        u = _rms(x, _gain(gm_ref, layer)).astype(BF16)
        yield
        gate_b = _dot(u, win_ref[:, :D_MODEL])
        yield
        gate_c = _dot(u, win_ref[:, D_MODEL:2 * D_MODEL])
        yield
        ch = gate_c * _dot(u, win_ref[:, 2 * D_MODEL:])
        halos.append(ch[rows - CONV_HALO:])
        if i == SUB_TILES - 1:
            prev_ref[...] = ch[rows - CONV_HALO:]
            state_ref[...] = ch[rows - CONV_STATE:]
        yield
        ext = jnp.concatenate([halos[i], ch], axis=0)
        conv = ext * _conv_tap(wconv_ref, 0)
        for back in range(1, CONV_WIDTH):
            conv = conv + pltpu.roll(ext, back, axis=0) * _conv_tap(wconv_ref, back)
        conv = conv[CONV_HALO:]
        x1 = x + _dot((gate_b * conv).astype(BF16), wout_ref[...])
        yield
        x2 = yield from _prompt_attend([x1], _gain(ga_ref, layer), k_ref, v_ref, wq_ref, wo_ref)
        (x3,) = yield from _mlp(x2, _gain(gf_ref, layer), wup_ref, wdown_ref, CONV_FF_CHUNK)
        y_ref[rs, :] = _rms(x3, gfin_ref[...])

    _weave(*[(chain(i), 0) for i in range(SUB_TILES)],
           (_sample_attend(sq_ref, sk_ref, sv_ref, so_ref), CONV_SAMPLE_START))


def _prompt_layer(kernel, name, layer, row_tile, vmem_limit, pipelined, x, kb, vb, sq, sk, sv,
                  small, weights, stacked, state_rows, halo, cast_along=()):
    batch, seq, _ = x.shape
    tiles_per_seq = seq // row_tile
    n_tiles = batch * tiles_per_seq
    n_sample = sq.shape[0]
    per_step = n_sample // n_tiles
    assert seq % row_tile == 0 and row_tile // SUB_TILES >= POOL_HALO
    assert per_step * n_tiles == n_sample

    started = (lambda i: jnp.minimum(i, n_tiles - 1)) if pipelined else (lambda i: i)
    finished = (lambda i: jnp.maximum(i - 1, 0)) if pipelined else (lambda i: i)
    seq_of = lambda j: lax.div(j, tiles_per_seq)
    tile_of = lambda j: lax.rem(j, tiles_per_seq)

    xspec = pl.BlockSpec((None, row_tile, D_MODEL),
                         lambda i: (seq_of(started(i)), tile_of(started(i)), 0))
    yspec = pl.BlockSpec((None, row_tile, D_MODEL),
                         lambda i: (seq_of(finished(i)), tile_of(finished(i)), 0))
    stspec = pl.BlockSpec((None, state_rows, D_MODEL), lambda i: (seq_of(started(i)), 0, 0))
    kvspec = pl.BlockSpec((None, None, N_MEM, D_MODEL),
                          lambda i: (layer, seq_of(started(i)), 0, 0))
    sqspec = pl.BlockSpec((per_step, HEAD_ROWS, LANES), lambda i: (started(i), 0, 0))
    skvspec = pl.BlockSpec((None, per_step, N_MEM, HEAD_ROWS, LANES),
                           lambda i: (layer, started(i), 0, 0, 0))
    def slab_of(a, layer_of_a):
        return pl.BlockSpec((None, a.shape[1] // n_tiles, a.shape[2]),
                            lambda i: (layer_of_a, started(i), 0))

    slabs_in = [slab_of(a, la) for a, la in cast_along]
    slabs = [pl.BlockSpec((a.shape[1] // n_tiles, a.shape[2]), lambda i: (started(i), 0))
             for a, _ in cast_along]
    scratch = [pltpu.VMEM((halo, D_MODEL), F32)]
    if pipelined:
        scratch.append(pltpu.VMEM((row_tile, D_MODEL), F32))
    return pl.pallas_call(
        functools.partial(kernel, tiles_per_seq, layer),
        grid=(n_tiles + 1 if pipelined else n_tiles,),
        in_specs=([xspec] + [_vmem()] * len(small) + [kvspec, kvspec, sqspec, skvspec, skvspec]
                  + [_vmem()] * len(weights) + [_of_layer(w, layer) for w in stacked] + slabs_in),
        out_specs=[yspec, stspec, sqspec] + slabs,
        out_shape=[jax.ShapeDtypeStruct(x.shape, F32),
                   jax.ShapeDtypeStruct((batch, state_rows, D_MODEL), F32),
                   jax.ShapeDtypeStruct(sq.shape, F32)]
                  + [jax.ShapeDtypeStruct(a.shape[1:], BF16) for a, _ in cast_along],
        scratch_shapes=scratch,
        compiler_params=pltpu.CompilerParams(
            dimension_semantics=("arbitrary",), vmem_limit_bytes=vmem_limit),
        name=name,
    )(x, *small, kb, vb, sq, sk, sv, *weights, *stacked, *[a for a, _ in cast_along])


def _sample_query(x, g_attn, wq_ref):
    q = _dot(_rms(x, g_attn).astype(BF16), wq_ref[...]) * (MEM_HEAD_DIM ** -0.5)
    return _to_head_rows(q)


def _sample_pool_pre_kernel(layer, x_ref, st_ref, gm_ref, ga_ref, wpool_ref, scale_ref, wq_ref,
                            x1_ref, q_ref, newst_ref):
    x = x_ref[...]
    u = _rms(x, _gain(gm_ref, layer))
    pooled = []
    for g, w in enumerate(POOL_WINDOWS):
        s = u[:, g * POOL_GROUP:(g + 1) * POOL_GROUP]
        for back in range(1, w):
            s = s + st_ref[POOL_STATE - back, :, g * POOL_GROUP:(g + 1) * POOL_GROUP]
        cnt = min(PAST_LEN + 1, w)
        pooled.append((s * (1.0 / cnt) - u[:, g * POOL_GROUP:(g + 1) * POOL_GROUP]).astype(BF16))
    x = x + _pool_project(pooled, wpool_ref, scale_ref[...])
    x1_ref[...] = x
    q_ref[...] = _sample_query(x, _gain(ga_ref, layer), wq_ref)
    newst_ref[:POOL_STATE - 1] = st_ref[1:]
    newst_ref[POOL_STATE - 1] = u


def _sample_conv_pre_kernel(layer, x_ref, st_ref, gm_ref, ga_ref, win_ref, wconv_ref, wout_ref,
                            wq_ref, x1_ref, q_ref, newst_ref):
    x = x_ref[...]
    u = _rms(x, _gain(gm_ref, layer)).astype(BF16)
    gate_b = _dot(u, win_ref[:, :D_MODEL])
    ch = _dot(u, win_ref[:, D_MODEL:2 * D_MODEL]) * _dot(u, win_ref[:, 2 * D_MODEL:])
    conv = ch * _conv_tap(wconv_ref, 0)
    for back in range(1, CONV_WIDTH):
        lo = (CONV_STATE - back) * D_MODEL
        conv = conv + st_ref[:, lo:lo + D_MODEL] * _conv_tap(wconv_ref, back)
    x = x + _dot((gate_b * conv).astype(BF16), wout_ref[...])
    x1_ref[...] = x
    q_ref[...] = _sample_query(x, _gain(ga_ref, layer), wq_ref)
    keep = (CONV_STATE - 1) * D_MODEL
    newst_ref[:, :keep] = st_ref[:, D_MODEL:]
    newst_ref[:, keep:] = ch


def _sample_pre(kernel, name, layer, x, state, small, weights, w_q):
    n = x.shape[0]
    return pl.pallas_call(
        functools.partial(kernel, layer),
        grid=(1,),
        in_specs=[_vmem()] * (2 + len(small) + len(weights)) + [_of_layer(w_q, layer)],
        out_specs=[_vmem()] * 3,
        out_shape=[jax.ShapeDtypeStruct((n, D_MODEL), F32),
                   jax.ShapeDtypeStruct((n, HEAD_ROWS, LANES), F32),
                   jax.ShapeDtypeStruct(state.shape, F32)],
        compiler_params=pltpu.CompilerParams(
            dimension_semantics=("arbitrary",), vmem_limit_bytes=VMEM_LIMIT),
        name=name,
    )(x, state, *small, *weights, w_q)


def _sample_post_kernel(final, layer, x_ref, o_ref, gf_ref, gfin_ref, wo_ref, wup_ref, wdown_ref,
                        y_ref, u_ref, acc_ref):
    c = pl.program_id(0)

    @pl.when(c == 0)
    def _():
        x = x_ref[...] + _dot(_from_head_rows(o_ref[...]).astype(BF16), wo_ref[...])
        u_ref[...] = _rms(x, _gain(gf_ref, layer)).astype(BF16)
        acc_ref[...] = x

    h = jnp.maximum(_dot(u_ref[...], wup_ref[...]), 0.0)
    acc_ref[...] += _dot((h * h).astype(BF16), wdown_ref[...])

    @pl.when(c == pl.num_programs(0) - 1)
    def _():
        x = acc_ref[...]
        y_ref[...] = _rms(x, gfin_ref[...]) if final else x


def _sample_post(x1, o, g_ffn, g_final, wo, wup, wdown, layer, final, name):
    n = x1.shape[0]
    whole = lambda shape: pl.BlockSpec(shape, lambda c: (0,) * len(shape))
    act = whole((n, D_MODEL))
    return pl.pallas_call(
        functools.partial(_sample_post_kernel, final, layer),
        grid=(D_FF // FF_CHUNK,),
        in_specs=[act, whole((n, HEAD_ROWS, LANES)), whole(g_ffn.shape), whole(g_final.shape),
                  pl.BlockSpec((None, D_MODEL, D_MODEL), lambda c: (layer, 0, 0)),
                  pl.BlockSpec((None, D_MODEL, FF_CHUNK), lambda c: (layer, 0, c)),
                  pl.BlockSpec((FF_CHUNK, D_MODEL), lambda c: (c, 0))],
        out_specs=act,
        out_shape=jax.ShapeDtypeStruct(x1.shape, F32),
        scratch_shapes=[pltpu.VMEM((n, D_MODEL), BF16), pltpu.VMEM((n, D_MODEL), F32)],
        compiler_params=pltpu.CompilerParams(
            dimension_semantics=("arbitrary",), vmem_limit_bytes=VMEM_LIMIT),
        name=name,
    )(x1, o, g_ffn, g_final, wo, wup, wdown)


def kernel(x_prompt, x_sample, state_pool, state_conv, cache_mem_k, cache_mem_v, mem_prompt,
           g_mix, g_attn, g_mem, g_ffn, g_final, w_pool, pool_scale,
           w_conv_in, w_conv, w_conv_out, w_q, w_kv, w_o, w_up, w_down):
    batch, seq, _ = x_prompt.shape
    n_sample = x_sample.shape[0]
    assert DEPTH == 2 and x_sample.shape[1] == 1
    assert (batch * N_MEM) % KV_ROW_TILE == 0

    cast = lambda a: a.astype(BF16)
    wpool_b = cast(w_pool[0])
    wq_b, wo_b, wdown0_b = cast(w_q), cast(w_o), cast(w_down[0])
    scale0, wconv0, gfin = pool_scale[0].reshape(1, D_MODEL), w_conv[0], g_final.reshape(1, D_MODEL)

    k_all, v_all, kb_all, vb_all, wup_b = _mem_kv(
        mem_prompt.reshape(batch * N_MEM, D_MODEL), g_mem, w_kv, w_up)
    kb_all = kb_all.reshape(DEPTH, batch, N_MEM, D_MODEL)
    vb_all = vb_all.reshape(DEPTH, batch, N_MEM, D_MODEL)
    xs = x_sample.reshape(n_sample, D_MODEL)
    ck, cv = _cache_rows(cache_mem_k), _cache_rows(cache_mem_v)

    xs1, q0, pool_s = _sample_pre(
        _sample_pool_pre_kernel, "sample_pool_pre", 0, xs,
        state_pool[0].transpose(1, 0, 2),
        [g_mix, g_attn], [wpool_b, scale0], wq_b)
    x1, pool_p, o0, win_b, wout_b, wdown1_b = _prompt_layer(
        _pool_layer_kernel, "prompt_pool_layer", 0, POOL_ROW_TILE, POOL_VMEM_LIMIT, True,
        x_prompt, kb_all, vb_all, q0, ck, cv, [g_mix, g_attn, g_ffn],
        [wpool_b, scale0], [wq_b, wo_b, wup_b, wdown0_b], POOL_STATE, POOL_HALO,
        cast_along=[(w_conv_in, 0), (w_conv_out, 0), (w_down, 1)])
    xs2 = _sample_post(xs1, o0, g_ffn, gfin, wo_b, wup_b, wdown0_b, 0, False, "sample_post0")

    xs3, q1, conv_s = _sample_pre(
        _sample_conv_pre_kernel, "sample_conv_pre", 1, xs2,
        state_conv[0].reshape(n_sample, CONV_STATE * D_MODEL),
        [g_mix, g_attn], [win_b, wconv0, wout_b], wq_b)
    y_prompt, conv_p, o1 = _prompt_layer(
        _conv_layer_kernel, "prompt_conv_layer", 1, CONV_ROW_TILE, CONV_VMEM_LIMIT, False,
        x1, kb_all, vb_all, q1, ck, cv, [g_mix, g_attn, g_ffn, gfin],
        [win_b, wconv0, wout_b], [wq_b, wo_b, wup_b, wdown1_b], CONV_STATE, CONV_HALO)
    y_sample = _sample_post(xs3, o1, g_ffn, gfin, wo_b, wup_b, wdown1_b, 1, True, "sample_post1")

    mem_shape = (DEPTH, batch, N_MEM, MEM_HEADS, MEM_HEAD_DIM)
    return (y_prompt,
            y_sample.reshape(n_sample, 1, D_MODEL),
            pool_p.reshape(1, batch, POOL_STATE, D_MODEL),
            conv_p.reshape(1, batch, CONV_STATE, D_MODEL),
            _unhead_rows(k_all.reshape(-1, HEAD_ROWS, LANES)).reshape(mem_shape),
            _unhead_rows(v_all.reshape(-1, HEAD_ROWS, LANES)).reshape(mem_shape),
            pool_s.transpose(1, 0, 2)[None],
            conv_s.reshape(1, n_sample, CONV_STATE, D_MODEL))
```

```python
import functools

import jax
import jax.numpy as jnp
from jax import lax
from jax.experimental import pallas as pl
from jax.experimental.pallas import tpu as pltpu

D_MODEL = 1024
DEPTH = 2
PAST_LEN = 16384
POOL_WINDOWS = (2, 4, 8, 16)
POOL_GROUP = D_MODEL // len(POOL_WINDOWS)
POOL_STATE = max(POOL_WINDOWS) - 1
CONV_WIDTH = 3
CONV_STATE = CONV_WIDTH - 1
D_FF = 4 * D_MODEL
N_MEM = 256
MEM_HEADS = 4
MEM_HEAD_DIM = D_MODEL // MEM_HEADS
EPS = 1e-6

SUBLANES = 8
LANES = 128
HEAD_ROWS = D_MODEL // LANES
POOL_HALO = 16
CONV_HALO = SUBLANES
POOL_ROW_TILE = 512
CONV_ROW_TILE = 512
SUB_TILES = 2
POOL_SUB_TILES = 1
KV_ROW_TILE = 512
FF_CHUNK = 1024
POOL_FF_CHUNK = 512
CONV_FF_CHUNK = 1024
POOL_SAMPLE_START = 4
POOL_CHUNK = 64
CONV_SAMPLE_START = 2
VMEM_LIMIT = 56 * 1024 * 1024
POOL_VMEM_LIMIT = 62 * 1024 * 1024
CONV_VMEM_LIMIT = 62 * 1024 * 1024

F32 = jnp.float32
BF16 = jnp.bfloat16

_dot = functools.partial(jnp.dot, preferred_element_type=F32)


def _rms(x, g):
    y = x * lax.rsqrt(jnp.mean(x * x, axis=-1, keepdims=True) + EPS)
    return y * g


def _vmem():
    return pl.BlockSpec(memory_space=pltpu.VMEM)


def _of_layer(w, layer):
    if w.ndim == 2:
        return pl.BlockSpec(w.shape, lambda i: (0, 0), pipeline_mode=pl.Buffered(1))
    return pl.BlockSpec((None, *w.shape[1:]), lambda i: (layer, 0, 0),
                        pipeline_mode=pl.Buffered(1))


def _gain(g_ref, layer):
    return g_ref[layer:layer + 1, :]


def _weave(*delayed):
    live = list(delayed)
    round_ = 0
    while live:
        for item in list(live):
            stages, first_round = item
            if round_ >= first_round:
                try:
                    next(stages)
                except StopIteration:
                    live.remove(item)
        round_ += 1


def _mlp(xs, g_ffn, wup_ref, wdown_ref, chunk=FF_CHUNK):
    us = [_rms(x, g_ffn).astype(BF16) for x in xs]
    yield
    accs = list(xs)
    for c in range(D_FF // chunk):
        sl = slice(c * chunk, (c + 1) * chunk)
        hs = [jnp.maximum(_dot(u, wup_ref[:, sl]), 0.0) for u in us]
        yield
        accs = [a + _dot((h * h).astype(BF16), wdown_ref[sl, :]) for a, h in zip(accs, hs)]
        yield
    return accs


def _prompt_attend(xs, g_attn, k_ref, v_ref, wq_ref, wo_ref):
    us = [_rms(x, g_attn).astype(BF16) for x in xs]
    yield
    qs = [_dot(u, wq_ref[...]) * (MEM_HEAD_DIM ** -0.5) for u in us]
    yield
    heads = [[] for _ in xs]
    for h in range(MEM_HEADS):
        sl = slice(h * MEM_HEAD_DIM, (h + 1) * MEM_HEAD_DIM)
        ss = [lax.dot_general(q[:, sl].astype(BF16), k_ref[:, sl],
                              (((1,), (1,)), ((), ())), preferred_element_type=F32) for q in qs]
        es = [jnp.exp(s - jnp.max(s, axis=-1, keepdims=True)) for s in ss]
        for i, e in enumerate(es):
            inv = 1.0 / jnp.sum(e, axis=-1, keepdims=True)
            heads[i].append((_dot(e.astype(BF16), v_ref[:, sl]) * inv).astype(BF16))
        yield
    os_ = [jnp.concatenate(hd, axis=-1) for hd in heads]
    return [x + _dot(o, wo_ref[...]) for x, o in zip(xs, os_)]


def _sample_attend(q_ref, k_ref, v_ref, o_ref):
    for j in range(q_ref.shape[0]):
        prod = k_ref[j] * q_ref[j][None]
        part = jnp.sum(prod, axis=-1, keepdims=True)
        s = part + pltpu.roll(part, MEM_HEADS, axis=1)
        e = jnp.exp(s - jnp.max(s, axis=0, keepdims=True))
        inv = 1.0 / jnp.sum(e, axis=0)
        yield
        o_ref[j] = jnp.sum(e * v_ref[j], axis=0) * inv
        yield


def _conv_tap(wconv_ref, back):
    return wconv_ref[CONV_WIDTH - 1 - back]


def _pool_project(pooled, wpool_ref, scale):
    ys = [_dot(p, wpool_ref[g]) for g, p in enumerate(pooled)]
    return jnp.concatenate(ys, axis=-1) * scale


def _carried(prev_ref, t):
    @pl.when(t == 0)
    def _():
        prev_ref[...] = jnp.zeros_like(prev_ref)

    return prev_ref[...]


def _to_head_rows(a):
    tiles = [a[:, (c % MEM_HEADS) * MEM_HEAD_DIM + (c // MEM_HEADS) * LANES:][:, :LANES]
             for c in range(HEAD_ROWS)]
    return jnp.concatenate(tiles, axis=-1).reshape(a.shape[0], HEAD_ROWS, LANES)


def _from_head_rows(a):
    flat = a.reshape(a.shape[0], D_MODEL)
    per_head = MEM_HEAD_DIM // LANES
    tiles = [flat[:, ((c % per_head) * MEM_HEADS + c // per_head) * LANES:][:, :LANES]
             for c in range(HEAD_ROWS)]
    return jnp.concatenate(tiles, axis=-1)


def _unhead_rows(a):
    n = a.shape[0]
    a = a.reshape(n, MEM_HEAD_DIM // LANES, MEM_HEADS, LANES)
    return a.transpose(0, 2, 1, 3).reshape(n, D_MODEL)


def _cache_rows(c):
    depth, n = c.shape[:2]
    c = c.reshape(depth, n, N_MEM, MEM_HEADS, MEM_HEAD_DIM // LANES, LANES)
    return c.transpose(0, 1, 2, 4, 3, 5).reshape(depth, n, N_MEM, HEAD_ROWS, LANES)


def _memkv_kernel(m_ref, g_ref, w_ref, wup_ref, k_ref, v_ref, kb_ref, vb_ref, wupb_ref, wb_ref):
    @pl.when(pl.program_id(1) == 0)
    def _():
        wb_ref[...] = w_ref[...].astype(BF16)

    wupb_ref[...] = wup_ref[...].astype(BF16)

    u = _rms(m_ref[...], g_ref[pl.ds(pl.program_id(0), 1), :]).astype(BF16)
    kv = _dot(u, wb_ref[...])
    k = kv[:, :D_MODEL]
    v = kv[:, D_MODEL:]
    k_ref[...] = _to_head_rows(k)
    v_ref[...] = _to_head_rows(v)
    kb_ref[...] = k.astype(BF16)
    vb_ref[...] = v.astype(BF16)


def _mem_kv(mem, g_mem, w_kv, w_up):
    rows = mem.shape[0]
    row_tiles = rows // KV_ROW_TILE
    slab = D_MODEL // row_tiles
    assert slab * row_tiles == D_MODEL
    out = jax.ShapeDtypeStruct((DEPTH, rows, HEAD_ROWS, LANES), F32)
    outb = jax.ShapeDtypeStruct((DEPTH, rows, D_MODEL), BF16)
    hspec = pl.BlockSpec((None, KV_ROW_TILE, HEAD_ROWS, LANES), lambda l, r: (l, r, 0, 0))
    ospec = pl.BlockSpec((None, KV_ROW_TILE, D_MODEL), lambda l, r: (l, r, 0))
    wspec = pl.BlockSpec((None, slab, D_FF), lambda l, r: (l, r, 0))
    return pl.pallas_call(
        _memkv_kernel,
        grid=(DEPTH, row_tiles),
        in_specs=[
            pl.BlockSpec((KV_ROW_TILE, D_MODEL), lambda l, r: (r, 0)),
            _vmem(),
            pl.BlockSpec((None, D_MODEL, 2 * D_MODEL), lambda l, r: (l, 0, 0)),
            wspec,
        ],
        out_specs=[hspec, hspec, ospec, ospec, wspec],
        out_shape=[out, out, outb, outb, jax.ShapeDtypeStruct(w_up.shape, BF16)],
        scratch_shapes=[pltpu.VMEM((D_MODEL, 2 * D_MODEL), BF16)],
        compiler_params=pltpu.CompilerParams(
            dimension_semantics=("arbitrary", "arbitrary"), vmem_limit_bytes=VMEM_LIMIT),
        name="mem_kv",
    )(mem, g_mem, w_kv, w_up)


def _pipelined_step(start, finish, carried, sample):
    step = pl.program_id(0)
    last = pl.num_programs(0) - 1

    def starts():
        halos = [carried]
        return [(start(i, halos), 0) for i in range(POOL_SUB_TILES)]

    finishes = lambda: [(finish(i), 0) for i in range(POOL_SUB_TILES)]

    @pl.when(step == 0)
    def _():
        _weave(*starts(), sample())

    @pl.when(jnp.logical_and(step > 0, step < last))
    def _():
        _weave(*finishes(), *starts(), sample())

    @pl.when(step == last)
    def _():
        _weave(*finishes())


def _started_tile(tiles_per_seq):
    step = jnp.minimum(pl.program_id(0), pl.num_programs(0) - 2)
    return lax.rem(step, tiles_per_seq)


def _pool_layer_kernel(tiles_per_seq, layer, x_ref, gm_ref, ga_ref, gf_ref, k_ref, v_ref, sq_ref,
                       sk_ref, sv_ref, wpool_ref, scale_ref, wq_ref, wo_ref, wup_ref, wdown_ref,
                       win_ref, wout_ref, y_ref, state_ref, so_ref, winb_ref, woutb_ref,
                       prev_ref, mid_ref):
    t = _started_tile(tiles_per_seq)
    tm = x_ref.shape[0]
    rows = tm // POOL_SUB_TILES
    carried = _carried(prev_ref, t)
    winb_ref[...] = win_ref[...].astype(BF16)
    woutb_ref[...] = wout_ref[...].astype(BF16)

    def start(i, halos):
        rs = slice(i * rows, (i + 1) * rows)
        x = x_ref[rs, :]
        u = _rms(x, _gain(gm_ref, layer))
        halos.append(u[rows - POOL_HALO:])
        if i == POOL_SUB_TILES - 1:
            prev_ref[...] = u[rows - POOL_HALO:]
            state_ref[...] = u[rows - POOL_STATE:]
        yield
        ext = jnp.concatenate([halos[i], u], axis=0)
        x1 = []
        for lo in range(0, rows, POOL_CHUNK):
            hi = lo + POOL_CHUNK
            pos = t * tm + i * rows + lo + lax.broadcasted_iota(jnp.int32, (POOL_CHUNK, 1), 0)
            pooled = []
            for g, w in enumerate(POOL_WINDOWS):
                sl = slice(g * POOL_GROUP, (g + 1) * POOL_GROUP)
                s = ext[lo:hi + POOL_HALO, sl]
                shift = 1
                while shift < w:
                    s = s + pltpu.roll(s, shift, axis=0)
                    shift *= 2
                inv_cnt = 1.0 / jnp.minimum(pos + 1, w).astype(F32)
                pooled.append((s[POOL_HALO:] * inv_cnt - u[lo:hi, sl]).astype(BF16))
            x1.append(x[lo:hi] + _pool_project(pooled, wpool_ref, scale_ref[...]))
            yield
        x1 = jnp.concatenate(x1, axis=0)
        (x2,) = yield from _prompt_attend([x1], _gain(ga_ref, layer), k_ref, v_ref, wq_ref, wo_ref)
        mid_ref[rs, :] = x2

    def finish(i):
        rs = slice(i * rows, (i + 1) * rows)
        x2 = mid_ref[rs, :]
        (y,) = yield from _mlp([x2], _gain(gf_ref, layer), wup_ref, wdown_ref, POOL_FF_CHUNK)
        y_ref[rs, :] = y

    _pipelined_step(start, finish, carried,
                    lambda: (_sample_attend(sq_ref, sk_ref, sv_ref, so_ref), POOL_SAMPLE_START))


def _conv_layer_kernel(tiles_per_seq, layer, x_ref, gm_ref, ga_ref, gf_ref, gfin_ref, k_ref, v_ref,
                       sq_ref, sk_ref, sv_ref, win_ref, wconv_ref, wout_ref, wq_ref, wo_ref,
                       wup_ref, wdown_ref, y_ref, state_ref, so_ref, prev_ref):
    t = lax.rem(pl.program_id(0), tiles_per_seq)
    rows = x_ref.shape[0] // SUB_TILES
    halos = [_carried(prev_ref, t)]

    def chain(i):
        rs = slice(i * rows, (i + 1) * rows)
        x = x_ref[rs, :]
        u = _rms(x, _gain(gm_ref, layer)).astype(BF16)
        yield
        gate_b = _dot(u, win_ref[:, :D_MODEL])
        yield
        gate_c = _dot(u, win_ref[:, D_MODEL:2 * D_MODEL])
        yield
        ch = gate_c * _dot(u, win_ref[:, 2 * D_MODEL:])
        halos.append(ch[rows - CONV_HALO:])
        if i == SUB_TILES - 1:
            prev_ref[...] = ch[rows - CONV_HALO:]
            state_ref[...] = ch[rows - CONV_STATE:]
        yield
        ext = jnp.concatenate([halos[i], ch], axis=0)
        conv = ext * _conv_tap(wconv_ref, 0)
        for back in range(1, CONV_WIDTH):
            conv = conv + pltpu.roll(ext, back, axis=0) * _conv_tap(wconv_ref, back)
        conv = conv[CONV_HALO:]
        x1 = x + _dot((gate_b * conv).astype(BF16), wout_ref[...])
        yield
        x2 = yield from _prompt_attend([x1], _gain(ga_ref, layer), k_ref, v_ref, wq_ref, wo_ref)
        (x3,) = yield from _mlp(x2, _gain(gf_ref, layer), wup_ref, wdown_ref, CONV_FF_CHUNK)
        y_ref[rs, :] = _rms(x3, gfin_ref[...])

    _weave(*[(chain(i), 0) for i in range(SUB_TILES)],
           (_sample_attend(sq_ref, sk_ref, sv_ref, so_ref), CONV_SAMPLE_START))


def _prompt_layer(kernel, name, layer, row_tile, vmem_limit, pipelined, x, kb, vb, sq, sk, sv,
                  small, weights, stacked, state_rows, halo, cast_along=()):
    batch, seq, _ = x.shape
    tiles_per_seq = seq // row_tile
    n_tiles = batch * tiles_per_seq
    n_sample = sq.shape[0]
    per_step = n_sample // n_tiles
    assert seq % row_tile == 0 and row_tile // SUB_TILES >= POOL_HALO
    assert per_step * n_tiles == n_sample

    started = (lambda i: jnp.minimum(i, n_tiles - 1)) if pipelined else (lambda i: i)
    finished = (lambda i: jnp.maximum(i - 1, 0)) if pipelined else (lambda i: i)
    seq_of = lambda j: lax.div(j, tiles_per_seq)
    tile_of = lambda j: lax.rem(j, tiles_per_seq)

    xspec = pl.BlockSpec((None, row_tile, D_MODEL),
                         lambda i: (seq_of(started(i)), tile_of(started(i)), 0))
    yspec = pl.BlockSpec((None, row_tile, D_MODEL),
                         lambda i: (seq_of(finished(i)), tile_of(finished(i)), 0))
    stspec = pl.BlockSpec((None, state_rows, D_MODEL), lambda i: (seq_of(started(i)), 0, 0))
    kvspec = pl.BlockSpec((None, None, N_MEM, D_MODEL),
                          lambda i: (layer, seq_of(started(i)), 0, 0))
    sqspec = pl.BlockSpec((per_step, HEAD_ROWS, LANES), lambda i: (started(i), 0, 0))
    skvspec = pl.BlockSpec((None, per_step, N_MEM, HEAD_ROWS, LANES),
                           lambda i: (layer, started(i), 0, 0, 0))
    def slab_of(a, layer_of_a):
        return pl.BlockSpec((None, a.shape[1] // n_tiles, a.shape[2]),
                            lambda i: (layer_of_a, started(i), 0))

    slabs_in = [slab_of(a, la) for a, la in cast_along]
    slabs = [pl.BlockSpec((a.shape[1] // n_tiles, a.shape[2]), lambda i: (started(i), 0))
             for a, _ in cast_along]
    scratch = [pltpu.VMEM((halo, D_MODEL), F32)]
    if pipelined:
        scratch.append(pltpu.VMEM((row_tile, D_MODEL), F32))
    return pl.pallas_call(
        functools.partial(kernel, tiles_per_seq, layer),
        grid=(n_tiles + 1 if pipelined else n_tiles,),
        in_specs=([xspec] + [_vmem()] * len(small) + [kvspec, kvspec, sqspec, skvspec, skvspec]
                  + [_vmem()] * len(weights) + [_of_layer(w, layer) for w in stacked] + slabs_in),
        out_specs=[yspec, stspec, sqspec] + slabs,
        out_shape=[jax.ShapeDtypeStruct(x.shape, F32),
                   jax.ShapeDtypeStruct((batch, state_rows, D_MODEL), F32),
                   jax.ShapeDtypeStruct(sq.shape, F32)]
                  + [jax.ShapeDtypeStruct(a.shape[1:], BF16) for a, _ in cast_along],
        scratch_shapes=scratch,
        compiler_params=pltpu.CompilerParams(
            dimension_semantics=("arbitrary",), vmem_limit_bytes=vmem_limit),
        name=name,
    )(x, *small, kb, vb, sq, sk, sv, *weights, *stacked, *[a for a, _ in cast_along])


def _sample_query(x, g_attn, wq_ref):
    q = _dot(_rms(x, g_attn).astype(BF16), wq_ref[...]) * (MEM_HEAD_DIM ** -0.5)
    return _to_head_rows(q)


def _sample_pool_pre_kernel(layer, x_ref, st_ref, gm_ref, ga_ref, wpool_ref, scale_ref, wq_ref,
                            x1_ref, q_ref, newst_ref):
    x = x_ref[:, 0, :]
    u = _rms(x, _gain(gm_ref, layer))
    pooled = []
    for g, w in enumerate(POOL_WINDOWS):
        s = u[:, g * POOL_GROUP:(g + 1) * POOL_GROUP]
        for back in range(1, w):
            s = s + st_ref[POOL_STATE - back, :, g * POOL_GROUP:(g + 1) * POOL_GROUP]
        cnt = min(PAST_LEN + 1, w)
        pooled.append((s * (1.0 / cnt) - u[:, g * POOL_GROUP:(g + 1) * POOL_GROUP]).astype(BF16))
    x = x + _pool_project(pooled, wpool_ref, scale_ref[...])
    x1_ref[...] = x
    q_ref[...] = _sample_query(x, _gain(ga_ref, layer), wq_ref)
    newst_ref[:POOL_STATE - 1] = st_ref[1:]
    newst_ref[POOL_STATE - 1] = u


def _sample_conv_pre_kernel(layer, x_ref, st_ref, gm_ref, ga_ref, win_ref, wconv_ref, wout_ref,
                            wq_ref, x1_ref, q_ref, newst_ref):
    x = x_ref[...]
    u = _rms(x, _gain(gm_ref, layer)).astype(BF16)
    gate_b = _dot(u, win_ref[:, :D_MODEL])
    ch = _dot(u, win_ref[:, D_MODEL:2 * D_MODEL]) * _dot(u, win_ref[:, 2 * D_MODEL:])
    conv = ch * _conv_tap(wconv_ref, 0)
    for back in range(1, CONV_WIDTH):
        conv = conv + st_ref[:, CONV_STATE - back, :] * _conv_tap(wconv_ref, back)
    x = x + _dot((gate_b * conv).astype(BF16), wout_ref[...])
    x1_ref[...] = x
    q_ref[...] = _sample_query(x, _gain(ga_ref, layer), wq_ref)
    for age in range(CONV_STATE - 1):
        newst_ref[:, age, :] = st_ref[:, age + 1, :]
    newst_ref[:, CONV_STATE - 1, :] = ch


def _sample_pre(kernel, name, layer, x, state, small, weights, w_q):
    n = x.shape[0]
    return pl.pallas_call(
        functools.partial(kernel, layer),
        grid=(1,),
        in_specs=[_vmem()] * (2 + len(small) + len(weights)) + [_of_layer(w_q, layer)],
        out_specs=[_vmem()] * 3,
        out_shape=[jax.ShapeDtypeStruct((n, D_MODEL), F32),
                   jax.ShapeDtypeStruct((n, HEAD_ROWS, LANES), F32),
                   jax.ShapeDtypeStruct(state.shape, F32)],
        compiler_params=pltpu.CompilerParams(
            dimension_semantics=("arbitrary",), vmem_limit_bytes=VMEM_LIMIT),
        name=name,
    )(x, state, *small, *weights, w_q)


def _sample_post_kernel(final, layer, x_ref, o_ref, gf_ref, gfin_ref, wo_ref, wup_ref, wdown_ref,
                        y_ref, u_ref, acc_ref):
    c = pl.program_id(0)

    @pl.when(c == 0)
    def _():
        x = x_ref[...] + _dot(_from_head_rows(o_ref[...]).astype(BF16), wo_ref[...])
        u_ref[...] = _rms(x, _gain(gf_ref, layer)).astype(BF16)
        acc_ref[...] = x

    h = jnp.maximum(_dot(u_ref[...], wup_ref[...]), 0.0)
    acc_ref[...] += _dot((h * h).astype(BF16), wdown_ref[...])

    @pl.when(c == pl.num_programs(0) - 1)
    def _():
        x = acc_ref[...]
        if final:
            y_ref[:, 0, :] = _rms(x, gfin_ref[...])
        else:
            y_ref[...] = x


def _sample_post(x1, o, g_ffn, g_final, wo, wup, wdown, layer, final, name):
    n = x1.shape[0]
    whole = lambda shape: pl.BlockSpec(shape, lambda c: (0,) * len(shape))
    act = whole((n, D_MODEL))
    out_shape = (n, 1, D_MODEL) if final else (n, D_MODEL)
    return pl.pallas_call(
        functools.partial(_sample_post_kernel, final, layer),
        grid=(D_FF // FF_CHUNK,),
        in_specs=[act, whole((n, HEAD_ROWS, LANES)), whole(g_ffn.shape), whole(g_final.shape),
                  pl.BlockSpec((None, D_MODEL, D_MODEL), lambda c: (layer, 0, 0)),
                  pl.BlockSpec((None, D_MODEL, FF_CHUNK), lambda c: (layer, 0, c)),
                  pl.BlockSpec((None, FF_CHUNK, D_MODEL), lambda c: (layer, c, 0))],
        out_specs=whole(out_shape),
        out_shape=jax.ShapeDtypeStruct(out_shape, F32),
        scratch_shapes=[pltpu.VMEM((n, D_MODEL), BF16), pltpu.VMEM((n, D_MODEL), F32)],
        compiler_params=pltpu.CompilerParams(
            dimension_semantics=("arbitrary",), vmem_limit_bytes=VMEM_LIMIT),
        name=name,
    )(x1, o, g_ffn, g_final, wo, wup, wdown)


def kernel(x_prompt, x_sample, state_pool, state_conv, cache_mem_k, cache_mem_v, mem_prompt,
           g_mix, g_attn, g_mem, g_ffn, g_final, w_pool, pool_scale,
           w_conv_in, w_conv, w_conv_out, w_q, w_kv, w_o, w_up, w_down):
    batch, seq, _ = x_prompt.shape
    n_sample = x_sample.shape[0]
    assert DEPTH == 2 and x_sample.shape[1] == 1
    assert (batch * N_MEM) % KV_ROW_TILE == 0

    cast = lambda a: a.astype(BF16)
    wpool_b = cast(w_pool[0])
    wq_b, wo_b, wdown_b = cast(w_q), cast(w_o), cast(w_down)
    scale0, gfin = pool_scale[0].reshape(1, D_MODEL), g_final.reshape(1, D_MODEL)
    wconv0 = w_conv.transpose(1, 0, 2)

    k_all, v_all, kb_all, vb_all, wup_b = _mem_kv(
        mem_prompt.reshape(batch * N_MEM, D_MODEL), g_mem, w_kv, w_up)
    kb_all = kb_all.reshape(DEPTH, batch, N_MEM, D_MODEL)
    vb_all = vb_all.reshape(DEPTH, batch, N_MEM, D_MODEL)
    ck, cv = _cache_rows(cache_mem_k), _cache_rows(cache_mem_v)

    xs1, q0, pool_s = _sample_pre(
        _sample_pool_pre_kernel, "sample_pool_pre", 0, x_sample,
        state_pool[0].transpose(1, 0, 2),
        [g_mix, g_attn], [wpool_b, scale0], wq_b)
    x1, pool_p, o0, win_b, wout_b = _prompt_layer(
        _pool_layer_kernel, "prompt_pool_layer", 0, POOL_ROW_TILE, POOL_VMEM_LIMIT, True,
        x_prompt, kb_all, vb_all, q0, ck, cv, [g_mix, g_attn, g_ffn],
        [wpool_b, scale0], [wq_b, wo_b, wup_b, wdown_b], POOL_STATE, POOL_HALO,
        cast_along=[(w_conv_in, 0), (w_conv_out, 0)])
    xs2 = _sample_post(xs1, o0, g_ffn, gfin, wo_b, wup_b, wdown_b, 0, False, "sample_post0")

    xs3, q1, conv_s = _sample_pre(
        _sample_conv_pre_kernel, "sample_conv_pre", 1, xs2,
        state_conv[0],
        [g_mix, g_attn], [win_b, wconv0, wout_b], wq_b)
    y_prompt, conv_p, o1 = _prompt_layer(
        _conv_layer_kernel, "prompt_conv_layer", 1, CONV_ROW_TILE, CONV_VMEM_LIMIT, False,
        x1, kb_all, vb_all, q1, ck, cv, [g_mix, g_attn, g_ffn, gfin],
        [win_b, wconv0, wout_b], [wq_b, wo_b, wup_b, wdown_b], CONV_STATE, CONV_HALO)
    y_sample = _sample_post(xs3, o1, g_ffn, gfin, wo_b, wup_b, wdown_b, 1, True, "sample_post1")

    mem_shape = (DEPTH, batch, N_MEM, MEM_HEADS, MEM_HEAD_DIM)
    return (y_prompt,
            y_sample,
            pool_p.reshape(1, batch, POOL_STATE, D_MODEL),
            conv_p.reshape(1, batch, CONV_STATE, D_MODEL),
            _unhead_rows(k_all.reshape(-1, HEAD_ROWS, LANES)).reshape(mem_shape),
            _unhead_rows(v_all.reshape(-1, HEAD_ROWS, LANES)).reshape(mem_shape),
            pool_s.transpose(1, 0, 2)[None],
            conv_s[None])
```

```python
import functools

import jax
import jax.numpy as jnp
from jax import lax
from jax.experimental import pallas as pl
from jax.experimental.pallas import tpu as pltpu

D_MODEL = 1024
DEPTH = 2
PAST_LEN = 16384
POOL_WINDOWS = (2, 4, 8, 16)
POOL_GROUP = D_MODEL // len(POOL_WINDOWS)
POOL_STATE = max(POOL_WINDOWS) - 1
CONV_WIDTH = 3
CONV_STATE = CONV_WIDTH - 1
D_FF = 4 * D_MODEL
N_MEM = 256
MEM_HEADS = 4
MEM_HEAD_DIM = D_MODEL // MEM_HEADS
EPS = 1e-6

SUBLANES = 8
LANES = 128
HEAD_ROWS = D_MODEL // LANES
POOL_HALO = 16
CONV_HALO = SUBLANES
POOL_ROW_TILE = 512
CONV_ROW_TILE = 512
SUB_TILES = 2
POOL_SUB_TILES = 1
KV_ROW_TILE = 512
FF_CHUNK = 1024
POOL_FF_CHUNK = 512
CONV_FF_CHUNK = 1024
POOL_SAMPLE_START = 4
POOL_CHUNK = 64
CONV_SAMPLE_START = 2
VMEM_LIMIT = 56 * 1024 * 1024
POOL_VMEM_LIMIT = 62 * 1024 * 1024
CONV_VMEM_LIMIT = 62 * 1024 * 1024

F32 = jnp.float32
BF16 = jnp.bfloat16

_dot = functools.partial(jnp.dot, preferred_element_type=F32)


def _rms(x, g):
    y = x * lax.rsqrt(jnp.mean(x * x, axis=-1, keepdims=True) + EPS)
    return y * g


def _vmem():
    return pl.BlockSpec(memory_space=pltpu.VMEM)


def _of_layer(w, layer):
    if w.ndim == 2:
        return pl.BlockSpec(w.shape, lambda i: (0, 0), pipeline_mode=pl.Buffered(1))
    return pl.BlockSpec((None, *w.shape[1:]), lambda i: (layer, 0, 0),
                        pipeline_mode=pl.Buffered(1))


def _gain(g_ref, layer):
    return g_ref[layer:layer + 1, :]


def _weave(*delayed):
    live = list(delayed)
    round_ = 0
    while live:
        for item in list(live):
            stages, first_round = item
            if round_ >= first_round:
                try:
                    next(stages)
                except StopIteration:
                    live.remove(item)
        round_ += 1


def _mlp(xs, g_ffn, wup_ref, wdown_ref, chunk=FF_CHUNK):
    us = [_rms(x, g_ffn).astype(BF16) for x in xs]
    yield
    accs = list(xs)
    for c in range(D_FF // chunk):
        sl = slice(c * chunk, (c + 1) * chunk)
        hs = [jnp.maximum(_dot(u, wup_ref[:, sl]), 0.0) for u in us]
        yield
        accs = [a + _dot((h * h).astype(BF16), wdown_ref[sl, :]) for a, h in zip(accs, hs)]
        yield
    return accs


def _prompt_attend(xs, g_attn, k_ref, v_ref, wq_ref, wo_ref):
    us = [_rms(x, g_attn).astype(BF16) for x in xs]
    yield
    qs = [_dot(u, wq_ref[...]) * (MEM_HEAD_DIM ** -0.5) for u in us]
    yield
    heads = [[] for _ in xs]
    for h in range(MEM_HEADS):
        sl = slice(h * MEM_HEAD_DIM, (h + 1) * MEM_HEAD_DIM)
        ss = [lax.dot_general(q[:, sl].astype(BF16), k_ref[:, sl],
                              (((1,), (1,)), ((), ())), preferred_element_type=F32) for q in qs]
        es = [jnp.exp(s - jnp.max(s, axis=-1, keepdims=True)) for s in ss]
        for i, e in enumerate(es):
            inv = 1.0 / jnp.sum(e, axis=-1, keepdims=True)
            heads[i].append((_dot(e.astype(BF16), v_ref[:, sl]) * inv).astype(BF16))
        yield
    os_ = [jnp.concatenate(hd, axis=-1) for hd in heads]
    return [x + _dot(o, wo_ref[...]) for x, o in zip(xs, os_)]


def _sample_attend(q_ref, k_ref, v_ref, o_ref):
    for j in range(q_ref.shape[0]):
        prod = k_ref[j] * q_ref[j][None]
        part = jnp.sum(prod, axis=-1, keepdims=True)
        s = part + pltpu.roll(part, MEM_HEADS, axis=1)
        e = jnp.exp(s - jnp.max(s, axis=0, keepdims=True))
        inv = 1.0 / jnp.sum(e, axis=0)
        yield
        o_ref[j] = jnp.sum(e * v_ref[j], axis=0) * inv
        yield


def _conv_tap(wconv_ref, back):
    return wconv_ref[CONV_WIDTH - 1 - back]


def _pool_project(pooled, wpool_ref, scale):
    ys = [_dot(p, wpool_ref[g]) for g, p in enumerate(pooled)]
    return jnp.concatenate(ys, axis=-1) * scale


def _carried(prev_ref, t):
    @pl.when(t == 0)
    def _():
        prev_ref[...] = jnp.zeros_like(prev_ref)

    return prev_ref[...]


def _to_head_rows(a):
    tiles = [a[:, (c % MEM_HEADS) * MEM_HEAD_DIM + (c // MEM_HEADS) * LANES:][:, :LANES]
             for c in range(HEAD_ROWS)]
    return jnp.concatenate(tiles, axis=-1).reshape(a.shape[0], HEAD_ROWS, LANES)


def _from_head_rows(a):
    flat = a.reshape(a.shape[0], D_MODEL)
    per_head = MEM_HEAD_DIM // LANES
    tiles = [flat[:, ((c % per_head) * MEM_HEADS + c // per_head) * LANES:][:, :LANES]
             for c in range(HEAD_ROWS)]
    return jnp.concatenate(tiles, axis=-1)


def _unhead_rows(a):
    n = a.shape[0]
    a = a.reshape(n, MEM_HEAD_DIM // LANES, MEM_HEADS, LANES)
    return a.transpose(0, 2, 1, 3).reshape(n, D_MODEL)


def _cache_rows(c):
    depth, n = c.shape[:2]
    c = c.reshape(depth, n, N_MEM, MEM_HEADS, MEM_HEAD_DIM // LANES, LANES)
    return c.transpose(0, 1, 2, 4, 3, 5).reshape(depth, n, N_MEM, HEAD_ROWS, LANES)


def _memkv_kernel(m_ref, g_ref, w_ref, wup_ref, k_ref, v_ref, kb_ref, vb_ref, wupb_ref, wb_ref):
    @pl.when(pl.program_id(1) == 0)
    def _():
        wb_ref[...] = w_ref[...].astype(BF16)

    wupb_ref[...] = wup_ref[...].astype(BF16)

    u = _rms(m_ref[...], g_ref[pl.ds(pl.program_id(0), 1), :]).astype(BF16)
    kv = _dot(u, wb_ref[...])
    k = kv[:, :D_MODEL]
    v = kv[:, D_MODEL:]
    k_ref[...] = _to_head_rows(k)
    v_ref[...] = _to_head_rows(v)
    kb_ref[...] = k.astype(BF16)
    vb_ref[...] = v.astype(BF16)


def _mem_kv(mem, g_mem, w_kv, w_up):
    rows = mem.shape[0]
    row_tiles = rows // KV_ROW_TILE
    slab = D_MODEL // row_tiles
    assert slab * row_tiles == D_MODEL
    out = jax.ShapeDtypeStruct((DEPTH, rows, HEAD_ROWS, LANES), F32)
    outb = jax.ShapeDtypeStruct((DEPTH, rows, D_MODEL), BF16)
    hspec = pl.BlockSpec((None, KV_ROW_TILE, HEAD_ROWS, LANES), lambda l, r: (l, r, 0, 0))
    ospec = pl.BlockSpec((None, KV_ROW_TILE, D_MODEL), lambda l, r: (l, r, 0))
    wspec = pl.BlockSpec((None, slab, D_FF), lambda l, r: (l, r, 0))
    return pl.pallas_call(
        _memkv_kernel,
        grid=(DEPTH, row_tiles),
        in_specs=[
            pl.BlockSpec((KV_ROW_TILE, D_MODEL), lambda l, r: (r, 0)),
            _vmem(),
            pl.BlockSpec((None, D_MODEL, 2 * D_MODEL), lambda l, r: (l, 0, 0)),
            wspec,
        ],
        out_specs=[hspec, hspec, ospec, ospec, wspec],
        out_shape=[out, out, outb, outb, jax.ShapeDtypeStruct(w_up.shape, BF16)],
        scratch_shapes=[pltpu.VMEM((D_MODEL, 2 * D_MODEL), BF16)],
        compiler_params=pltpu.CompilerParams(
            dimension_semantics=("arbitrary", "arbitrary"), vmem_limit_bytes=VMEM_LIMIT),
        name="mem_kv",
    )(mem, g_mem, w_kv, w_up)


def _pipelined_step(start, finish, carried, sample):
    step = pl.program_id(0)
    last = pl.num_programs(0) - 1

    def starts():
        halos = [carried]
        return [(start(i, halos), 0) for i in range(POOL_SUB_TILES)]

    finishes = lambda: [(finish(i), 0) for i in range(POOL_SUB_TILES)]

    @pl.when(step == 0)
    def _():
        _weave(*starts(), sample())

    @pl.when(jnp.logical_and(step > 0, step < last))
    def _():
        _weave(*finishes(), *starts(), sample())

    @pl.when(step == last)
    def _():
        _weave(*finishes())


def _started_tile(tiles_per_seq):
    step = jnp.minimum(pl.program_id(0), pl.num_programs(0) - 2)
    return lax.rem(step, tiles_per_seq)


def _pool_layer_kernel(tiles_per_seq, layer, x_ref, gm_ref, ga_ref, gf_ref, k_ref, v_ref, sq_ref,
                       sk_ref, sv_ref, wpool_ref, scale_ref, wq_ref, wo_ref, wup_ref, wdown_ref,
                       win_ref, wout_ref, y_ref, state_ref, so_ref, winb_ref, woutb_ref,
                       prev_ref, mid_ref):
    t = _started_tile(tiles_per_seq)
    tm = x_ref.shape[0]
    rows = tm // POOL_SUB_TILES
    carried = _carried(prev_ref, t)
    winb_ref[...] = win_ref[...].astype(BF16)
    woutb_ref[...] = wout_ref[...].astype(BF16)

    def start(i, halos):
        rs = slice(i * rows, (i + 1) * rows)
        x = x_ref[rs, :]
        u = _rms(x, _gain(gm_ref, layer))
        halos.append(u[rows - POOL_HALO:])
        if i == POOL_SUB_TILES - 1:
            prev_ref[...] = u[rows - POOL_HALO:]
            state_ref[...] = u[rows - POOL_STATE:]
        yield
        ext = jnp.concatenate([halos[i], u], axis=0)
        x1 = []
        for lo in range(0, rows, POOL_CHUNK):
            hi = lo + POOL_CHUNK
            pos = t * tm + i * rows + lo + lax.broadcasted_iota(jnp.int32, (POOL_CHUNK, 1), 0)
            pooled = []
            for g, w in enumerate(POOL_WINDOWS):
                sl = slice(g * POOL_GROUP, (g + 1) * POOL_GROUP)
                s = ext[lo:hi + POOL_HALO, sl]
                shift = 1
                while shift < w:
                    s = s + pltpu.roll(s, shift, axis=0)
                    shift *= 2
                inv_cnt = 1.0 / jnp.minimum(pos + 1, w).astype(F32)
                pooled.append((s[POOL_HALO:] * inv_cnt - u[lo:hi, sl]).astype(BF16))
            x1.append(x[lo:hi] + _pool_project(pooled, wpool_ref, scale_ref[...]))
            yield
        x1 = jnp.concatenate(x1, axis=0)
        (x2,) = yield from _prompt_attend([x1], _gain(ga_ref, layer), k_ref, v_ref, wq_ref, wo_ref)
        mid_ref[rs, :] = x2

    def finish(i):
        rs = slice(i * rows, (i + 1) * rows)
        x2 = mid_ref[rs, :]
        (y,) = yield from _mlp([x2], _gain(gf_ref, layer), wup_ref, wdown_ref, POOL_FF_CHUNK)
        y_ref[rs, :] = y

    _pipelined_step(start, finish, carried,
                    lambda: (_sample_attend(sq_ref, sk_ref, sv_ref, so_ref), POOL_SAMPLE_START))


def _conv_layer_kernel(tiles_per_seq, layer, x_ref, gm_ref, ga_ref, gf_ref, gfin_ref, k_ref, v_ref,
                       sq_ref, sk_ref, sv_ref, win_ref, wconv_ref, wout_ref, wq_ref, wo_ref,
                       wup_ref, wdown_ref, y_ref, state_ref, so_ref, prev_ref):
    t = lax.rem(pl.program_id(0), tiles_per_seq)
    rows = x_ref.shape[0] // SUB_TILES
    halos = [_carried(prev_ref, t)]

    def chain(i):
        rs = slice(i * rows, (i + 1) * rows)
        x = x_ref[rs, :]
        u = _rms(x, _gain(gm_ref, layer)).astype(BF16)
        yield
        gate_b = _dot(u, win_ref[:, :D_MODEL])
        yield
        gate_c = _dot(u, win_ref[:, D_MODEL:2 * D_MODEL])
        yield
        ch = gate_c * _dot(u, win_ref[:, 2 * D_MODEL:])
        halos.append(ch[rows - CONV_HALO:])
        if i == SUB_TILES - 1:
            prev_ref[...] = ch[rows - CONV_HALO:]
            state_ref[...] = ch[rows - CONV_STATE:]
        yield
        ext = jnp.concatenate([halos[i], ch], axis=0)
        conv = ext * _conv_tap(wconv_ref, 0)
        for back in range(1, CONV_WIDTH):
            conv = conv + pltpu.roll(ext, back, axis=0) * _conv_tap(wconv_ref, back)
        conv = conv[CONV_HALO:]
        x1 = x + _dot((gate_b * conv).astype(BF16), wout_ref[...])
        yield
        x2 = yield from _prompt_attend([x1], _gain(ga_ref, layer), k_ref, v_ref, wq_ref, wo_ref)
        (x3,) = yield from _mlp(x2, _gain(gf_ref, layer), wup_ref, wdown_ref, CONV_FF_CHUNK)
        y_ref[rs, :] = _rms(x3, gfin_ref[...])

    _weave(*[(chain(i), 0) for i in range(SUB_TILES)],
           (_sample_attend(sq_ref, sk_ref, sv_ref, so_ref), CONV_SAMPLE_START))


def _prompt_layer(kernel, name, layer, row_tile, vmem_limit, pipelined, x, kb, vb, sq, sk, sv,
                  small, weights, stacked, state_rows, halo, cast_along=()):
    batch, seq, _ = x.shape
    tiles_per_seq = seq // row_tile
    n_tiles = batch * tiles_per_seq
    n_sample = sq.shape[0]
    per_step = n_sample // n_tiles
    assert seq % row_tile == 0 and row_tile // SUB_TILES >= POOL_HALO
    assert per_step * n_tiles == n_sample

    started = (lambda i: jnp.minimum(i, n_tiles - 1)) if pipelined else (lambda i: i)
    finished = (lambda i: jnp.maximum(i - 1, 0)) if pipelined else (lambda i: i)
    seq_of = lambda j: lax.div(j, tiles_per_seq)
    tile_of = lambda j: lax.rem(j, tiles_per_seq)

    xspec = pl.BlockSpec((None, row_tile, D_MODEL),
                         lambda i: (seq_of(started(i)), tile_of(started(i)), 0))
    yspec = pl.BlockSpec((None, row_tile, D_MODEL),
                         lambda i: (seq_of(finished(i)), tile_of(finished(i)), 0))
    stspec = pl.BlockSpec((None, state_rows, D_MODEL), lambda i: (seq_of(started(i)), 0, 0))
    kvspec = pl.BlockSpec((None, None, N_MEM, D_MODEL),
                          lambda i: (layer, seq_of(started(i)), 0, 0))
    sqspec = pl.BlockSpec((per_step, HEAD_ROWS, LANES), lambda i: (started(i), 0, 0))
    skvspec = pl.BlockSpec((None, per_step, N_MEM, HEAD_ROWS, LANES),
                           lambda i: (layer, started(i), 0, 0, 0))
    def slab_of(a, layer_of_a):
        return pl.BlockSpec((None, a.shape[1] // n_tiles, a.shape[2]),
                            lambda i: (layer_of_a, started(i), 0))

    slabs_in = [slab_of(a, la) for a, la in cast_along]
    slabs = [pl.BlockSpec((a.shape[1] // n_tiles, a.shape[2]), lambda i: (started(i), 0))
             for a, _ in cast_along]
    scratch = [pltpu.VMEM((halo, D_MODEL), F32)]
    if pipelined:
        scratch.append(pltpu.VMEM((row_tile, D_MODEL), F32))
    return pl.pallas_call(
        functools.partial(kernel, tiles_per_seq, layer),
        grid=(n_tiles + 1 if pipelined else n_tiles,),
        in_specs=([xspec] + [_vmem()] * len(small) + [kvspec, kvspec, sqspec, skvspec, skvspec]
                  + [_vmem()] * len(weights) + [_of_layer(w, layer) for w in stacked] + slabs_in),
        out_specs=[yspec, stspec, sqspec] + slabs,
        out_shape=[jax.ShapeDtypeStruct(x.shape, F32),
                   jax.ShapeDtypeStruct((batch, state_rows, D_MODEL), F32),
                   jax.ShapeDtypeStruct(sq.shape, F32)]
                  + [jax.ShapeDtypeStruct(a.shape[1:], BF16) for a, _ in cast_along],
        scratch_shapes=scratch,
        compiler_params=pltpu.CompilerParams(
            dimension_semantics=("arbitrary",), vmem_limit_bytes=vmem_limit),
        name=name,
    )(x, *small, kb, vb, sq, sk, sv, *weights, *stacked, *[a for a, _ in cast_along])


def _sample_query(x, g_attn, wq_ref):
    q = _dot(_rms(x, g_attn).astype(BF16), wq_ref[...]) * (MEM_HEAD_DIM ** -0.5)
    return _to_head_rows(q)


def _sample_pool_pre_kernel(layer, x_ref, st_ref, gm_ref, ga_ref, wpool_ref, scale_ref, wq_ref,
                            x1_ref, q_ref, newst_ref):
    x = x_ref[:, 0, :]
    u = _rms(x, _gain(gm_ref, layer))
    pooled = []
    for g, w in enumerate(POOL_WINDOWS):
        s = u[:, g * POOL_GROUP:(g + 1) * POOL_GROUP]
        for back in range(1, w):
            s = s + st_ref[POOL_STATE - back, :, g * POOL_GROUP:(g + 1) * POOL_GROUP]
        cnt = min(PAST_LEN + 1, w)
        pooled.append((s * (1.0 / cnt) - u[:, g * POOL_GROUP:(g + 1) * POOL_GROUP]).astype(BF16))
    x = x + _pool_project(pooled, wpool_ref, scale_ref[...])
    x1_ref[...] = x
    q_ref[...] = _sample_query(x, _gain(ga_ref, layer), wq_ref)
    newst_ref[:POOL_STATE - 1] = st_ref[1:]
    newst_ref[POOL_STATE - 1] = u


def _sample_conv_pre_kernel(layer, x_ref, st_ref, gm_ref, ga_ref, win_ref, wconv_ref, wout_ref,
                            wq_ref, x1_ref, q_ref, newst_ref):
    x = x_ref[...]
    u = _rms(x, _gain(gm_ref, layer)).astype(BF16)
    gate_b = _dot(u, win_ref[:, :D_MODEL])
    ch = _dot(u, win_ref[:, D_MODEL:2 * D_MODEL]) * _dot(u, win_ref[:, 2 * D_MODEL:])
    conv = ch * _conv_tap(wconv_ref, 0)
    for back in range(1, CONV_WIDTH):
        conv = conv + st_ref[:, CONV_STATE - back, :] * _conv_tap(wconv_ref, back)
    x = x + _dot((gate_b * conv).astype(BF16), wout_ref[...])
    x1_ref[...] = x
    q_ref[...] = _sample_query(x, _gain(ga_ref, layer), wq_ref)
    for age in range(CONV_STATE - 1):
        newst_ref[:, age, :] = st_ref[:, age + 1, :]
    newst_ref[:, CONV_STATE - 1, :] = ch


def _sample_pre(kernel, name, layer, x, state, small, weights, w_q):
    n = x.shape[0]
    return pl.pallas_call(
        functools.partial(kernel, layer),
        grid=(1,),
        in_specs=[_vmem()] * (2 + len(small) + len(weights)) + [_of_layer(w_q, layer)],
        out_specs=[_vmem()] * 3,
        out_shape=[jax.ShapeDtypeStruct((n, D_MODEL), F32),
                   jax.ShapeDtypeStruct((n, HEAD_ROWS, LANES), F32),
                   jax.ShapeDtypeStruct(state.shape, F32)],
        compiler_params=pltpu.CompilerParams(
            dimension_semantics=("arbitrary",), vmem_limit_bytes=VMEM_LIMIT),
        name=name,
    )(x, state, *small, *weights, w_q)


def _sample_post_kernel(final, layer, x_ref, o_ref, gf_ref, gfin_ref, wo_ref, wup_ref, wdown_ref,
                        *rest):
    if len(rest) == 5:
        wnext_ref, y_ref, wnextb_ref, u_ref, acc_ref = rest
        wnextb_ref[...] = wnext_ref[...].astype(BF16)
    else:
        y_ref, u_ref, acc_ref = rest
    c = pl.program_id(0)

    @pl.when(c == 0)
    def _():
        x = x_ref[...] + _dot(_from_head_rows(o_ref[...]).astype(BF16), wo_ref[...])
        u_ref[...] = _rms(x, _gain(gf_ref, layer)).astype(BF16)
        acc_ref[...] = x

    h = jnp.maximum(_dot(u_ref[...], wup_ref[...]), 0.0)
    acc_ref[...] += _dot((h * h).astype(BF16), wdown_ref[...])

    @pl.when(c == pl.num_programs(0) - 1)
    def _():
        x = acc_ref[...]
        if final:
            y_ref[:, 0, :] = _rms(x, gfin_ref[...])
        else:
            y_ref[...] = x


def _sample_post(x1, o, g_ffn, g_final, wo, wup, wdown, layer, final, name, cast_next=None):
    n = x1.shape[0]
    whole = lambda shape: pl.BlockSpec(shape, lambda c: (0,) * len(shape))
    act = whole((n, D_MODEL))
    chunk_rows = pl.BlockSpec((FF_CHUNK, D_MODEL), lambda c: (c, 0))
    y_shape = (n, 1, D_MODEL) if final else (n, D_MODEL)
    in_specs = [act, whole((n, HEAD_ROWS, LANES)), whole(g_ffn.shape), whole(g_final.shape),
                pl.BlockSpec((None, D_MODEL, D_MODEL), lambda c: (layer, 0, 0)),
                pl.BlockSpec((None, D_MODEL, FF_CHUNK), lambda c: (layer, 0, c)),
                chunk_rows]
    operands = [x1, o, g_ffn, g_final, wo, wup, wdown]
    out_specs = [whole(y_shape)]
    out_shape = [jax.ShapeDtypeStruct(y_shape, F32)]
    if cast_next is not None:
        stack, next_layer = cast_next
        in_specs.append(pl.BlockSpec((None, FF_CHUNK, D_MODEL), lambda c: (next_layer, c, 0)))
        operands.append(stack)
        out_specs.append(chunk_rows)
        out_shape.append(jax.ShapeDtypeStruct(stack.shape[1:], BF16))
    return pl.pallas_call(
        functools.partial(_sample_post_kernel, final, layer),
        grid=(D_FF // FF_CHUNK,),
        in_specs=in_specs,
        out_specs=out_specs,
        out_shape=out_shape,
        scratch_shapes=[pltpu.VMEM((n, D_MODEL), BF16), pltpu.VMEM((n, D_MODEL), F32)],
        compiler_params=pltpu.CompilerParams(
            dimension_semantics=("arbitrary",), vmem_limit_bytes=VMEM_LIMIT),
        name=name,
    )(*operands)


def kernel(x_prompt, x_sample, state_pool, state_conv, cache_mem_k, cache_mem_v, mem_prompt,
           g_mix, g_attn, g_mem, g_ffn, g_final, w_pool, pool_scale,
           w_conv_in, w_conv, w_conv_out, w_q, w_kv, w_o, w_up, w_down):
    batch, seq, _ = x_prompt.shape
    n_sample = x_sample.shape[0]
    assert DEPTH == 2 and x_sample.shape[1] == 1
    assert (batch * N_MEM) % KV_ROW_TILE == 0

    cast = lambda a: a.astype(BF16)
    wpool_b = cast(w_pool[0])
    wq_b, wo_b, wdown0_b = cast(w_q), cast(w_o), cast(w_down[0])
    scale0, gfin = pool_scale[0].reshape(1, D_MODEL), g_final.reshape(1, D_MODEL)
    wconv0 = w_conv.transpose(1, 0, 2)

    k_all, v_all, kb_all, vb_all, wup_b = _mem_kv(
        mem_prompt.reshape(batch * N_MEM, D_MODEL), g_mem, w_kv, w_up)
    kb_all = kb_all.reshape(DEPTH, batch, N_MEM, D_MODEL)
    vb_all = vb_all.reshape(DEPTH, batch, N_MEM, D_MODEL)
    ck, cv = _cache_rows(cache_mem_k), _cache_rows(cache_mem_v)

    xs1, q0, pool_s = _sample_pre(
        _sample_pool_pre_kernel, "sample_pool_pre", 0, x_sample,
        state_pool[0].transpose(1, 0, 2),
        [g_mix, g_attn], [wpool_b, scale0], wq_b)
    x1, pool_p, o0, win_b, wout_b = _prompt_layer(
        _pool_layer_kernel, "prompt_pool_layer", 0, POOL_ROW_TILE, POOL_VMEM_LIMIT, True,
        x_prompt, kb_all, vb_all, q0, ck, cv, [g_mix, g_attn, g_ffn],
        [wpool_b, scale0], [wq_b, wo_b, wup_b, wdown0_b], POOL_STATE, POOL_HALO,
        cast_along=[(w_conv_in, 0), (w_conv_out, 0)])
    xs2, wdown1_b = _sample_post(xs1, o0, g_ffn, gfin, wo_b, wup_b, wdown0_b, 0, False,
                                 "sample_post0", cast_next=(w_down, 1))

    xs3, q1, conv_s = _sample_pre(
        _sample_conv_pre_kernel, "sample_conv_pre", 1, xs2,
        state_conv[0],
        [g_mix, g_attn], [win_b, wconv0, wout_b], wq_b)
    y_prompt, conv_p, o1 = _prompt_layer(
        _conv_layer_kernel, "prompt_conv_layer", 1, CONV_ROW_TILE, CONV_VMEM_LIMIT, False,
        x1, kb_all, vb_all, q1, ck, cv, [g_mix, g_attn, g_ffn, gfin],
        [win_b, wconv0, wout_b], [wq_b, wo_b, wup_b, wdown1_b], CONV_STATE, CONV_HALO)
    (y_sample,) = _sample_post(xs3, o1, g_ffn, gfin, wo_b, wup_b, wdown1_b, 1, True, "sample_post1")

    mem_shape = (DEPTH, batch, N_MEM, MEM_HEADS, MEM_HEAD_DIM)
    return (y_prompt,
            y_sample,
            pool_p.reshape(1, batch, POOL_STATE, D_MODEL),
            conv_p.reshape(1, batch, CONV_STATE, D_MODEL),
            _unhead_rows(k_all.reshape(-1, HEAD_ROWS, LANES)).reshape(mem_shape),
            _unhead_rows(v_all.reshape(-1, HEAD_ROWS, LANES)).reshape(mem_shape),
            pool_s.transpose(1, 0, 2)[None],
            conv_s[None])
```

```python
import functools

import jax
import jax.numpy as jnp
from jax import lax
from jax.experimental import pallas as pl
from jax.experimental.pallas import tpu as pltpu

D_MODEL = 1024
DEPTH = 2
PAST_LEN = 16384
POOL_WINDOWS = (2, 4, 8, 16)
POOL_GROUP = D_MODEL // len(POOL_WINDOWS)
POOL_STATE = max(POOL_WINDOWS) - 1
CONV_WIDTH = 3
CONV_STATE = CONV_WIDTH - 1
D_FF = 4 * D_MODEL
N_MEM = 256
MEM_HEADS = 4
MEM_HEAD_DIM = D_MODEL // MEM_HEADS
EPS = 1e-6

SUBLANES = 8
LANES = 128
HEAD_ROWS = D_MODEL // LANES
POOL_HALO = 16
CONV_HALO = SUBLANES
POOL_ROW_TILE = 512
CONV_ROW_TILE = 512
SUB_TILES = 2
POOL_SUB_TILES = 1
KV_ROW_TILE = 512
FF_CHUNK = 1024
POOL_FF_CHUNK = 512
CONV_FF_CHUNK = 1024
POOL_SAMPLE_START = 4
POOL_CHUNK = 64
CONV_SAMPLE_START = 2
VMEM_LIMIT = 56 * 1024 * 1024
POOL_VMEM_LIMIT = 62 * 1024 * 1024
CONV_VMEM_LIMIT = 62 * 1024 * 1024

F32 = jnp.float32
BF16 = jnp.bfloat16

_dot = functools.partial(jnp.dot, preferred_element_type=F32)


def _rms(x, g):
    y = x * lax.rsqrt(jnp.mean(x * x, axis=-1, keepdims=True) + EPS)
    return y * g


def _vmem():
    return pl.BlockSpec(memory_space=pltpu.VMEM)


def _of_layer(w, layer):
    if w.ndim == 2:
        return pl.BlockSpec(w.shape, lambda i: (0, 0), pipeline_mode=pl.Buffered(1))
    return pl.BlockSpec((None, *w.shape[1:]), lambda i: (layer, 0, 0),
                        pipeline_mode=pl.Buffered(1))


def _gain(g_ref, layer):
    return g_ref[layer:layer + 1, :]


def _weave(*delayed):
    live = list(delayed)
    round_ = 0
    while live:
        for item in list(live):
            stages, first_round = item
            if round_ >= first_round:
                try:
                    next(stages)
                except StopIteration:
                    live.remove(item)
        round_ += 1


def _mlp(xs, g_ffn, wup_ref, wdown_ref, chunk=FF_CHUNK):
    us = [_rms(x, g_ffn).astype(BF16) for x in xs]
    yield
    accs = list(xs)
    for c in range(D_FF // chunk):
        sl = slice(c * chunk, (c + 1) * chunk)
        hs = [jnp.maximum(_dot(u, wup_ref[:, sl]), 0.0) for u in us]
        yield
        accs = [a + _dot((h * h).astype(BF16), wdown_ref[sl, :]) for a, h in zip(accs, hs)]
        yield
    return accs


def _prompt_attend(xs, g_attn, k_ref, v_ref, wq_ref, wo_ref):
    us = [_rms(x, g_attn).astype(BF16) for x in xs]
    yield
    qs = [_dot(u, wq_ref[...]) * (MEM_HEAD_DIM ** -0.5) for u in us]
    yield
    heads = [[] for _ in xs]
    for h in range(MEM_HEADS):
        sl = slice(h * MEM_HEAD_DIM, (h + 1) * MEM_HEAD_DIM)
        ss = [_dot(q[:, sl].astype(BF16), k_ref[sl, :]) for q in qs]
        es = [jnp.exp(s - jnp.max(s, axis=-1, keepdims=True)) for s in ss]
        for i, e in enumerate(es):
            inv = 1.0 / jnp.sum(e, axis=-1, keepdims=True)
            heads[i].append((_dot(e.astype(BF16), v_ref[:, sl]) * inv).astype(BF16))
        yield
    os_ = [jnp.concatenate(hd, axis=-1) for hd in heads]
    return [x + _dot(o, wo_ref[...]) for x, o in zip(xs, os_)]


def _sample_attend(q_ref, k_ref, v_ref, o_ref):
    for j in range(q_ref.shape[0]):
        prod = k_ref[j] * q_ref[j][None]
        part = jnp.sum(prod, axis=-1, keepdims=True)
        s = part + pltpu.roll(part, MEM_HEADS, axis=1)
        e = jnp.exp(s - jnp.max(s, axis=0, keepdims=True))
        inv = 1.0 / jnp.sum(e, axis=0)
        yield
        o_ref[j] = jnp.sum(e * v_ref[j], axis=0) * inv
        yield


def _conv_tap(wconv_ref, back):
    return wconv_ref[CONV_WIDTH - 1 - back]


def _pool_project(pooled, wpool_ref, scale):
    ys = [_dot(p, wpool_ref[g]) for g, p in enumerate(pooled)]
    return jnp.concatenate(ys, axis=-1) * scale


def _carried(prev_ref, t):
    @pl.when(t == 0)
    def _():
        prev_ref[...] = jnp.zeros_like(prev_ref)

    return prev_ref[...]


def _to_head_rows(a):
    tiles = [a[:, (c % MEM_HEADS) * MEM_HEAD_DIM + (c // MEM_HEADS) * LANES:][:, :LANES]
             for c in range(HEAD_ROWS)]
    return jnp.concatenate(tiles, axis=-1).reshape(a.shape[0], HEAD_ROWS, LANES)


def _from_head_rows(a):
    flat = a.reshape(a.shape[0], D_MODEL)
    per_head = MEM_HEAD_DIM // LANES
    tiles = [flat[:, ((c % per_head) * MEM_HEADS + c // per_head) * LANES:][:, :LANES]
             for c in range(HEAD_ROWS)]
    return jnp.concatenate(tiles, axis=-1)


def _unhead_rows(a):
    n = a.shape[0]
    a = a.reshape(n, MEM_HEAD_DIM // LANES, MEM_HEADS, LANES)
    return a.transpose(0, 2, 1, 3).reshape(n, D_MODEL)


def _cache_rows(c):
    depth, n = c.shape[:2]
    c = c.reshape(depth, n, N_MEM, MEM_HEADS, MEM_HEAD_DIM // LANES, LANES)
    return c.transpose(0, 1, 2, 4, 3, 5).reshape(depth, n, N_MEM, HEAD_ROWS, LANES)


def _memkv_kernel(m_ref, g_ref, w_ref, wup_ref, k_ref, v_ref, kb_ref, vb_ref, wupb_ref, wb_ref):
    @pl.when(pl.program_id(1) == 0)
    def _():
        wb_ref[...] = w_ref[...].astype(BF16)

    wupb_ref[...] = wup_ref[...].astype(BF16)

    u = _rms(m_ref[...], g_ref[pl.ds(pl.program_id(0), 1), :]).astype(BF16)
    kv = _dot(u, wb_ref[...])
    k = kv[:, :D_MODEL]
    v = kv[:, D_MODEL:]
    k_ref[...] = _to_head_rows(k)
    v_ref[...] = _to_head_rows(v)
    for s in range(kb_ref.shape[0]):
        kb_ref[s] = k[s * N_MEM:(s + 1) * N_MEM, :].T.astype(BF16)
    vb_ref[...] = v.astype(BF16)


def _mem_kv(mem, g_mem, w_kv, w_up):
    rows = mem.shape[0]
    row_tiles = rows // KV_ROW_TILE
    slab = D_MODEL // row_tiles
    assert slab * row_tiles == D_MODEL
    out = jax.ShapeDtypeStruct((DEPTH, rows, HEAD_ROWS, LANES), F32)
    outb = jax.ShapeDtypeStruct((DEPTH, rows, D_MODEL), BF16)
    hspec = pl.BlockSpec((None, KV_ROW_TILE, HEAD_ROWS, LANES), lambda l, r: (l, r, 0, 0))
    ospec = pl.BlockSpec((None, KV_ROW_TILE, D_MODEL), lambda l, r: (l, r, 0))
    seqs = KV_ROW_TILE // N_MEM
    outt = jax.ShapeDtypeStruct((DEPTH, rows // N_MEM, D_MODEL, N_MEM), BF16)
    tspec = pl.BlockSpec((None, seqs, D_MODEL, N_MEM), lambda l, r: (l, r, 0, 0))
    wspec = pl.BlockSpec((None, slab, D_FF), lambda l, r: (l, r, 0))
    return pl.pallas_call(
        _memkv_kernel,
        grid=(DEPTH, row_tiles),
        in_specs=[
            pl.BlockSpec((KV_ROW_TILE, D_MODEL), lambda l, r: (r, 0)),
            _vmem(),
            pl.BlockSpec((None, D_MODEL, 2 * D_MODEL), lambda l, r: (l, 0, 0)),
            wspec,
        ],
        out_specs=[hspec, hspec, tspec, ospec, wspec],
        out_shape=[out, out, outt, outb, jax.ShapeDtypeStruct(w_up.shape, BF16)],
        scratch_shapes=[pltpu.VMEM((D_MODEL, 2 * D_MODEL), BF16)],
        compiler_params=pltpu.CompilerParams(
            dimension_semantics=("arbitrary", "arbitrary"), vmem_limit_bytes=VMEM_LIMIT),
        name="mem_kv",
    )(mem, g_mem, w_kv, w_up)


def _pipelined_step(start, finish, carried, sample):
    step = pl.program_id(0)
    last = pl.num_programs(0) - 1

    def starts():
        halos = [carried]
        return [(start(i, halos), 0) for i in range(POOL_SUB_TILES)]

    finishes = lambda: [(finish(i), 0) for i in range(POOL_SUB_TILES)]

    @pl.when(step == 0)
    def _():
        _weave(*starts(), sample())

    @pl.when(jnp.logical_and(step > 0, step < last))
    def _():
        _weave(*finishes(), *starts(), sample())

    @pl.when(step == last)
    def _():
        _weave(*finishes())


def _started_tile(tiles_per_seq):
    step = jnp.minimum(pl.program_id(0), pl.num_programs(0) - 2)
    return lax.rem(step, tiles_per_seq)


def _pool_layer_kernel(tiles_per_seq, layer, x_ref, gm_ref, ga_ref, gf_ref, k_ref, v_ref, sq_ref,
                       sk_ref, sv_ref, wpool_ref, scale_ref, wq_ref, wo_ref, wup_ref, wdown_ref,
                       win_ref, wout_ref, y_ref, state_ref, so_ref, winb_ref, woutb_ref,
                       prev_ref, mid_ref):
    t = _started_tile(tiles_per_seq)
    tm = x_ref.shape[0]
    rows = tm // POOL_SUB_TILES
    carried = _carried(prev_ref, t)
    winb_ref[...] = win_ref[...].astype(BF16)
    woutb_ref[...] = wout_ref[...].astype(BF16)

    def start(i, halos):
        rs = slice(i * rows, (i + 1) * rows)
        x = x_ref[rs, :]
        u = _rms(x, _gain(gm_ref, layer))
        halos.append(u[rows - POOL_HALO:])
        if i == POOL_SUB_TILES - 1:
            prev_ref[...] = u[rows - POOL_HALO:]
            state_ref[...] = u[rows - POOL_STATE:]
        yield
        ext = jnp.concatenate([halos[i], u], axis=0)
        x1 = []
        for lo in range(0, rows, POOL_CHUNK):
            hi = lo + POOL_CHUNK
            pos = t * tm + i * rows + lo + lax.broadcasted_iota(jnp.int32, (POOL_CHUNK, 1), 0)
            pooled = []
            for g, w in enumerate(POOL_WINDOWS):
                sl = slice(g * POOL_GROUP, (g + 1) * POOL_GROUP)
                s = ext[lo:hi + POOL_HALO, sl]
                shift = 1
                while shift < w:
                    s = s + pltpu.roll(s, shift, axis=0)
                    shift *= 2
                inv_cnt = 1.0 / jnp.minimum(pos + 1, w).astype(F32)
                pooled.append((s[POOL_HALO:] * inv_cnt - u[lo:hi, sl]).astype(BF16))
            x1.append(x[lo:hi] + _pool_project(pooled, wpool_ref, scale_ref[...]))
            yield
        x1 = jnp.concatenate(x1, axis=0)
        (x2,) = yield from _prompt_attend([x1], _gain(ga_ref, layer), k_ref, v_ref, wq_ref, wo_ref)
        mid_ref[rs, :] = x2

    def finish(i):
        rs = slice(i * rows, (i + 1) * rows)
        x2 = mid_ref[rs, :]
        (y,) = yield from _mlp([x2], _gain(gf_ref, layer), wup_ref, wdown_ref, POOL_FF_CHUNK)
        y_ref[rs, :] = y

    _pipelined_step(start, finish, carried,
                    lambda: (_sample_attend(sq_ref, sk_ref, sv_ref, so_ref), POOL_SAMPLE_START))


def _conv_layer_kernel(tiles_per_seq, layer, x_ref, gm_ref, ga_ref, gf_ref, gfin_ref, k_ref, v_ref,
                       sq_ref, sk_ref, sv_ref, win_ref, wconv_ref, wout_ref, wq_ref, wo_ref,
                       wup_ref, wdown_ref, y_ref, state_ref, so_ref, prev_ref):
    t = lax.rem(pl.program_id(0), tiles_per_seq)
    rows = x_ref.shape[0] // SUB_TILES
    halos = [_carried(prev_ref, t)]

    def chain(i):
        rs = slice(i * rows, (i + 1) * rows)
        x = x_ref[rs, :]
        u = _rms(x, _gain(gm_ref, layer)).astype(BF16)
        yield
        gate_b = _dot(u, win_ref[:, :D_MODEL])
        yield
        gate_c = _dot(u, win_ref[:, D_MODEL:2 * D_MODEL])
        yield
        ch = gate_c * _dot(u, win_ref[:, 2 * D_MODEL:])
        halos.append(ch[rows - CONV_HALO:])
        if i == SUB_TILES - 1:
            prev_ref[...] = ch[rows - CONV_HALO:]
            state_ref[...] = ch[rows - CONV_STATE:]
        yield
        ext = jnp.concatenate([halos[i], ch], axis=0)
        conv = ext * _conv_tap(wconv_ref, 0)
        for back in range(1, CONV_WIDTH):
            conv = conv + pltpu.roll(ext, back, axis=0) * _conv_tap(wconv_ref, back)
        conv = conv[CONV_HALO:]
        x1 = x + _dot((gate_b * conv).astype(BF16), wout_ref[...])
        yield
        x2 = yield from _prompt_attend([x1], _gain(ga_ref, layer), k_ref, v_ref, wq_ref, wo_ref)
        (x3,) = yield from _mlp(x2, _gain(gf_ref, layer), wup_ref, wdown_ref, CONV_FF_CHUNK)
        y_ref[rs, :] = _rms(x3, gfin_ref[...])

    _weave(*[(chain(i), 0) for i in range(SUB_TILES)],
           (_sample_attend(sq_ref, sk_ref, sv_ref, so_ref), CONV_SAMPLE_START))


def _prompt_layer(kernel, name, layer, row_tile, vmem_limit, pipelined, x, kb, vb, sq, sk, sv,
                  small, weights, stacked, state_rows, halo, cast_along=()):
    batch, seq, _ = x.shape
    tiles_per_seq = seq // row_tile
    n_tiles = batch * tiles_per_seq
    n_sample = sq.shape[0]
    per_step = n_sample // n_tiles
    assert seq % row_tile == 0 and row_tile // SUB_TILES >= POOL_HALO
    assert per_step * n_tiles == n_sample

    started = (lambda i: jnp.minimum(i, n_tiles - 1)) if pipelined else (lambda i: i)
    finished = (lambda i: jnp.maximum(i - 1, 0)) if pipelined else (lambda i: i)
    seq_of = lambda j: lax.div(j, tiles_per_seq)
    tile_of = lambda j: lax.rem(j, tiles_per_seq)

    xspec = pl.BlockSpec((None, row_tile, D_MODEL),
                         lambda i: (seq_of(started(i)), tile_of(started(i)), 0))
    yspec = pl.BlockSpec((None, row_tile, D_MODEL),
                         lambda i: (seq_of(finished(i)), tile_of(finished(i)), 0))
    stspec = pl.BlockSpec((None, state_rows, D_MODEL), lambda i: (seq_of(started(i)), 0, 0))
    kvspec = pl.BlockSpec((None, None, N_MEM, D_MODEL),
                          lambda i: (layer, seq_of(started(i)), 0, 0))
    ktspec = pl.BlockSpec((None, None, D_MODEL, N_MEM),
                          lambda i: (layer, seq_of(started(i)), 0, 0))
    sqspec = pl.BlockSpec((per_step, HEAD_ROWS, LANES), lambda i: (started(i), 0, 0))
    skvspec = pl.BlockSpec((None, per_step, N_MEM, HEAD_ROWS, LANES),
                           lambda i: (layer, started(i), 0, 0, 0))
    def slab_of(a, layer_of_a):
        return pl.BlockSpec((None, a.shape[1] // n_tiles, a.shape[2]),
                            lambda i: (layer_of_a, started(i), 0))

    slabs_in = [slab_of(a, la) for a, la in cast_along]
    slabs = [pl.BlockSpec((a.shape[1] // n_tiles, a.shape[2]), lambda i: (started(i), 0))
             for a, _ in cast_along]
    scratch = [pltpu.VMEM((halo, D_MODEL), F32)]
    if pipelined:
        scratch.append(pltpu.VMEM((row_tile, D_MODEL), F32))
    return pl.pallas_call(
        functools.partial(kernel, tiles_per_seq, layer),
        grid=(n_tiles + 1 if pipelined else n_tiles,),
        in_specs=([xspec] + [_vmem()] * len(small) + [ktspec, kvspec, sqspec, skvspec, skvspec]
                  + [_vmem()] * len(weights) + [_of_layer(w, layer) for w in stacked] + slabs_in),
        out_specs=[yspec, stspec, sqspec] + slabs,
        out_shape=[jax.ShapeDtypeStruct(x.shape, F32),
                   jax.ShapeDtypeStruct((batch, state_rows, D_MODEL), F32),
                   jax.ShapeDtypeStruct(sq.shape, F32)]
                  + [jax.ShapeDtypeStruct(a.shape[1:], BF16) for a, _ in cast_along],
        scratch_shapes=scratch,
        compiler_params=pltpu.CompilerParams(
            dimension_semantics=("arbitrary",), vmem_limit_bytes=vmem_limit),
        name=name,
    )(x, *small, kb, vb, sq, sk, sv, *weights, *stacked, *[a for a, _ in cast_along])


def _sample_query(x, g_attn, wq_ref):
    q = _dot(_rms(x, g_attn).astype(BF16), wq_ref[...]) * (MEM_HEAD_DIM ** -0.5)
    return _to_head_rows(q)


def _sample_pool_pre_kernel(layer, x_ref, st_ref, gm_ref, ga_ref, wpool_ref, scale_ref, wq_ref,
                            x1_ref, q_ref, newst_ref):
    x = x_ref[:, 0, :]
    u = _rms(x, _gain(gm_ref, layer))
    pooled = []
    for g, w in enumerate(POOL_WINDOWS):
        s = u[:, g * POOL_GROUP:(g + 1) * POOL_GROUP]
        for back in range(1, w):
            s = s + st_ref[POOL_STATE - back, :, g * POOL_GROUP:(g + 1) * POOL_GROUP]
        cnt = min(PAST_LEN + 1, w)
        pooled.append((s * (1.0 / cnt) - u[:, g * POOL_GROUP:(g + 1) * POOL_GROUP]).astype(BF16))
    x = x + _pool_project(pooled, wpool_ref, scale_ref[...])
    x1_ref[...] = x
    q_ref[...] = _sample_query(x, _gain(ga_ref, layer), wq_ref)
    newst_ref[:POOL_STATE - 1] = st_ref[1:]
    newst_ref[POOL_STATE - 1] = u


def _sample_conv_pre_kernel(layer, x_ref, st_ref, gm_ref, ga_ref, win_ref, wconv_ref, wout_ref,
                            wq_ref, x1_ref, q_ref, newst_ref):
    x = x_ref[...]
    u = _rms(x, _gain(gm_ref, layer)).astype(BF16)
    gate_b = _dot(u, win_ref[:, :D_MODEL])
    ch = _dot(u, win_ref[:, D_MODEL:2 * D_MODEL]) * _dot(u, win_ref[:, 2 * D_MODEL:])
    conv = ch * _conv_tap(wconv_ref, 0)
    for back in range(1, CONV_WIDTH):
        conv = conv + st_ref[:, CONV_STATE - back, :] * _conv_tap(wconv_ref, back)
    x = x + _dot((gate_b * conv).astype(BF16), wout_ref[...])
    x1_ref[...] = x
    q_ref[...] = _sample_query(x, _gain(ga_ref, layer), wq_ref)
    for age in range(CONV_STATE - 1):
        newst_ref[:, age, :] = st_ref[:, age + 1, :]
    newst_ref[:, CONV_STATE - 1, :] = ch


def _sample_pre(kernel, name, layer, x, state, small, weights, w_q):
    n = x.shape[0]
    return pl.pallas_call(
        functools.partial(kernel, layer),
        grid=(1,),
        in_specs=[_vmem()] * (2 + len(small) + len(weights)) + [_of_layer(w_q, layer)],
        out_specs=[_vmem()] * 3,
        out_shape=[jax.ShapeDtypeStruct((n, D_MODEL), F32),
                   jax.ShapeDtypeStruct((n, HEAD_ROWS, LANES), F32),
                   jax.ShapeDtypeStruct(state.shape, F32)],
        compiler_params=pltpu.CompilerParams(
            dimension_semantics=("arbitrary",), vmem_limit_bytes=VMEM_LIMIT),
        name=name,
    )(x, state, *small, *weights, w_q)


def _sample_post_kernel(final, layer, x_ref, o_ref, gf_ref, gfin_ref, wo_ref, wup_ref, wdown_ref,
                        *rest):
    if len(rest) == 5:
        wnext_ref, y_ref, wnextb_ref, u_ref, acc_ref = rest
        wnextb_ref[...] = wnext_ref[...].astype(BF16)
    else:
        y_ref, u_ref, acc_ref = rest
    c = pl.program_id(0)

    @pl.when(c == 0)
    def _():
        x = x_ref[...] + _dot(_from_head_rows(o_ref[...]).astype(BF16), wo_ref[...])
        u_ref[...] = _rms(x, _gain(gf_ref, layer)).astype(BF16)
        acc_ref[...] = x

    h = jnp.maximum(_dot(u_ref[...], wup_ref[...]), 0.0)
    acc_ref[...] += _dot((h * h).astype(BF16), wdown_ref[...])

    @pl.when(c == pl.num_programs(0) - 1)
    def _():
        x = acc_ref[...]
        if final:
            y_ref[:, 0, :] = _rms(x, gfin_ref[...])
        else:
            y_ref[...] = x


def _sample_post(x1, o, g_ffn, g_final, wo, wup, wdown, layer, final, name, cast_next=None):
    n = x1.shape[0]
    whole = lambda shape: pl.BlockSpec(shape, lambda c: (0,) * len(shape))
    act = whole((n, D_MODEL))
    chunk_rows = pl.BlockSpec((FF_CHUNK, D_MODEL), lambda c: (c, 0))
    y_shape = (n, 1, D_MODEL) if final else (n, D_MODEL)
    in_specs = [act, whole((n, HEAD_ROWS, LANES)), whole(g_ffn.shape), whole(g_final.shape),
                pl.BlockSpec((None, D_MODEL, D_MODEL), lambda c: (layer, 0, 0)),
                pl.BlockSpec((None, D_MODEL, FF_CHUNK), lambda c: (layer, 0, c)),
                chunk_rows]
    operands = [x1, o, g_ffn, g_final, wo, wup, wdown]
    out_specs = [whole(y_shape)]
    out_shape = [jax.ShapeDtypeStruct(y_shape, F32)]
    if cast_next is not None:
        stack, next_layer = cast_next
        in_specs.append(pl.BlockSpec((None, FF_CHUNK, D_MODEL), lambda c: (next_layer, c, 0)))
        operands.append(stack)
        out_specs.append(chunk_rows)
        out_shape.append(jax.ShapeDtypeStruct(stack.shape[1:], BF16))
    return pl.pallas_call(
        functools.partial(_sample_post_kernel, final, layer),
        grid=(D_FF // FF_CHUNK,),
        in_specs=in_specs,
        out_specs=out_specs,
        out_shape=out_shape,
        scratch_shapes=[pltpu.VMEM((n, D_MODEL), BF16), pltpu.VMEM((n, D_MODEL), F32)],
        compiler_params=pltpu.CompilerParams(
            dimension_semantics=("arbitrary",), vmem_limit_bytes=VMEM_LIMIT),
        name=name,
    )(*operands)


def kernel(x_prompt, x_sample, state_pool, state_conv, cache_mem_k, cache_mem_v, mem_prompt,
           g_mix, g_attn, g_mem, g_ffn, g_final, w_pool, pool_scale,
           w_conv_in, w_conv, w_conv_out, w_q, w_kv, w_o, w_up, w_down):
    batch, seq, _ = x_prompt.shape
    n_sample = x_sample.shape[0]
    assert DEPTH == 2 and x_sample.shape[1] == 1
    assert (batch * N_MEM) % KV_ROW_TILE == 0

    cast = lambda a: a.astype(BF16)
    wpool_b = cast(w_pool[0])
    wq_b, wo_b, wdown0_b = cast(w_q), cast(w_o), cast(w_down[0])
    scale0, gfin = pool_scale[0].reshape(1, D_MODEL), g_final.reshape(1, D_MODEL)
    wconv0 = w_conv.transpose(1, 0, 2)

    k_all, v_all, kb_all, vb_all, wup_b = _mem_kv(
        mem_prompt.reshape(batch * N_MEM, D_MODEL), g_mem, w_kv, w_up)
    vb_all = vb_all.reshape(DEPTH, batch, N_MEM, D_MODEL)
    ck, cv = _cache_rows(cache_mem_k), _cache_rows(cache_mem_v)

    xs1, q0, pool_s = _sample_pre(
        _sample_pool_pre_kernel, "sample_pool_pre", 0, x_sample,
        state_pool[0].transpose(1, 0, 2),
        [g_mix, g_attn], [wpool_b, scale0], wq_b)
    x1, pool_p, o0, win_b, wout_b = _prompt_layer(
        _pool_layer_kernel, "prompt_pool_layer", 0, POOL_ROW_TILE, POOL_VMEM_LIMIT, True,
        x_prompt, kb_all, vb_all, q0, ck, cv, [g_mix, g_attn, g_ffn],
        [wpool_b, scale0], [wq_b, wo_b, wup_b, wdown0_b], POOL_STATE, POOL_HALO,
        cast_along=[(w_conv_in, 0), (w_conv_out, 0)])
    xs2, wdown1_b = _sample_post(xs1, o0, g_ffn, gfin, wo_b, wup_b, wdown0_b, 0, False,
                                 "sample_post0", cast_next=(w_down, 1))

    xs3, q1, conv_s = _sample_pre(
        _sample_conv_pre_kernel, "sample_conv_pre", 1, xs2,
        state_conv[0],
        [g_mix, g_attn], [win_b, wconv0, wout_b], wq_b)
    y_prompt, conv_p, o1 = _prompt_layer(
        _conv_layer_kernel, "prompt_conv_layer", 1, CONV_ROW_TILE, CONV_VMEM_LIMIT, False,
        x1, kb_all, vb_all, q1, ck, cv, [g_mix, g_attn, g_ffn, gfin],
        [win_b, wconv0, wout_b], [wq_b, wo_b, wup_b, wdown1_b], CONV_STATE, CONV_HALO)
    (y_sample,) = _sample_post(xs3, o1, g_ffn, gfin, wo_b, wup_b, wdown1_b, 1, True, "sample_post1")

    mem_shape = (DEPTH, batch, N_MEM, MEM_HEADS, MEM_HEAD_DIM)
    return (y_prompt,
            y_sample,
            pool_p.reshape(1, batch, POOL_STATE, D_MODEL),
            conv_p.reshape(1, batch, CONV_STATE, D_MODEL),
            _unhead_rows(k_all.reshape(-1, HEAD_ROWS, LANES)).reshape(mem_shape),
            _unhead_rows(v_all.reshape(-1, HEAD_ROWS, LANES)).reshape(mem_shape),
            pool_s.transpose(1, 0, 2)[None],
            conv_s[None])
```

```python
import functools

import jax
import jax.numpy as jnp
from jax import lax
from jax.experimental import pallas as pl
from jax.experimental.pallas import tpu as pltpu

D_MODEL = 1024
DEPTH = 2
PAST_LEN = 16384
POOL_WINDOWS = (2, 4, 8, 16)
POOL_GROUP = D_MODEL // len(POOL_WINDOWS)
POOL_STATE = max(POOL_WINDOWS) - 1
CONV_WIDTH = 3
CONV_STATE = CONV_WIDTH - 1
D_FF = 4 * D_MODEL
N_MEM = 256
MEM_HEADS = 4
MEM_HEAD_DIM = D_MODEL // MEM_HEADS
EPS = 1e-6

SUBLANES = 8
LANES = 128
HEAD_ROWS = D_MODEL // LANES
POOL_HALO = 16
CONV_HALO = SUBLANES
POOL_ROW_TILE = 512
CONV_ROW_TILE = 512
SUB_TILES = 2
POOL_SUB_TILES = 1
KV_ROW_TILE = 512
FF_CHUNK = 1024
POOL_FF_CHUNK = 512
CONV_FF_CHUNK = 1024
POOL_SAMPLE_START = 4
POOL_CHUNK = 64
CONV_SAMPLE_START = 2
VMEM_LIMIT = 56 * 1024 * 1024
POOL_VMEM_LIMIT = 62 * 1024 * 1024
CONV_VMEM_LIMIT = 62 * 1024 * 1024

F32 = jnp.float32
BF16 = jnp.bfloat16

_dot = functools.partial(jnp.dot, preferred_element_type=F32)


def _rms(x, g):
    y = x * lax.rsqrt(jnp.mean(x * x, axis=-1, keepdims=True) + EPS)
    return y * g


def _vmem():
    return pl.BlockSpec(memory_space=pltpu.VMEM)


def _of_layer(w, layer):
    if w.ndim == 2:
        return pl.BlockSpec(w.shape, lambda i: (0, 0), pipeline_mode=pl.Buffered(1))
    return pl.BlockSpec((None, *w.shape[1:]), lambda i: (layer, 0, 0),
                        pipeline_mode=pl.Buffered(1))


def _gain(g_ref, layer):
    return g_ref[layer:layer + 1, :]


def _weave(*delayed):
    live = list(delayed)
    round_ = 0
    while live:
        for item in list(live):
            stages, first_round = item
            if round_ >= first_round:
                try:
                    next(stages)
                except StopIteration:
                    live.remove(item)
        round_ += 1


def _mlp(xs, g_ffn, wup_ref, wdown_ref, chunk=FF_CHUNK):
    us = [_rms(x, g_ffn).astype(BF16) for x in xs]
    yield
    accs = list(xs)
    for c in range(D_FF // chunk):
        sl = slice(c * chunk, (c + 1) * chunk)
        hs = [jnp.maximum(_dot(u, wup_ref[:, sl]), 0.0) for u in us]
        yield
        accs = [a + _dot((h * h).astype(BF16), wdown_ref[sl, :]) for a, h in zip(accs, hs)]
        yield
    return accs


def _prompt_attend(xs, g_attn, k_ref, v_ref, wq_ref, wo_ref):
    us = [_rms(x, g_attn).astype(BF16) for x in xs]
    yield
    qs = [_dot(u, wq_ref[...]) * (MEM_HEAD_DIM ** -0.5) for u in us]
    yield
    heads = [[] for _ in xs]
    for h in range(MEM_HEADS):
        sl = slice(h * MEM_HEAD_DIM, (h + 1) * MEM_HEAD_DIM)
        ss = [lax.dot_general(q[:, sl].astype(BF16), k_ref[:, sl],
                              (((1,), (1,)), ((), ())), preferred_element_type=F32) for q in qs]
        es = [jnp.exp(s - jnp.max(s, axis=-1, keepdims=True)) for s in ss]
        for i, e in enumerate(es):
            inv = 1.0 / jnp.sum(e, axis=-1, keepdims=True)
            heads[i].append((_dot(e.astype(BF16), v_ref[:, sl]) * inv).astype(BF16))
        yield
    os_ = [jnp.concatenate(hd, axis=-1) for hd in heads]
    return [x + _dot(o, wo_ref[...]) for x, o in zip(xs, os_)]


def _sample_attend(q_ref, k_ref, v_ref, o_ref):
    for j in range(q_ref.shape[0]):
        prod = k_ref[j] * q_ref[j][None]
        part = jnp.sum(prod, axis=-1, keepdims=True)
        s = part + pltpu.roll(part, MEM_HEADS, axis=1)
        e = jnp.exp(s - jnp.max(s, axis=0, keepdims=True))
        inv = 1.0 / jnp.sum(e, axis=0)
        yield
        o_ref[j] = jnp.sum(e * v_ref[j], axis=0) * inv
        yield


def _conv_tap(wconv_ref, back):
    return wconv_ref[CONV_WIDTH - 1 - back]


def _pool_project(pooled, wpool_ref, scale):
    ys = [_dot(p, wpool_ref[g]) for g, p in enumerate(pooled)]
    return jnp.concatenate(ys, axis=-1) * scale


def _carried(prev_ref, t):
    @pl.when(t == 0)
    def _():
        prev_ref[...] = jnp.zeros_like(prev_ref)

    return prev_ref[...]


def _to_head_rows(a):
    tiles = [a[:, (c % MEM_HEADS) * MEM_HEAD_DIM + (c // MEM_HEADS) * LANES:][:, :LANES]
             for c in range(HEAD_ROWS)]
    return jnp.concatenate(tiles, axis=-1).reshape(a.shape[0], HEAD_ROWS, LANES)


def _from_head_rows(a):
    flat = a.reshape(a.shape[0], D_MODEL)
    per_head = MEM_HEAD_DIM // LANES
    tiles = [flat[:, ((c % per_head) * MEM_HEADS + c // per_head) * LANES:][:, :LANES]
             for c in range(HEAD_ROWS)]
    return jnp.concatenate(tiles, axis=-1)


def _unhead_rows(a):
    n = a.shape[0]
    a = a.reshape(n, MEM_HEAD_DIM // LANES, MEM_HEADS, LANES)
    return a.transpose(0, 2, 1, 3).reshape(n, D_MODEL)


def _cache_rows(c):
    depth, n = c.shape[:2]
    c = c.reshape(depth, n, N_MEM, MEM_HEADS, MEM_HEAD_DIM // LANES, LANES)
    return c.transpose(0, 1, 2, 4, 3, 5).reshape(depth, n, N_MEM, HEAD_ROWS, LANES)


def _memkv_kernel(m_ref, g_ref, w_ref, wup_ref, k_ref, v_ref, kb_ref, vb_ref, wupb_ref, wb_ref):
    @pl.when(pl.program_id(1) == 0)
    def _():
        wb_ref[...] = w_ref[...].astype(BF16)

    wupb_ref[...] = wup_ref[...].astype(BF16)

    u = _rms(m_ref[...], g_ref[pl.ds(pl.program_id(0), 1), :]).astype(BF16)
    kv = _dot(u, wb_ref[...])
    k = kv[:, :D_MODEL]
    v = kv[:, D_MODEL:]
    k_ref[...] = _to_head_rows(k)
    v_ref[...] = _to_head_rows(v)
    kb_ref[...] = k.astype(BF16)
    vb_ref[...] = v.astype(BF16)


def _mem_kv(mem, g_mem, w_kv, w_up):
    rows = mem.shape[0]
    row_tiles = rows // KV_ROW_TILE
    slab = D_MODEL // row_tiles
    assert slab * row_tiles == D_MODEL
    out = jax.ShapeDtypeStruct((DEPTH, rows, HEAD_ROWS, LANES), F32)
    outb = jax.ShapeDtypeStruct((DEPTH, rows, D_MODEL), BF16)
    hspec = pl.BlockSpec((None, KV_ROW_TILE, HEAD_ROWS, LANES), lambda l, r: (l, r, 0, 0))
    ospec = pl.BlockSpec((None, KV_ROW_TILE, D_MODEL), lambda l, r: (l, r, 0))
    wspec = pl.BlockSpec((None, slab, D_FF), lambda l, r: (l, r, 0))
    return pl.pallas_call(
        _memkv_kernel,
        grid=(DEPTH, row_tiles),
        in_specs=[
            pl.BlockSpec((KV_ROW_TILE, D_MODEL), lambda l, r: (r, 0)),
            _vmem(),
            pl.BlockSpec((None, D_MODEL, 2 * D_MODEL), lambda l, r: (l, 0, 0)),
            wspec,
        ],
        out_specs=[hspec, hspec, ospec, ospec, wspec],
        out_shape=[out, out, outb, outb, jax.ShapeDtypeStruct(w_up.shape, BF16)],
        scratch_shapes=[pltpu.VMEM((D_MODEL, 2 * D_MODEL), BF16)],
        compiler_params=pltpu.CompilerParams(
            dimension_semantics=("arbitrary", "arbitrary"), vmem_limit_bytes=VMEM_LIMIT),
        name="mem_kv",
    )(mem, g_mem, w_kv, w_up)


def _pipelined_step(start, finish, carried, sample):
    step = pl.program_id(0)
    last = pl.num_programs(0) - 1

    def starts():
        halos = [carried]
        return [(start(i, halos), 0) for i in range(POOL_SUB_TILES)]

    finishes = lambda: [(finish(i), 0) for i in range(POOL_SUB_TILES)]

    @pl.when(step == 0)
    def _():
        _weave(*starts(), sample())

    @pl.when(jnp.logical_and(step > 0, step < last))
    def _():
        _weave(*finishes(), *starts(), sample())

    @pl.when(step == last)
    def _():
        _weave(*finishes())


def _started_tile(tiles_per_seq):
    step = jnp.minimum(pl.program_id(0), pl.num_programs(0) - 2)
    return lax.rem(step, tiles_per_seq)


def _pool_layer_kernel(tiles_per_seq, layer, x_ref, gm_ref, ga_ref, gf_ref, k_ref, v_ref, sq_ref,
                       sk_ref, sv_ref, wpool_ref, scale_ref, wq_ref, wo_ref, wup_ref, wdown_ref,
                       win_ref, wout_ref, y_ref, state_ref, so_ref, winb_ref, woutb_ref,
                       prev_ref, mid_ref):
    t = _started_tile(tiles_per_seq)
    tm = x_ref.shape[0]
    rows = tm // POOL_SUB_TILES
    carried = _carried(prev_ref, t)
    winb_ref[...] = win_ref[...].astype(BF16)
    woutb_ref[...] = wout_ref[...].astype(BF16)

    def start(i, halos):
        rs = slice(i * rows, (i + 1) * rows)
        x = x_ref[rs, :]
        u = _rms(x, _gain(gm_ref, layer))
        halos.append(u[rows - POOL_HALO:])
        if i == POOL_SUB_TILES - 1:
            prev_ref[...] = u[rows - POOL_HALO:]
            state_ref[...] = u[rows - POOL_STATE:]
        yield
        ext = jnp.concatenate([halos[i], u], axis=0)
        x1 = []
        for lo in range(0, rows, POOL_CHUNK):
            hi = lo + POOL_CHUNK
            pos = t * tm + i * rows + lo + lax.broadcasted_iota(jnp.int32, (POOL_CHUNK, 1), 0)
            pooled = []
            for g, w in enumerate(POOL_WINDOWS):
                sl = slice(g * POOL_GROUP, (g + 1) * POOL_GROUP)
                s = ext[lo:hi + POOL_HALO, sl]
                shift = 1
                while shift < w:
                    s = s + pltpu.roll(s, shift, axis=0)
                    shift *= 2
                inv_cnt = 1.0 / jnp.minimum(pos + 1, w).astype(F32)
                pooled.append((s[POOL_HALO:] * inv_cnt - u[lo:hi, sl]).astype(BF16))
            x1.append(x[lo:hi] + _pool_project(pooled, wpool_ref, scale_ref[...]))
            yield
        x1 = jnp.concatenate(x1, axis=0)
        (x2,) = yield from _prompt_attend([x1], _gain(ga_ref, layer), k_ref, v_ref, wq_ref, wo_ref)
        mid_ref[rs, :] = x2

    def finish(i):
        rs = slice(i * rows, (i + 1) * rows)
        x2 = mid_ref[rs, :]
        (y,) = yield from _mlp([x2], _gain(gf_ref, layer), wup_ref, wdown_ref, POOL_FF_CHUNK)
        y_ref[rs, :] = y

    _pipelined_step(start, finish, carried,
                    lambda: (_sample_attend(sq_ref, sk_ref, sv_ref, so_ref), POOL_SAMPLE_START))


def _conv_layer_kernel(tiles_per_seq, layer, x_ref, gm_ref, ga_ref, gf_ref, gfin_ref, k_ref, v_ref,
                       sq_ref, sk_ref, sv_ref, win_ref, wconv_ref, wout_ref, wq_ref, wo_ref,
                       wup_ref, wdown_ref, y_ref, state_ref, so_ref, prev_ref):
    t = lax.rem(pl.program_id(0), tiles_per_seq)
    rows = x_ref.shape[0] // SUB_TILES
    halos = [_carried(prev_ref, t)]

    def chain(i):
        rs = slice(i * rows, (i + 1) * rows)
        x = x_ref[rs, :]
        u = _rms(x, _gain(gm_ref, layer)).astype(BF16)
        yield
        gate_b = _dot(u, win_ref[:, :D_MODEL])
        yield
        gate_c = _dot(u, win_ref[:, D_MODEL:2 * D_MODEL])
        yield
        ch = gate_c * _dot(u, win_ref[:, 2 * D_MODEL:])
        halos.append(ch[rows - CONV_HALO:])
        if i == SUB_TILES - 1:
            prev_ref[...] = ch[rows - CONV_HALO:]
            state_ref[...] = ch[rows - CONV_STATE:]
        yield
        ext = jnp.concatenate([halos[i], ch], axis=0)
        conv = ext * _conv_tap(wconv_ref, 0)
        for back in range(1, CONV_WIDTH):
            conv = conv + pltpu.roll(ext, back, axis=0) * _conv_tap(wconv_ref, back)
        conv = conv[CONV_HALO:]
        x1 = x + _dot((gate_b * conv).astype(BF16), wout_ref[...])
        yield
        x2 = yield from _prompt_attend([x1], _gain(ga_ref, layer), k_ref, v_ref, wq_ref, wo_ref)
        (x3,) = yield from _mlp(x2, _gain(gf_ref, layer), wup_ref, wdown_ref, CONV_FF_CHUNK)
        y_ref[rs, :] = _rms(x3, gfin_ref[...])

    _weave(*[(chain(i), 0) for i in range(SUB_TILES)],
           (_sample_attend(sq_ref, sk_ref, sv_ref, so_ref), CONV_SAMPLE_START))


def _prompt_layer(kernel, name, layer, row_tile, vmem_limit, pipelined, x, kb, vb, sq, sk, sv,
                  small, weights, stacked, state_rows, halo, cast_along=()):
    batch, seq, _ = x.shape
    tiles_per_seq = seq // row_tile
    n_tiles = batch * tiles_per_seq
    n_sample = sq.shape[0]
    per_step = n_sample // n_tiles
    assert seq % row_tile == 0 and row_tile // SUB_TILES >= POOL_HALO
    assert per_step * n_tiles == n_sample

    started = (lambda i: jnp.minimum(i, n_tiles - 1)) if pipelined else (lambda i: i)
    finished = (lambda i: jnp.maximum(i - 1, 0)) if pipelined else (lambda i: i)
    seq_of = lambda j: lax.div(j, tiles_per_seq)
    tile_of = lambda j: lax.rem(j, tiles_per_seq)

    xspec = pl.BlockSpec((None, row_tile, D_MODEL),
                         lambda i: (seq_of(started(i)), tile_of(started(i)), 0))
    yspec = pl.BlockSpec((None, row_tile, D_MODEL),
                         lambda i: (seq_of(finished(i)), tile_of(finished(i)), 0))
    stspec = pl.BlockSpec((None, state_rows, D_MODEL), lambda i: (seq_of(started(i)), 0, 0))
    kvspec = pl.BlockSpec((None, None, N_MEM, D_MODEL),
                          lambda i: (layer, seq_of(started(i)), 0, 0),
                          pipeline_mode=pl.Buffered(1))
    sqspec = pl.BlockSpec((per_step, HEAD_ROWS, LANES), lambda i: (started(i), 0, 0))
    skvspec = pl.BlockSpec((None, per_step, N_MEM, HEAD_ROWS, LANES),
                           lambda i: (layer, started(i), 0, 0, 0))
    def slab_of(a, layer_of_a):
        return pl.BlockSpec((None, a.shape[1] // n_tiles, a.shape[2]),
                            lambda i: (layer_of_a, started(i), 0))

    slabs_in = [slab_of(a, la) for a, la in cast_along]
    slabs = [pl.BlockSpec((a.shape[1] // n_tiles, a.shape[2]), lambda i: (started(i), 0))
             for a, _ in cast_along]
    scratch = [pltpu.VMEM((halo, D_MODEL), F32)]
    if pipelined:
        scratch.append(pltpu.VMEM((row_tile, D_MODEL), F32))
    return pl.pallas_call(
        functools.partial(kernel, tiles_per_seq, layer),
        grid=(n_tiles + 1 if pipelined else n_tiles,),
        in_specs=([xspec] + [_vmem()] * len(small) + [kvspec, kvspec, sqspec, skvspec, skvspec]
                  + [_vmem()] * len(weights) + [_of_layer(w, layer) for w in stacked] + slabs_in),
        out_specs=[yspec, stspec, sqspec] + slabs,
        out_shape=[jax.ShapeDtypeStruct(x.shape, F32),
                   jax.ShapeDtypeStruct((batch, state_rows, D_MODEL), F32),
                   jax.ShapeDtypeStruct(sq.shape, F32)]
                  + [jax.ShapeDtypeStruct(a.shape[1:], BF16) for a, _ in cast_along],
        scratch_shapes=scratch,
        compiler_params=pltpu.CompilerParams(
            dimension_semantics=("arbitrary",), vmem_limit_bytes=vmem_limit),
        name=name,
    )(x, *small, kb, vb, sq, sk, sv, *weights, *stacked, *[a for a, _ in cast_along])


def _sample_query(x, g_attn, wq_ref):
    q = _dot(_rms(x, g_attn).astype(BF16), wq_ref[...]) * (MEM_HEAD_DIM ** -0.5)
    return _to_head_rows(q)


def _sample_pool_pre_kernel(layer, x_ref, st_ref, gm_ref, ga_ref, wpool_ref, scale_ref, wq_ref,
                            x1_ref, q_ref, newst_ref):
    x = x_ref[:, 0, :]
    u = _rms(x, _gain(gm_ref, layer))
    pooled = []
    for g, w in enumerate(POOL_WINDOWS):
        s = u[:, g * POOL_GROUP:(g + 1) * POOL_GROUP]
        for back in range(1, w):
            s = s + st_ref[POOL_STATE - back, :, g * POOL_GROUP:(g + 1) * POOL_GROUP]
        cnt = min(PAST_LEN + 1, w)
        pooled.append((s * (1.0 / cnt) - u[:, g * POOL_GROUP:(g + 1) * POOL_GROUP]).astype(BF16))
    x = x + _pool_project(pooled, wpool_ref, scale_ref[...])
    x1_ref[...] = x
    q_ref[...] = _sample_query(x, _gain(ga_ref, layer), wq_ref)
    newst_ref[:POOL_STATE - 1] = st_ref[1:]
    newst_ref[POOL_STATE - 1] = u


def _sample_conv_pre_kernel(layer, x_ref, st_ref, gm_ref, ga_ref, win_ref, wconv_ref, wout_ref,
                            wq_ref, x1_ref, q_ref, newst_ref):
    x = x_ref[...]
    u = _rms(x, _gain(gm_ref, layer)).astype(BF16)
    gate_b = _dot(u, win_ref[:, :D_MODEL])
    ch = _dot(u, win_ref[:, D_MODEL:2 * D_MODEL]) * _dot(u, win_ref[:, 2 * D_MODEL:])
    conv = ch * _conv_tap(wconv_ref, 0)
    for back in range(1, CONV_WIDTH):
        conv = conv + st_ref[:, CONV_STATE - back, :] * _conv_tap(wconv_ref, back)
    x = x + _dot((gate_b * conv).astype(BF16), wout_ref[...])
    x1_ref[...] = x
    q_ref[...] = _sample_query(x, _gain(ga_ref, layer), wq_ref)
    for age in range(CONV_STATE - 1):
        newst_ref[:, age, :] = st_ref[:, age + 1, :]
    newst_ref[:, CONV_STATE - 1, :] = ch


def _sample_pre(kernel, name, layer, x, state, small, weights, w_q):
    n = x.shape[0]
    return pl.pallas_call(
        functools.partial(kernel, layer),
        grid=(1,),
        in_specs=[_vmem()] * (2 + len(small) + len(weights)) + [_of_layer(w_q, layer)],
        out_specs=[_vmem()] * 3,
        out_shape=[jax.ShapeDtypeStruct((n, D_MODEL), F32),
                   jax.ShapeDtypeStruct((n, HEAD_ROWS, LANES), F32),
                   jax.ShapeDtypeStruct(state.shape, F32)],
        compiler_params=pltpu.CompilerParams(
            dimension_semantics=("arbitrary",), vmem_limit_bytes=VMEM_LIMIT),
        name=name,
    )(x, state, *small, *weights, w_q)


def _sample_post_kernel(final, layer, x_ref, o_ref, gf_ref, gfin_ref, wo_ref, wup_ref, wdown_ref,
                        *rest):
    if len(rest) == 5:
        wnext_ref, y_ref, wnextb_ref, u_ref, acc_ref = rest
        wnextb_ref[...] = wnext_ref[...].astype(BF16)
    else:
        y_ref, u_ref, acc_ref = rest
    c = pl.program_id(0)

    @pl.when(c == 0)
    def _():
        x = x_ref[...] + _dot(_from_head_rows(o_ref[...]).astype(BF16), wo_ref[...])
        u_ref[...] = _rms(x, _gain(gf_ref, layer)).astype(BF16)
        acc_ref[...] = x

    h = jnp.maximum(_dot(u_ref[...], wup_ref[...]), 0.0)
    acc_ref[...] += _dot((h * h).astype(BF16), wdown_ref[...])

    @pl.when(c == pl.num_programs(0) - 1)
    def _():
        x = acc_ref[...]
        if final:
            y_ref[:, 0, :] = _rms(x, gfin_ref[...])
        else:
            y_ref[...] = x


def _sample_post(x1, o, g_ffn, g_final, wo, wup, wdown, layer, final, name, cast_next=None):
    n = x1.shape[0]
    whole = lambda shape: pl.BlockSpec(shape, lambda c: (0,) * len(shape))
    act = whole((n, D_MODEL))
    chunk_rows = pl.BlockSpec((FF_CHUNK, D_MODEL), lambda c: (c, 0))
    y_shape = (n, 1, D_MODEL) if final else (n, D_MODEL)
    in_specs = [act, whole((n, HEAD_ROWS, LANES)), whole(g_ffn.shape), whole(g_final.shape),
                pl.BlockSpec((None, D_MODEL, D_MODEL), lambda c: (layer, 0, 0)),
                pl.BlockSpec((None, D_MODEL, FF_CHUNK), lambda c: (layer, 0, c)),
                chunk_rows]
    operands = [x1, o, g_ffn, g_final, wo, wup, wdown]
    out_specs = [whole(y_shape)]
    out_shape = [jax.ShapeDtypeStruct(y_shape, F32)]
    if cast_next is not None:
        stack, next_layer = cast_next
        in_specs.append(pl.BlockSpec((None, FF_CHUNK, D_MODEL), lambda c: (next_layer, c, 0)))
        operands.append(stack)
        out_specs.append(chunk_rows)
        out_shape.append(jax.ShapeDtypeStruct(stack.shape[1:], BF16))
    return pl.pallas_call(
        functools.partial(_sample_post_kernel, final, layer),
        grid=(D_FF // FF_CHUNK,),
        in_specs=in_specs,
        out_specs=out_specs,
        out_shape=out_shape,
        scratch_shapes=[pltpu.VMEM((n, D_MODEL), BF16), pltpu.VMEM((n, D_MODEL), F32)],
        compiler_params=pltpu.CompilerParams(
            dimension_semantics=("arbitrary",), vmem_limit_bytes=VMEM_LIMIT),
        name=name,
    )(*operands)


def kernel(x_prompt, x_sample, state_pool, state_conv, cache_mem_k, cache_mem_v, mem_prompt,
           g_mix, g_attn, g_mem, g_ffn, g_final, w_pool, pool_scale,
           w_conv_in, w_conv, w_conv_out, w_q, w_kv, w_o, w_up, w_down):
    batch, seq, _ = x_prompt.shape
    n_sample = x_sample.shape[0]
    assert DEPTH == 2 and x_sample.shape[1] == 1
    assert (batch * N_MEM) % KV_ROW_TILE == 0

    cast = lambda a: a.astype(BF16)
    wpool_b = cast(w_pool[0])
    wq_b, wo_b, wdown0_b = cast(w_q), cast(w_o), cast(w_down[0])
    scale0, gfin = pool_scale[0].reshape(1, D_MODEL), g_final.reshape(1, D_MODEL)
    wconv0 = w_conv.transpose(1, 0, 2)

    k_all, v_all, kb_all, vb_all, wup_b = _mem_kv(
        mem_prompt.reshape(batch * N_MEM, D_MODEL), g_mem, w_kv, w_up)
    kb_all = kb_all.reshape(DEPTH, batch, N_MEM, D_MODEL)
    vb_all = vb_all.reshape(DEPTH, batch, N_MEM, D_MODEL)
    ck, cv = _cache_rows(cache_mem_k), _cache_rows(cache_mem_v)

    xs1, q0, pool_s = _sample_pre(
        _sample_pool_pre_kernel, "sample_pool_pre", 0, x_sample,
        state_pool[0].transpose(1, 0, 2),
        [g_mix, g_attn], [wpool_b, scale0], wq_b)
    x1, pool_p, o0, win_b, wout_b = _prompt_layer(
        _pool_layer_kernel, "prompt_pool_layer", 0, POOL_ROW_TILE, POOL_VMEM_LIMIT, True,
        x_prompt, kb_all, vb_all, q0, ck, cv, [g_mix, g_attn, g_ffn],
        [wpool_b, scale0], [wq_b, wo_b, wup_b, wdown0_b], POOL_STATE, POOL_HALO,
        cast_along=[(w_conv_in, 0), (w_conv_out, 0)])
    xs2, wdown1_b = _sample_post(xs1, o0, g_ffn, gfin, wo_b, wup_b, wdown0_b, 0, False,
                                 "sample_post0", cast_next=(w_down, 1))

    xs3, q1, conv_s = _sample_pre(
        _sample_conv_pre_kernel, "sample_conv_pre", 1, xs2,
        state_conv[0],
        [g_mix, g_attn], [win_b, wconv0, wout_b], wq_b)
    y_prompt, conv_p, o1 = _prompt_layer(
        _conv_layer_kernel, "prompt_conv_layer", 1, CONV_ROW_TILE, CONV_VMEM_LIMIT, False,
        x1, kb_all, vb_all, q1, ck, cv, [g_mix, g_attn, g_ffn, gfin],
        [win_b, wconv0, wout_b], [wq_b, wo_b, wup_b, wdown1_b], CONV_STATE, CONV_HALO)
    (y_sample,) = _sample_post(xs3, o1, g_ffn, gfin, wo_b, wup_b, wdown1_b, 1, True, "sample_post1")

    mem_shape = (DEPTH, batch, N_MEM, MEM_HEADS, MEM_HEAD_DIM)
    return (y_prompt,
            y_sample,
            pool_p.reshape(1, batch, POOL_STATE, D_MODEL),
            conv_p.reshape(1, batch, CONV_STATE, D_MODEL),
            _unhead_rows(k_all.reshape(-1, HEAD_ROWS, LANES)).reshape(mem_shape),
            _unhead_rows(v_all.reshape(-1, HEAD_ROWS, LANES)).reshape(mem_shape),
            pool_s.transpose(1, 0, 2)[None],
            conv_s[None])
```
